```python
import jax, jax.numpy as jnp
from jax import lax
import numpy as np

D_MODEL = 1024
BATCH = 8
SEQ = 4096
DEPTH = 1

N_META = 16
LRU_WIDTH = D_MODEL // 2
LRU_HEADS = 8
LRU_HEAD_DIM = LRU_WIDTH // LRU_HEADS
CONV_WIDTH = 4
LRU_C = 8.0
POOL_WIDTH = D_MODEL - LRU_WIDTH
POOL_WINDOWS = (2, 4, 8, 16)
POOL_GROUP = POOL_WIDTH // len(POOL_WINDOWS)
MIX_WIDTH = LRU_WIDTH + POOL_WIDTH
IN_WIDTH = 2 * LRU_WIDTH + POOL_WIDTH
N_GROUPS = 4
EXPERTS_PER_GROUP = 8
N_EXPERTS = N_GROUPS * EXPERTS_PER_GROUP
TOP_K = 2
EXPERT_FF = D_MODEL // 2
MOE_BLOCK = 128
RMS_EPS = 1e-6

kernel_name = "hymba_rglru_multipool_hmoe_layer"


def rmsnorm(x, gain):
    xf = x.astype(jnp.float32)
    y = xf * lax.rsqrt(jnp.mean(xf * xf, axis=-1, keepdims=True) + RMS_EPS)
    return (y * gain.astype(jnp.float32)).astype(x.dtype)


def rg_lru_group(u_x, u_gate, conv_w, conv_b, wa, ba, wx, bx, lam):
    B, T, _ = u_x.shape
    xpad = jnp.pad(u_x, ((0, 0), (CONV_WIDTH - 1, 0), (0, 0)))
    xc = conv_b + sum(xpad[:, k:k + T] * conv_w[k] for k in range(CONV_WIDTH))
    xf = xc.astype(jnp.float32)
    xh = xf.reshape(B, T, LRU_HEADS, LRU_HEAD_DIM)
    r = jax.nn.sigmoid(jnp.einsum('bthi,hij->bthj', xh, wa.astype(jnp.float32)).reshape(B, T, LRU_WIDTH)
                       + ba.astype(jnp.float32))
    i = jax.nn.sigmoid(jnp.einsum('bthi,hij->bthj', xh, wx.astype(jnp.float32)).reshape(B, T, LRU_WIDTH)
                       + bx.astype(jnp.float32))
    log_a = LRU_C * r * jax.nn.log_sigmoid(lam.astype(jnp.float32))
    a = jnp.exp(log_a)
    b = jnp.sqrt(-jnp.expm1(2.0 * log_a)) * (i * xf)

    def combine(left, right):
        a1, b1 = left
        a2, b2 = right
        return a1 * a2, a2 * b1 + b2

    _, h = lax.associative_scan(combine, (a, b), axis=1)
    return h * jax.nn.gelu(u_gate.astype(jnp.float32))


def multiscale_pool_group(u, pool_w):
    uf = u.astype(jnp.float32)
    T = uf.shape[1]
    count_base = jnp.arange(1, T + 1, dtype=jnp.float32)[None, :, None]
    outs = []
    for g, w in enumerate(POOL_WINDOWS):
        ug = uf[..., g * POOL_GROUP:(g + 1) * POOL_GROUP]
        cs = jnp.cumsum(ug, axis=1)
        lag = jnp.pad(cs[:, :T - w], ((0, 0), (w, 0), (0, 0)))
        mean = (cs - lag) / jnp.minimum(count_base, float(w))
        outs.append(jnp.einsum('btc,cd->btd', mean - ug, pool_w[g].astype(jnp.float32)))
    return jnp.concatenate(outs, axis=-1)


def hierarchical_moe(h, w_group, b_group, w_router, b_router, w_gate, w_up, w_down):
    B, T, D = h.shape
    N = B * T
    xt = h.reshape(N, D)
    g_logits = (xt @ w_group).astype(jnp.float32) + b_group.astype(jnp.float32)
    p_groups = jax.nn.softmax(g_logits, axis=-1)
    g_idx = jnp.argmax(g_logits, axis=-1).astype(jnp.int32)
    p_g = jnp.take_along_axis(p_groups, g_idx[:, None], axis=-1)
    e_logits = ((xt @ w_router).astype(jnp.float32) + b_router.astype(jnp.float32)
                ).reshape(N, N_GROUPS, EXPERTS_PER_GROUP)
    e_sel = jnp.take_along_axis(e_logits, g_idx[:, None, None], axis=1)[:, 0]
    top_v, top_i = lax.top_k(e_sel, TOP_K)
    gate = p_g * jax.nn.softmax(top_v, axis=-1)
    eid = g_idx[:, None] * EXPERTS_PER_GROUP + top_i.astype(jnp.int32)

    A = N * TOP_K
    flat_e = eid.reshape(A)
    flat_tok = jnp.repeat(jnp.arange(N, dtype=jnp.int32), TOP_K)
    flat_gate = gate.reshape(A)
    order = jnp.argsort(flat_e)
    se, stok, sg = flat_e[order], flat_tok[order], flat_gate[order]
    counts = jnp.bincount(flat_e, length=N_EXPERTS)
    starts = jnp.cumsum(counts) - counts
    padded = ((counts + MOE_BLOCK - 1) // MOE_BLOCK) * MOE_BLOCK
    pends = jnp.cumsum(padded)
    pstarts = pends - padded
    dest = pstarts[se] + (jnp.arange(A, dtype=jnp.int32) - starts[se])
    n_blocks = -(-(A + N_EXPERTS * (MOE_BLOCK - 1)) // MOE_BLOCK)
    P = n_blocks * MOE_BLOCK
    buf = jnp.zeros((P, D), h.dtype).at[dest].set(xt[stok])
    block_e = jnp.clip(jnp.searchsorted(pends, jnp.arange(n_blocks) * MOE_BLOCK, side='right'),
                       0, N_EXPERTS - 1).astype(jnp.int32)

    def expert_block(args):
        xb, e = args
        hid = jax.nn.silu(xb @ w_gate[e]) * (xb @ w_up[e])
        return hid @ w_down[e]

    out = lax.map(expert_block, (buf.reshape(n_blocks, MOE_BLOCK, D), block_e)).reshape(P, D)
    contrib = (out[dest].astype(jnp.float32) * sg[:, None]).astype(h.dtype)
    y = jnp.zeros((N, D), h.dtype).at[stok].add(contrib)
    return y.reshape(B, T, D)


def setup_inputs(seed: int = 0) -> dict:
    key = jax.random.key(seed)
    ks = jax.random.split(key, 24)
    f32 = jnp.float32

    def nrm(k, shape, scale):
        return jax.random.normal(k, shape, f32) * scale

    u = jax.random.uniform(ks[10], (DEPTH, LRU_WIDTH), f32, minval=0.9, maxval=0.999)
    s = u ** (1.0 / LRU_C)
    lam = jnp.log(s) - jnp.log1p(-s)
    return {
        "x": nrm(ks[0], (BATCH, SEQ, D_MODEL), 1.0),
        "meta_tokens": nrm(ks[1], (N_META, D_MODEL), 1.0),
        "norm1_gain": 1.0 + nrm(ks[2], (DEPTH, D_MODEL), 0.05),
        "w_in": nrm(ks[3], (DEPTH, D_MODEL, IN_WIDTH), D_MODEL ** -0.5),
        "conv_w": nrm(ks[4], (DEPTH, CONV_WIDTH, LRU_WIDTH), CONV_WIDTH ** -0.5),
        "conv_b": nrm(ks[5], (DEPTH, LRU_WIDTH), 0.02),
        "lru_wa": nrm(ks[6], (DEPTH, LRU_HEADS, LRU_HEAD_DIM, LRU_HEAD_DIM), LRU_HEAD_DIM ** -0.5),
        "lru_ba": nrm(ks[7], (DEPTH, LRU_WIDTH), 0.02),
        "lru_wx": nrm(ks[8], (DEPTH, LRU_HEADS, LRU_HEAD_DIM, LRU_HEAD_DIM), LRU_HEAD_DIM ** -0.5),
        "lru_bx": nrm(ks[9], (DEPTH, LRU_WIDTH), 0.02),
        "lru_lambda": lam,
        "lru_out_gain": 1.0 + nrm(ks[11], (DEPTH, LRU_WIDTH), 0.05),
        "pool_w": nrm(ks[12], (DEPTH, len(POOL_WINDOWS), POOL_GROUP, POOL_GROUP), POOL_GROUP ** -0.5),
        "pool_scale": 1.0 + nrm(ks[13], (DEPTH, POOL_WIDTH), 0.1),
        "w_out": nrm(ks[14], (DEPTH, MIX_WIDTH, D_MODEL), MIX_WIDTH ** -0.5),
        "norm2_gain": 1.0 + nrm(ks[15], (DEPTH, D_MODEL), 0.05),
        "w_group": nrm(ks[16], (DEPTH, D_MODEL, N_GROUPS), D_MODEL ** -0.5),
        "b_group": nrm(ks[17], (DEPTH, N_GROUPS), 0.01),
        "w_router": nrm(ks[18], (DEPTH, D_MODEL, N_EXPERTS), D_MODEL ** -0.5),
        "b_router": nrm(ks[19], (DEPTH, N_EXPERTS), 0.01),
        "w_gate": nrm(ks[20], (DEPTH, N_EXPERTS, D_MODEL, EXPERT_FF), D_MODEL ** -0.5),
        "w_up": nrm(ks[21], (DEPTH, N_EXPERTS, D_MODEL, EXPERT_FF), D_MODEL ** -0.5),
        "w_down": nrm(ks[22], (DEPTH, N_EXPERTS, EXPERT_FF, D_MODEL), EXPERT_FF ** -0.5),
        "final_gain": 1.0 + nrm(ks[23], (D_MODEL,), 0.05),
    }


def reference(x, meta_tokens, norm1_gain, w_in, conv_w, conv_b, lru_wa, lru_ba, lru_wx, lru_bx,
              lru_lambda, lru_out_gain, pool_w, pool_scale, w_out, norm2_gain, w_group, b_group,
              w_router, b_router, w_gate, w_up, w_down, final_gain):
    B = x.shape[0]
    meta = jnp.broadcast_to(meta_tokens[None].astype(x.dtype), (B, N_META, D_MODEL))
    h = jnp.concatenate([meta, x], axis=1)
    for l in range(DEPTH):
        hn = rmsnorm(h, norm1_gain[l])
        proj = hn @ w_in[l]
        u_x = proj[..., :LRU_WIDTH]
        u_g = proj[..., LRU_WIDTH:2 * LRU_WIDTH]
        u_p = proj[..., 2 * LRU_WIDTH:]
        y_lru = rmsnorm(rg_lru_group(u_x, u_g, conv_w[l], conv_b[l], lru_wa[l], lru_ba[l],
                                     lru_wx[l], lru_bx[l], lru_lambda[l]), lru_out_gain[l])
        y_pool = rmsnorm(multiscale_pool_group(u_p, pool_w[l]), pool_scale[l])
        y = jnp.concatenate([y_lru, y_pool], axis=-1).astype(h.dtype) @ w_out[l]
        h = h + y
        hn = rmsnorm(h, norm2_gain[l])
        h = h + hierarchical_moe(hn, w_group[l], b_group[l], w_router[l], b_router[l],
                                 w_gate[l], w_up[l], w_down[l])
    h = rmsnorm(h, final_gain)
    return h[:, N_META:, :]
```

```python
import functools

import jax
import jax.numpy as jnp
from jax import lax
from jax.experimental import pallas as pl
from jax.experimental.pallas import tpu as pltpu

D_MODEL = 1024
BATCH = 8
SEQ = 4096
N_META = 16
LRU_WIDTH = 512
LRU_HEADS = 8
LRU_HEAD_DIM = 64
CONV_WIDTH = 4
LRU_C = 8.0
POOL_WIDTH = 512
POOL_WINDOWS = (2, 4, 8, 16)
POOL_GROUP = 128
N_GROUPS = 4
EXPERTS_PER_GROUP = 8
N_EXPERTS = 32
TOP_K = 2
EXPERT_FF = 512
RMS_EPS = 1e-6

N_TOK = BATCH * SEQ
SUBLANES = 8
LANES = 128
MIX_ROWS = 512
META_ROWS = N_META * BATCH
CONV_HALO = (CONV_WIDTH - 1) * BATCH
POOL_HALO = (max(POOL_WINDOWS) - 1) * BATCH
MOE_BM = 256
N_ASSIGN = N_TOK * TOP_K
N_BLOCKS = -(-(N_ASSIGN + N_EXPERTS * (MOE_BM - 1)) // MOE_BM)
P_ROWS = N_BLOCKS * MOE_BM
CMB_TM = 256
VMEM_LIMIT = 56 * 1024 * 1024


def _rms(x, gain):
    return x * lax.rsqrt(jnp.mean(x * x, axis=-1, keepdims=True) + RMS_EPS) * gain


def _bdot(a, b):
    return jnp.dot(a.astype(jnp.bfloat16), b, preferred_element_type=jnp.float32)


def _mix_rows(x, nrows, with_output, g1_ref, w_in_ref, conv_w_ref, conv_b_ref, w_gate_ref,
              b_gate_ref, lam8_ref, lru_gain_ref, w_pool_ref, pool_scale_ref, w_out_ref,
              ux_buf, up_buf, a_buf, b_buf, hs_buf, hstate):
    hn = _rms(x, g1_ref[...])
    proj = _bdot(hn, w_in_ref[...])
    ux = proj[:, :LRU_WIDTH]
    up = proj[:, 2 * LRU_WIDTH:]

    ux_buf[CONV_HALO:CONV_HALO + nrows, :] = ux
    xc = conv_b_ref[...] + conv_w_ref[3:4, :] * ux
    for k in range(CONV_WIDTH - 1):
        xc = xc + conv_w_ref[k:k + 1, :] * ux_buf[k * BATCH:k * BATCH + nrows, :]
    ux_buf[0:CONV_HALO, :] = ux_buf[nrows:nrows + CONV_HALO, :]

    half = LRU_WIDTH // 2
    z = [_bdot(xc[:, j * half:(j + 1) * half], w_gate_ref[j]) for j in range(2)]
    za = jnp.concatenate([z[0][:, :half], z[1][:, :half]], axis=-1) + b_gate_ref[0:1, :]
    zx = jnp.concatenate([z[0][:, half:], z[1][:, half:]], axis=-1) + b_gate_ref[1:2, :]
    r = jax.nn.sigmoid(za)
    gi = jax.nn.sigmoid(zx)
    log_a = lam8_ref[...] * r
    a = jnp.exp(log_a)
    a_buf[0:nrows, :] = a
    b_buf[0:nrows, :] = jnp.sqrt(1.0 - a * a) * (gi * xc)

    def step(t, h):
        r0 = pl.multiple_of(t * SUBLANES, SUBLANES)
        h = a_buf[pl.ds(r0, SUBLANES), :] * h + b_buf[pl.ds(r0, SUBLANES), :]
        hs_buf[pl.ds(r0, SUBLANES), :] = h
        return h

    hstate[...] = lax.fori_loop(0, nrows // SUBLANES, step, hstate[...], unroll=8)

    up_buf[POOL_HALO:POOL_HALO + nrows, :] = up
    if not with_output:
        up_buf[0:POOL_HALO, :] = up_buf[nrows:nrows + POOL_HALO, :]
        return None

    ug = proj[:, LRU_WIDTH:2 * LRU_WIDTH]
    y_lru = _rms(hs_buf[0:nrows, :] * jax.nn.gelu(ug), lru_gain_ref[...])

    pm = []
    for g, w in enumerate(POOL_WINDOWS):
        lo, hi = g * POOL_GROUP, (g + 1) * POOL_GROUP
        cur = up[:, lo:hi]
        s = cur
        for j in range(1, w):
            off = POOL_HALO - j * BATCH
            s = s + up_buf[off:off + nrows, lo:hi]
        pm.append(s * (1.0 / w) - cur)
    up_buf[0:POOL_HALO, :] = up_buf[nrows:nrows + POOL_HALO, :]
    yp = [_bdot(jnp.concatenate(pm[2 * j:2 * j + 2], axis=-1), w_pool_ref[j]) for j in range(2)]
    y_pool = _rms(jnp.concatenate(yp, axis=-1), pool_scale_ref[...])

    y = _bdot(y_lru, w_out_ref[0:LRU_WIDTH, :]) + _bdot(y_pool, w_out_ref[LRU_WIDTH:, :])
    return x + y


def _mixer_kernel(x_ref, meta_ref, g1_ref, w_in_ref, conv_w_ref, conv_b_ref, w_gate_ref,
                  b_gate_ref, lam8_ref, lru_gain_ref, w_pool_ref, pool_scale_ref, w_out_ref,
                  g2_ref, w_rt_ref, b_rt_ref, tri_ref,
                  h1_ref, hn2_ref, mi_ref, mf_ref, cnt_ref,
                  ux_buf, up_buf, a_buf, b_buf, hs_buf, hstate, running):
    i = pl.program_id(0)
    mix = functools.partial(
        _mix_rows, g1_ref=g1_ref, w_in_ref=w_in_ref, conv_w_ref=conv_w_ref,
        conv_b_ref=conv_b_ref, w_gate_ref=w_gate_ref, b_gate_ref=b_gate_ref, lam8_ref=lam8_ref,
        lru_gain_ref=lru_gain_ref, w_pool_ref=w_pool_ref, pool_scale_ref=pool_scale_ref,
        w_out_ref=w_out_ref, ux_buf=ux_buf, up_buf=up_buf, a_buf=a_buf, b_buf=b_buf,
        hs_buf=hs_buf, hstate=hstate)

    @pl.when(i == 0)
    def _():
        ux_buf[0:CONV_HALO, :] = jnp.zeros((CONV_HALO, LRU_WIDTH), jnp.float32)
        up_buf[0:POOL_HALO, :] = jnp.zeros((POOL_HALO, POOL_WIDTH), jnp.float32)
        hstate[...] = jnp.zeros_like(hstate)
        running[...] = jnp.zeros_like(running)
        mix(meta_ref[...], META_ROWS, False)

    h1 = mix(x_ref[...], MIX_ROWS, True)
    h1_ref[...] = h1

    hn2 = _rms(h1, g2_ref[...])
    hn2_ref[...] = hn2
    logits = _bdot(hn2, w_rt_ref[...]) + b_rt_ref[...]
    lane = lax.broadcasted_iota(jnp.int32, (MIX_ROWS, LANES), 1)
    lane_f = lane.astype(jnp.float32)
    neg = jnp.float32(-jnp.inf)
    big = jnp.float32(4 * LANES)

    is_g = (lane >= N_EXPERTS) & (lane < N_EXPERTS + N_GROUPS)
    gl = jnp.where(is_g, logits, neg)
    gmax = jnp.max(gl, axis=-1, keepdims=True)
    gidx = jnp.min(jnp.where(gl == gmax, lane_f, big), axis=-1, keepdims=True)
    p_g = 1.0 / jnp.sum(jnp.exp(gl - gmax), axis=-1, keepdims=True)
    grp = gidx.astype(jnp.int32) - N_EXPERTS

    el = jnp.where((lane >> 3) == grp, logits, neg)
    m1 = jnp.max(el, axis=-1, keepdims=True)
    i1 = jnp.min(jnp.where(el == m1, lane_f, big), axis=-1, keepdims=True)
    el2 = jnp.where(lane_f == i1, neg, el)
    m2 = jnp.max(el2, axis=-1, keepdims=True)
    i2 = jnp.min(jnp.where(el2 == m2, lane_f, big), axis=-1, keepdims=True)
    e21 = jnp.exp(m2 - m1)
    den = 1.0 + e21
    gate0 = p_g * (1.0 / den)
    gate1 = p_g * (e21 / den)

    sel0 = lane_f == i1
    sel1 = lane_f == i2
    onehot = jnp.where(sel0 | sel1, 1.0, 0.0)
    before = jnp.dot(tri_ref[...], onehot.astype(jnp.bfloat16),
                     preferred_element_type=jnp.float32) + running[0:1, :]
    rank0 = jnp.sum(jnp.where(sel0, before, 0.0), axis=-1, keepdims=True)
    rank1 = jnp.sum(jnp.where(sel1, before, 0.0), axis=-1, keepdims=True)
    running[0:1, :] = running[0:1, :] + jnp.sum(onehot, axis=0, keepdims=True)

    packed = jnp.where(lane == 0, i1, jnp.where(lane == 1, i2, jnp.where(
        lane == 2, rank0, jnp.where(lane == 3, rank1, 0.0))))
    mi_ref[...] = packed.astype(jnp.int32)
    mf_ref[...] = jnp.where(lane == 0, gate0, jnp.where(lane == 1, gate1, 0.0))
    cnt_ref[...] = running[...]


def _mixer(x_t, meta_t, g1, w_in, conv_w, conv_b, w_gate, b_gate, lam8, lru_gain, w_pool,
           pool_scale, w_out, g2, w_rt, b_rt, tri):
    n_steps = N_TOK // MIX_ROWS
    full = lambda a: pl.BlockSpec(a.shape, lambda i: (0,) * a.ndim)
    row_blk = lambda w: pl.BlockSpec((MIX_ROWS, w), lambda i: (i, 0))
    consts = (meta_t, g1, w_in, conv_w, conv_b, w_gate, b_gate, lam8, lru_gain, w_pool,
              pool_scale, w_out, g2, w_rt, b_rt, tri)
    return pl.pallas_call(
        _mixer_kernel,
        grid=(n_steps,),
        in_specs=[row_blk(D_MODEL)] + [full(a) for a in consts],
        out_specs=[row_blk(D_MODEL), row_blk(D_MODEL), row_blk(LANES), row_blk(LANES),
                   pl.BlockSpec((SUBLANES, LANES), lambda i: (0, 0))],
        out_shape=[jax.ShapeDtypeStruct((N_TOK, D_MODEL), jnp.float32),
                   jax.ShapeDtypeStruct((N_TOK, D_MODEL), jnp.float32),
                   jax.ShapeDtypeStruct((N_TOK, LANES), jnp.int32),
                   jax.ShapeDtypeStruct((N_TOK, LANES), jnp.float32),
                   jax.ShapeDtypeStruct((SUBLANES, LANES), jnp.float32)],
        scratch_shapes=[pltpu.VMEM((CONV_HALO + MIX_ROWS, LRU_WIDTH), jnp.float32),
                        pltpu.VMEM((POOL_HALO + MIX_ROWS, POOL_WIDTH), jnp.float32),
                        pltpu.VMEM((MIX_ROWS, LRU_WIDTH), jnp.float32),
                        pltpu.VMEM((MIX_ROWS, LRU_WIDTH), jnp.float32),
                        pltpu.VMEM((MIX_ROWS, LRU_WIDTH), jnp.float32),
                        pltpu.VMEM((SUBLANES, LRU_WIDTH), jnp.float32),
                        pltpu.VMEM((SUBLANES, LANES), jnp.float32)],
        compiler_params=pltpu.CompilerParams(dimension_semantics=("arbitrary",),
                                             vmem_limit_bytes=VMEM_LIMIT),
        name="mixer",
    )(x_t, *consts)


def _row_gather_start(src_hbm, idx_ref, base, n, dst, sem):
    def body(r, c):
        tok = idx_ref[base + r]
        pltpu.make_async_copy(src_hbm.at[pl.ds(tok, 1)], dst.at[pl.ds(r, 1)], sem).start()
        return c
    lax.fori_loop(0, n, body, 0, unroll=8)


def _row_gather_wait(src_hbm, n, dst, sem):
    pltpu.make_async_copy(src_hbm.at[pl.ds(0, n)], dst, sem).wait()


def _experts_kernel(bexp_ref, rowtok_ref, x_hbm, wg_ref, wu_ref, wd_ref, o_ref,
                    xbuf, wg_b, wu_b, wd_b, sem):
    i = pl.program_id(0)
    nb = pl.num_programs(0)
    slot = i % 2

    @pl.when(i == 0)
    def _():
        _row_gather_start(x_hbm, rowtok_ref, 0, MOE_BM, xbuf.at[0], sem.at[0])

    @pl.when(i + 1 < nb)
    def _():
        _row_gather_start(x_hbm, rowtok_ref, (i + 1) * MOE_BM, MOE_BM, xbuf.at[1 - slot],
                          sem.at[1 - slot])

    @pl.when((i == 0) | (bexp_ref[i] != bexp_ref[jnp.maximum(i - 1, 0)]))
    def _():
        wg_b[...] = wg_ref[0].astype(jnp.bfloat16)
        wu_b[...] = wu_ref[0].astype(jnp.bfloat16)
        wd_b[...] = wd_ref[0].astype(jnp.bfloat16)

    _row_gather_wait(x_hbm, MOE_BM, xbuf.at[slot], sem.at[slot])
    x = xbuf[slot].astype(jnp.bfloat16)
    g = jnp.dot(x, wg_b[...], preferred_element_type=jnp.float32)
    u = jnp.dot(x, wu_b[...], preferred_element_type=jnp.float32)
    hid = (g * jax.nn.sigmoid(g)) * u
    o_ref[...] = _bdot(hid, wd_b[...])


def _experts(block_expert, row_token, hn2, w_gate, w_up, w_down):
    wmap = lambda i, bexp, rt: (bexp[i], 0, 0)
    return pl.pallas_call(
        _experts_kernel,
        grid_spec=pltpu.PrefetchScalarGridSpec(
            num_scalar_prefetch=2,
            grid=(N_BLOCKS,),
            in_specs=[pl.BlockSpec(memory_space=pl.ANY),
                      pl.BlockSpec((1, D_MODEL, EXPERT_FF), wmap),
                      pl.BlockSpec((1, D_MODEL, EXPERT_FF), wmap),
                      pl.BlockSpec((1, EXPERT_FF, D_MODEL), wmap)],
            out_specs=pl.BlockSpec((MOE_BM, D_MODEL), lambda i, bexp, rt: (i, 0)),
            scratch_shapes=[pltpu.VMEM((2, MOE_BM, D_MODEL), jnp.float32),
                            pltpu.VMEM((D_MODEL, EXPERT_FF), jnp.bfloat16),
                            pltpu.VMEM((D_MODEL, EXPERT_FF), jnp.bfloat16),
                            pltpu.VMEM((EXPERT_FF, D_MODEL), jnp.bfloat16),
                            pltpu.SemaphoreType.DMA((2,))]),
        out_shape=jax.ShapeDtypeStruct((P_ROWS, D_MODEL), jnp.float32),
        compiler_params=pltpu.CompilerParams(dimension_semantics=("arbitrary",),
                                             vmem_limit_bytes=VMEM_LIMIT),
        name="experts",
    )(block_expert, row_token, hn2, w_gate, w_up, w_down)


def _combine_kernel(dest_ref, o_hbm, h1_ref, mf_ref, gain_ref, out_ref, obuf, sem):
    i = pl.program_id(0)
    nb = pl.num_programs(0)
    slot = i % 2
    rows = TOP_K * CMB_TM

    @pl.when(i == 0)
    def _():
        _row_gather_start(o_hbm, dest_ref, 0, rows, obuf.at[0], sem.at[0])

    @pl.when(i + 1 < nb)
    def _():
        _row_gather_start(o_hbm, dest_ref, (i + 1) * rows, rows, obuf.at[1 - slot],
                          sem.at[1 - slot])

    _row_gather_wait(o_hbm, rows, obuf.at[slot], sem.at[slot])
    mf = mf_ref[...]
    y = mf[:, 0:1] * obuf[slot, 0:CMB_TM, :] + mf[:, 1:2] * obuf[slot, CMB_TM:rows, :]
    out_ref[...] = _rms(h1_ref[...] + y, gain_ref[...])


def _combine(dest, o_sorted, h1, meta_f, final_gain):
    return pl.pallas_call(
        _combine_kernel,
        grid_spec=pltpu.PrefetchScalarGridSpec(
            num_scalar_prefetch=1,
            grid=(N_TOK // CMB_TM,),
            in_specs=[pl.BlockSpec(memory_space=pl.ANY),
                      pl.BlockSpec((CMB_TM, D_MODEL), lambda i, d: (i, 0)),
                      pl.BlockSpec((CMB_TM, LANES), lambda i, d: (i, 0)),
                      pl.BlockSpec((1, D_MODEL), lambda i, d: (0, 0))],
            out_specs=pl.BlockSpec((CMB_TM, D_MODEL), lambda i, d: (i, 0)),
            scratch_shapes=[pltpu.VMEM((2, TOP_K * CMB_TM, D_MODEL), jnp.float32),
                            pltpu.SemaphoreType.DMA((2,))]),
        out_shape=jax.ShapeDtypeStruct((N_TOK, D_MODEL), jnp.float32),
        compiler_params=pltpu.CompilerParams(dimension_semantics=("arbitrary",),
                                             vmem_limit_bytes=VMEM_LIMIT),
        name="combine",
    )(dest, o_sorted, h1, meta_f, final_gain)


def _block_diag(blocks):
    n = len(blocks)
    r, c = blocks[0].shape
    out = jnp.zeros((n * r, n * c), blocks[0].dtype)
    for k, blk in enumerate(blocks):
        out = out.at[k * r:(k + 1) * r, k * c:(k + 1) * c].set(blk)
    return out


def kernel(x, meta_tokens, norm1_gain, w_in, conv_w, conv_b, lru_wa, lru_ba, lru_wx, lru_bx,
           lru_lambda, lru_out_gain, pool_w, pool_scale, w_out, norm2_gain, w_group, b_group,
           w_router, b_router, w_gate, w_up, w_down, final_gain):
    assert x.shape == (BATCH, SEQ, D_MODEL) and norm1_gain.shape[0] == 1
    f32, bf16 = jnp.float32, jnp.bfloat16
    row = lambda v: v.reshape(1, -1).astype(f32)

    x_t = jnp.transpose(x, (1, 0, 2)).reshape(N_TOK, D_MODEL)
    meta_t = jnp.repeat(meta_tokens.astype(f32), BATCH, axis=0)

    heads_per = LRU_HEADS // 2
    w_gate_blk = jnp.stack([
        jnp.concatenate([_block_diag([lru_wa[0, h] for h in range(j * heads_per, (j + 1) * heads_per)]),
                         _block_diag([lru_wx[0, h] for h in range(j * heads_per, (j + 1) * heads_per)])],
                        axis=1) for j in range(2)]).astype(bf16)
    b_gate_blk = jnp.stack([lru_ba[0], lru_bx[0]]).astype(f32)
    w_pool_blk = jnp.stack([_block_diag([pool_w[0, 2 * j], pool_w[0, 2 * j + 1]])
                            for j in range(2)]).astype(bf16)
    lam8 = row(LRU_C * jax.nn.log_sigmoid(lru_lambda[0].astype(f32)))
    pad = LANES - N_EXPERTS - N_GROUPS
    w_rt = jnp.concatenate([w_router[0], w_group[0], jnp.zeros((D_MODEL, pad), f32)], axis=1).astype(bf16)
    b_rt = jnp.concatenate([b_router[0], b_group[0], jnp.zeros((pad,), f32)]).reshape(1, LANES).astype(f32)
    ridx = jnp.arange(MIX_ROWS)
    tri = (ridx[None, :] < ridx[:, None]).astype(bf16)

    h1, hn2, meta_i, meta_f, cnt = _mixer(
        x_t, meta_t, row(norm1_gain[0]), w_in[0].astype(bf16), conv_w[0].astype(f32),
        row(conv_b[0]), w_gate_blk, b_gate_blk, lam8, row(lru_out_gain[0]), w_pool_blk,
        row(pool_scale[0]), w_out[0].astype(bf16), row(norm2_gain[0]), w_rt, b_rt, tri)

    counts = cnt[0, :N_EXPERTS].astype(jnp.int32)
    padded = ((counts + MOE_BM - 1) // MOE_BM) * MOE_BM
    pends = jnp.cumsum(padded)
    pstarts = pends - padded
    eid = meta_i[:, 0:TOP_K]
    dest = pstarts[eid] + meta_i[:, TOP_K:2 * TOP_K]
    tok = jnp.broadcast_to(jnp.arange(N_TOK, dtype=jnp.int32)[:, None], (N_TOK, TOP_K))
    row_token = jnp.zeros((P_ROWS,), jnp.int32).at[dest.reshape(-1)].set(tok.reshape(-1))
    block_expert = jnp.clip(jnp.searchsorted(pends, jnp.arange(N_BLOCKS) * MOE_BM, side='right'),
                            0, N_EXPERTS - 1).astype(jnp.int32)

    o_sorted = _experts(block_expert, row_token, hn2, w_gate[0], w_up[0], w_down[0])

    dest_tiles = dest.reshape(N_TOK // CMB_TM, CMB_TM, TOP_K).transpose(0, 2, 1).reshape(-1)
    out_t = _combine(dest_tiles, o_sorted, h1, meta_f, row(final_gain))
    return jnp.transpose(out_t.reshape(SEQ, BATCH, D_MODEL), (1, 0, 2))
```

```python
import functools

import jax
import jax.numpy as jnp
from jax import lax
from jax.experimental import pallas as pl
from jax.experimental.pallas import tpu as pltpu

D_MODEL = 1024
BATCH = 8
SEQ = 4096
N_META = 16
LRU_WIDTH = 512
LRU_HEADS = 8
LRU_HEAD_DIM = 64
CONV_WIDTH = 4
LRU_C = 8.0
POOL_WIDTH = 512
POOL_WINDOWS = (2, 4, 8, 16)
POOL_GROUP = 128
N_GROUPS = 4
EXPERTS_PER_GROUP = 8
N_EXPERTS = 32
TOP_K = 2
EXPERT_FF = 512
RMS_EPS = 1e-6

N_TOK = BATCH * SEQ
SUBLANES = 8
LANES = 128
MIX_ROWS = 512
META_ROWS = N_META * BATCH
CONV_HALO = (CONV_WIDTH - 1) * BATCH
POOL_HALO = (max(POOL_WINDOWS) - 1) * BATCH
MOE_BM = 256
N_ASSIGN = N_TOK * TOP_K
N_BLOCKS = -(-(N_ASSIGN + N_EXPERTS * (MOE_BM - 1)) // MOE_BM)
P_ROWS = N_BLOCKS * MOE_BM
CMB_TM = 256
VMEM_LIMIT = 56 * 1024 * 1024


def _rms(x, gain):
    return x * lax.rsqrt(jnp.mean(x * x, axis=-1, keepdims=True) + RMS_EPS) * gain


def _bdot(a, b):
    return jnp.dot(a.astype(jnp.bfloat16), b, preferred_element_type=jnp.float32)


def _mix_rows(x, nrows, with_output, g1_ref, w_in_ref, conv_w_ref, conv_b_ref, w_gate_ref,
              b_gate_ref, lam8_ref, lru_gain_ref, w_pool_ref, pool_scale_ref, w_out_ref,
              ux_buf, up_buf, a_buf, b_buf, hs_buf, hstate):
    hn = _rms(x, g1_ref[...])
    proj = _bdot(hn, w_in_ref[...])
    ux = proj[:, :LRU_WIDTH]
    up = proj[:, 2 * LRU_WIDTH:]

    ux_buf[CONV_HALO:CONV_HALO + nrows, :] = ux
    xc = conv_b_ref[...] + conv_w_ref[3:4, :] * ux
    for k in range(CONV_WIDTH - 1):
        xc = xc + conv_w_ref[k:k + 1, :] * ux_buf[k * BATCH:k * BATCH + nrows, :]
    ux_buf[0:CONV_HALO, :] = ux_buf[nrows:nrows + CONV_HALO, :]

    half = LRU_WIDTH // 2
    z = [_bdot(xc[:, j * half:(j + 1) * half], w_gate_ref[j]) for j in range(2)]
    za = jnp.concatenate([z[0][:, :half], z[1][:, :half]], axis=-1) + b_gate_ref[0:1, :]
    zx = jnp.concatenate([z[0][:, half:], z[1][:, half:]], axis=-1) + b_gate_ref[1:2, :]
    r = jax.nn.sigmoid(za)
    gi = jax.nn.sigmoid(zx)
    log_a = lam8_ref[...] * r
    a = jnp.exp(log_a)
    a_buf[0:nrows, :] = a
    b_buf[0:nrows, :] = jnp.sqrt(1.0 - a * a) * (gi * xc)

    def step(t, h):
        r0 = pl.multiple_of(t * SUBLANES, SUBLANES)
        h = a_buf[pl.ds(r0, SUBLANES), :] * h + b_buf[pl.ds(r0, SUBLANES), :]
        hs_buf[pl.ds(r0, SUBLANES), :] = h
        return h

    hstate[...] = lax.fori_loop(0, nrows // SUBLANES, step, hstate[...], unroll=8)

    up_buf[POOL_HALO:POOL_HALO + nrows, :] = up
    if not with_output:
        up_buf[0:POOL_HALO, :] = up_buf[nrows:nrows + POOL_HALO, :]
        return None

    ug = proj[:, LRU_WIDTH:2 * LRU_WIDTH]
    y_lru = _rms(hs_buf[0:nrows, :] * jax.nn.gelu(ug), lru_gain_ref[...])

    pm = []
    for g, w in enumerate(POOL_WINDOWS):
        lo, hi = g * POOL_GROUP, (g + 1) * POOL_GROUP
        cur = up[:, lo:hi]
        s = cur
        for j in range(1, w):
            off = POOL_HALO - j * BATCH
            s = s + up_buf[off:off + nrows, lo:hi]
        pm.append(s * (1.0 / w) - cur)
    up_buf[0:POOL_HALO, :] = up_buf[nrows:nrows + POOL_HALO, :]
    yp = [_bdot(jnp.concatenate(pm[2 * j:2 * j + 2], axis=-1), w_pool_ref[j]) for j in range(2)]
    y_pool = _rms(jnp.concatenate(yp, axis=-1), pool_scale_ref[...])

    y = _bdot(y_lru, w_out_ref[0:LRU_WIDTH, :]) + _bdot(y_pool, w_out_ref[LRU_WIDTH:, :])
    return x + y


def _mixer_kernel(x_ref, meta_ref, g1_ref, w_in_ref, conv_w_ref, conv_b_ref, w_gate_ref,
                  b_gate_ref, lam8_ref, lru_gain_ref, w_pool_ref, pool_scale_ref, w_out_ref,
                  g2_ref, w_rt_ref, b_rt_ref, tri_ref,
                  h1_ref, hn2_ref, mi_ref, mf_ref, cnt_ref,
                  ux_buf, up_buf, a_buf, b_buf, hs_buf, hstate, running):
    i = pl.program_id(0)
    mix = functools.partial(
        _mix_rows, g1_ref=g1_ref, w_in_ref=w_in_ref, conv_w_ref=conv_w_ref,
        conv_b_ref=conv_b_ref, w_gate_ref=w_gate_ref, b_gate_ref=b_gate_ref, lam8_ref=lam8_ref,
        lru_gain_ref=lru_gain_ref, w_pool_ref=w_pool_ref, pool_scale_ref=pool_scale_ref,
        w_out_ref=w_out_ref, ux_buf=ux_buf, up_buf=up_buf, a_buf=a_buf, b_buf=b_buf,
        hs_buf=hs_buf, hstate=hstate)

    @pl.when(i == 0)
    def _():
        ux_buf[0:CONV_HALO, :] = jnp.zeros((CONV_HALO, LRU_WIDTH), jnp.float32)
        up_buf[0:POOL_HALO, :] = jnp.zeros((POOL_HALO, POOL_WIDTH), jnp.float32)
        hstate[...] = jnp.zeros_like(hstate)
        running[...] = jnp.zeros_like(running)
        mix(meta_ref[...], META_ROWS, False)

    h1 = mix(x_ref[...], MIX_ROWS, True)
    h1_ref[...] = h1

    hn2 = _rms(h1, g2_ref[...])
    hn2_ref[...] = hn2
    logits = _bdot(hn2, w_rt_ref[...]) + b_rt_ref[...]
    lane = lax.broadcasted_iota(jnp.int32, (MIX_ROWS, LANES), 1)
    lane_f = lane.astype(jnp.float32)
    neg = jnp.float32(-jnp.inf)
    big = jnp.float32(4 * LANES)

    is_g = (lane >= N_EXPERTS) & (lane < N_EXPERTS + N_GROUPS)
    gl = jnp.where(is_g, logits, neg)
    gmax = jnp.max(gl, axis=-1, keepdims=True)
    gidx = jnp.min(jnp.where(gl == gmax, lane_f, big), axis=-1, keepdims=True)
    p_g = 1.0 / jnp.sum(jnp.exp(gl - gmax), axis=-1, keepdims=True)
    grp = gidx.astype(jnp.int32) - N_EXPERTS

    el = jnp.where((lane >> 3) == grp, logits, neg)
    m1 = jnp.max(el, axis=-1, keepdims=True)
    i1 = jnp.min(jnp.where(el == m1, lane_f, big), axis=-1, keepdims=True)
    el2 = jnp.where(lane_f == i1, neg, el)
    m2 = jnp.max(el2, axis=-1, keepdims=True)
    i2 = jnp.min(jnp.where(el2 == m2, lane_f, big), axis=-1, keepdims=True)
    e21 = jnp.exp(m2 - m1)
    den = 1.0 + e21
    gate0 = p_g * (1.0 / den)
    gate1 = p_g * (e21 / den)

    sel0 = lane_f == i1
    sel1 = lane_f == i2
    onehot = jnp.where(sel0 | sel1, 1.0, 0.0)
    before = jnp.dot(tri_ref[...], onehot.astype(jnp.bfloat16),
                     preferred_element_type=jnp.float32) + running[0:1, :]
    rank0 = jnp.sum(jnp.where(sel0, before, 0.0), axis=-1, keepdims=True)
    rank1 = jnp.sum(jnp.where(sel1, before, 0.0), axis=-1, keepdims=True)
    running[0:1, :] = running[0:1, :] + jnp.sum(onehot, axis=0, keepdims=True)

    packed = jnp.where(lane == 0, i1, jnp.where(lane == 1, i2, jnp.where(
        lane == 2, rank0, jnp.where(lane == 3, rank1, 0.0))))
    mi_ref[...] = packed.astype(jnp.int32)
    mf_ref[...] = jnp.where(lane == 0, gate0, jnp.where(lane == 1, gate1, 0.0))
    cnt_ref[...] = running[...]


def _mixer(x_t, meta_t, g1, w_in, conv_w, conv_b, w_gate, b_gate, lam8, lru_gain, w_pool,
           pool_scale, w_out, g2, w_rt, b_rt, tri):
    n_steps = N_TOK // MIX_ROWS
    full = lambda a: pl.BlockSpec(a.shape, lambda i: (0,) * a.ndim)
    row_blk = lambda w: pl.BlockSpec((MIX_ROWS, w), lambda i: (i, 0))
    consts = (meta_t, g1, w_in, conv_w, conv_b, w_gate, b_gate, lam8, lru_gain, w_pool,
              pool_scale, w_out, g2, w_rt, b_rt, tri)
    return pl.pallas_call(
        _mixer_kernel,
        grid=(n_steps,),
        in_specs=[row_blk(D_MODEL)] + [full(a) for a in consts],
        out_specs=[row_blk(D_MODEL), row_blk(D_MODEL), row_blk(LANES), row_blk(LANES),
                   pl.BlockSpec((SUBLANES, LANES), lambda i: (0, 0))],
        out_shape=[jax.ShapeDtypeStruct((N_TOK, D_MODEL), jnp.float32),
                   jax.ShapeDtypeStruct((N_TOK, D_MODEL), jnp.float32),
                   jax.ShapeDtypeStruct((N_TOK, LANES), jnp.int32),
                   jax.ShapeDtypeStruct((N_TOK, LANES), jnp.float32),
                   jax.ShapeDtypeStruct((SUBLANES, LANES), jnp.float32)],
        scratch_shapes=[pltpu.VMEM((CONV_HALO + MIX_ROWS, LRU_WIDTH), jnp.float32),
                        pltpu.VMEM((POOL_HALO + MIX_ROWS, POOL_WIDTH), jnp.float32),
                        pltpu.VMEM((MIX_ROWS, LRU_WIDTH), jnp.float32),
                        pltpu.VMEM((MIX_ROWS, LRU_WIDTH), jnp.float32),
                        pltpu.VMEM((MIX_ROWS, LRU_WIDTH), jnp.float32),
                        pltpu.VMEM((SUBLANES, LRU_WIDTH), jnp.float32),
                        pltpu.VMEM((SUBLANES, LANES), jnp.float32)],
        compiler_params=pltpu.CompilerParams(dimension_semantics=("arbitrary",),
                                             vmem_limit_bytes=VMEM_LIMIT),
        name="mixer",
    )(x_t, *consts)


def _row_gather_start(src_hbm, idx_ref, base, n, dst, sem):
    for r in range(n):
        tok = idx_ref[base + r]
        pltpu.make_async_copy(src_hbm.at[pl.ds(tok, 1)], dst.at[pl.ds(r, 1)], sem).start()


def _row_gather_wait(src_hbm, n, dst, sem):
    pltpu.make_async_copy(src_hbm.at[pl.ds(0, n)], dst, sem).wait()


def _experts_kernel(bexp_ref, rowtok_ref, x_hbm, wg_ref, wu_ref, wd_ref, o_ref,
                    xbuf, wg_b, wu_b, wd_b, sem):
    i = pl.program_id(0)
    nb = pl.num_programs(0)
    slot = i % 2

    @pl.when(i == 0)
    def _():
        _row_gather_start(x_hbm, rowtok_ref, 0, MOE_BM, xbuf.at[0], sem.at[0])

    @pl.when(i + 1 < nb)
    def _():
        _row_gather_start(x_hbm, rowtok_ref, (i + 1) * MOE_BM, MOE_BM, xbuf.at[1 - slot],
                          sem.at[1 - slot])

    @pl.when((i == 0) | (bexp_ref[i] != bexp_ref[jnp.maximum(i - 1, 0)]))
    def _():
        wg_b[...] = wg_ref[0].astype(jnp.bfloat16)
        wu_b[...] = wu_ref[0].astype(jnp.bfloat16)
        wd_b[...] = wd_ref[0].astype(jnp.bfloat16)

    _row_gather_wait(x_hbm, MOE_BM, xbuf.at[slot], sem.at[slot])
    x = xbuf[slot].astype(jnp.bfloat16)
    g = jnp.dot(x, wg_b[...], preferred_element_type=jnp.float32)
    u = jnp.dot(x, wu_b[...], preferred_element_type=jnp.float32)
    hid = (g * jax.nn.sigmoid(g)) * u
    o_ref[...] = _bdot(hid, wd_b[...])


def _experts(block_expert, row_token, hn2, w_gate, w_up, w_down):
    wmap = lambda i, bexp, rt: (bexp[i], 0, 0)
    return pl.pallas_call(
        _experts_kernel,
        grid_spec=pltpu.PrefetchScalarGridSpec(
            num_scalar_prefetch=2,
            grid=(N_BLOCKS,),
            in_specs=[pl.BlockSpec(memory_space=pl.ANY),
                      pl.BlockSpec((1, D_MODEL, EXPERT_FF), wmap),
                      pl.BlockSpec((1, D_MODEL, EXPERT_FF), wmap),
                      pl.BlockSpec((1, EXPERT_FF, D_MODEL), wmap)],
            out_specs=pl.BlockSpec((MOE_BM, D_MODEL), lambda i, bexp, rt: (i, 0)),
            scratch_shapes=[pltpu.VMEM((2, MOE_BM, D_MODEL), jnp.float32),
                            pltpu.VMEM((D_MODEL, EXPERT_FF), jnp.bfloat16),
                            pltpu.VMEM((D_MODEL, EXPERT_FF), jnp.bfloat16),
                            pltpu.VMEM((EXPERT_FF, D_MODEL), jnp.bfloat16),
                            pltpu.SemaphoreType.DMA((2,))]),
        out_shape=jax.ShapeDtypeStruct((P_ROWS, D_MODEL), jnp.float32),
        compiler_params=pltpu.CompilerParams(dimension_semantics=("arbitrary",),
                                             vmem_limit_bytes=VMEM_LIMIT),
        name="experts",
    )(block_expert, row_token, hn2, w_gate, w_up, w_down)


def _combine_kernel(dest_ref, o_hbm, h1_ref, mf_ref, gain_ref, out_ref, obuf, sem):
    i = pl.program_id(0)
    nb = pl.num_programs(0)
    slot = i % 2
    rows = TOP_K * CMB_TM

    @pl.when(i == 0)
    def _():
        _row_gather_start(o_hbm, dest_ref, 0, rows, obuf.at[0], sem.at[0])

    @pl.when(i + 1 < nb)
    def _():
        _row_gather_start(o_hbm, dest_ref, (i + 1) * rows, rows, obuf.at[1 - slot],
                          sem.at[1 - slot])

    _row_gather_wait(o_hbm, rows, obuf.at[slot], sem.at[slot])
    mf = mf_ref[...]
    y = mf[:, 0:1] * obuf[slot, 0:CMB_TM, :] + mf[:, 1:2] * obuf[slot, CMB_TM:rows, :]
    out_ref[...] = _rms(h1_ref[...] + y, gain_ref[...])


def _combine(dest, o_sorted, h1, meta_f, final_gain):
    return pl.pallas_call(
        _combine_kernel,
        grid_spec=pltpu.PrefetchScalarGridSpec(
            num_scalar_prefetch=1,
            grid=(N_TOK // CMB_TM,),
            in_specs=[pl.BlockSpec(memory_space=pl.ANY),
                      pl.BlockSpec((CMB_TM, D_MODEL), lambda i, d: (i, 0)),
                      pl.BlockSpec((CMB_TM, LANES), lambda i, d: (i, 0)),
                      pl.BlockSpec((1, D_MODEL), lambda i, d: (0, 0))],
            out_specs=pl.BlockSpec((CMB_TM, D_MODEL), lambda i, d: (i, 0)),
            scratch_shapes=[pltpu.VMEM((2, TOP_K * CMB_TM, D_MODEL), jnp.float32),
                            pltpu.SemaphoreType.DMA((2,))]),
        out_shape=jax.ShapeDtypeStruct((N_TOK, D_MODEL), jnp.float32),
        compiler_params=pltpu.CompilerParams(dimension_semantics=("arbitrary",),
                                             vmem_limit_bytes=VMEM_LIMIT),
        name="combine",
    )(dest, o_sorted, h1, meta_f, final_gain)


def _block_diag(blocks):
    n = len(blocks)
    r, c = blocks[0].shape
    out = jnp.zeros((n * r, n * c), blocks[0].dtype)
    for k, blk in enumerate(blocks):
        out = out.at[k * r:(k + 1) * r, k * c:(k + 1) * c].set(blk)
    return out


def kernel(x, meta_tokens, norm1_gain, w_in, conv_w, conv_b, lru_wa, lru_ba, lru_wx, lru_bx,
           lru_lambda, lru_out_gain, pool_w, pool_scale, w_out, norm2_gain, w_group, b_group,
           w_router, b_router, w_gate, w_up, w_down, final_gain):
    assert x.shape == (BATCH, SEQ, D_MODEL) and norm1_gain.shape[0] == 1
    f32, bf16 = jnp.float32, jnp.bfloat16
    row = lambda v: v.reshape(1, -1).astype(f32)

    x_t = jnp.transpose(x, (1, 0, 2)).reshape(N_TOK, D_MODEL)
    meta_t = jnp.repeat(meta_tokens.astype(f32), BATCH, axis=0)

    heads_per = LRU_HEADS // 2
    w_gate_blk = jnp.stack([
        jnp.concatenate([_block_diag([lru_wa[0, h] for h in range(j * heads_per, (j + 1) * heads_per)]),
                         _block_diag([lru_wx[0, h] for h in range(j * heads_per, (j + 1) * heads_per)])],
                        axis=1) for j in range(2)]).astype(bf16)
    b_gate_blk = jnp.stack([lru_ba[0], lru_bx[0]]).astype(f32)
    w_pool_blk = jnp.stack([_block_diag([pool_w[0, 2 * j], pool_w[0, 2 * j + 1]])
                            for j in range(2)]).astype(bf16)
    lam8 = row(LRU_C * jax.nn.log_sigmoid(lru_lambda[0].astype(f32)))
    pad = LANES - N_EXPERTS - N_GROUPS
    w_rt = jnp.concatenate([w_router[0], w_group[0], jnp.zeros((D_MODEL, pad), f32)], axis=1).astype(bf16)
    b_rt = jnp.concatenate([b_router[0], b_group[0], jnp.zeros((pad,), f32)]).reshape(1, LANES).astype(f32)
    ridx = jnp.arange(MIX_ROWS)
    tri = (ridx[None, :] < ridx[:, None]).astype(bf16)

    h1, hn2, meta_i, meta_f, cnt = _mixer(
        x_t, meta_t, row(norm1_gain[0]), w_in[0].astype(bf16), conv_w[0].astype(f32),
        row(conv_b[0]), w_gate_blk, b_gate_blk, lam8, row(lru_out_gain[0]), w_pool_blk,
        row(pool_scale[0]), w_out[0].astype(bf16), row(norm2_gain[0]), w_rt, b_rt, tri)

    counts = cnt[0, :N_EXPERTS].astype(jnp.int32)
    padded = ((counts + MOE_BM - 1) // MOE_BM) * MOE_BM
    pends = jnp.cumsum(padded)
    pstarts = pends - padded
    eid = meta_i[:, 0:TOP_K]
    lut = (eid[..., None] == jnp.arange(N_EXPERTS, dtype=jnp.int32)) * pstarts
    dest = jnp.sum(lut, axis=-1) + meta_i[:, TOP_K:2 * TOP_K]
    tok = jnp.broadcast_to(jnp.arange(N_TOK, dtype=jnp.int32)[:, None], (N_TOK, TOP_K))
    row_token = jnp.zeros((P_ROWS,), jnp.int32).at[dest.reshape(-1)].set(tok.reshape(-1))
    block_start = jnp.arange(N_BLOCKS, dtype=jnp.int32) * MOE_BM
    block_expert = jnp.minimum(jnp.sum(pends[None, :] <= block_start[:, None], axis=-1),
                               N_EXPERTS - 1).astype(jnp.int32)

    o_sorted = _experts(block_expert, row_token, hn2, w_gate[0], w_up[0], w_down[0])

    dest_tiles = dest.reshape(N_TOK // CMB_TM, CMB_TM, TOP_K).transpose(0, 2, 1).reshape(-1)
    out_t = _combine(dest_tiles, o_sorted, h1, meta_f, row(final_gain))
    return jnp.transpose(out_t.reshape(SEQ, BATCH, D_MODEL), (1, 0, 2))
```

```python
import functools

import jax
import jax.numpy as jnp
from jax import lax
from jax.experimental import pallas as pl
from jax.experimental.pallas import tpu as pltpu

D_MODEL = 1024
BATCH = 8
SEQ = 4096
N_META = 16
LRU_WIDTH = 512
LRU_HEADS = 8
LRU_HEAD_DIM = 64
CONV_WIDTH = 4
LRU_C = 8.0
POOL_WIDTH = 512
POOL_WINDOWS = (2, 4, 8, 16)
POOL_GROUP = 128
N_GROUPS = 4
EXPERTS_PER_GROUP = 8
N_EXPERTS = 32
TOP_K = 2
EXPERT_FF = 512
RMS_EPS = 1e-6

N_TOK = BATCH * SEQ
SUBLANES = 8
LANES = 128
MIX_ROWS = 512
META_ROWS = N_META * BATCH
CONV_HALO = (CONV_WIDTH - 1) * BATCH
POOL_HALO = (max(POOL_WINDOWS) - 1) * BATCH
MIX_STEPS = N_TOK // MIX_ROWS
MOE_BM = 256
N_ASSIGN = N_TOK * TOP_K
N_BLOCKS = N_ASSIGN // MOE_BM + N_EXPERTS
EXPERT_CAP = N_TOK
CAP_BLOCKS = EXPERT_CAP // MOE_BM
ROW_SUB = 8
DATA_SUB = 4
HALF = D_MODEL // 2
XS_DUMP = N_EXPERTS * EXPERT_CAP
XS_ROWS = XS_DUMP + TOP_K * MIX_ROWS
Y_DUMP = N_ASSIGN
Y_ROWS = Y_DUMP + MOE_BM
FIN_TM = 512
VMEM_LIMIT = 56 * 1024 * 1024


def _rms(x, gain):
    return x * lax.rsqrt(jnp.mean(x * x, axis=-1, keepdims=True) + RMS_EPS) * gain


def _bdot(a, b):
    return jnp.dot(a.astype(jnp.bfloat16), b, preferred_element_type=jnp.float32)


def _pack_bf16_pairs(lo, hi):
    lo_b = lax.bitcast_convert_type(lo.astype(jnp.bfloat16).astype(jnp.float32), jnp.uint32)
    hi_b = lax.bitcast_convert_type(hi.astype(jnp.bfloat16).astype(jnp.float32), jnp.uint32)
    return (lo_b >> 16) | (hi_b & jnp.uint32(0xFFFF0000))


def _unpack_bf16_pairs(u):
    lo = lax.bitcast_convert_type(u << 16, jnp.float32)
    hi = lax.bitcast_convert_type(u & jnp.uint32(0xFFFF0000), jnp.float32)
    return lo, hi


def _mix_rows(x, nrows, with_output, g1_ref, w_in_ref, conv_w_ref, conv_b_ref, w_gate_ref,
              b_gate_ref, lam8_ref, lru_gain_ref, w_pool_ref, pool_scale_ref, w_out_ref,
              ux_buf, up_buf, a_buf, b_buf, hs_buf, hstate):
    hn = _rms(x, g1_ref[...])
    proj = _bdot(hn, w_in_ref[...])
    ux = proj[:, :LRU_WIDTH]
    up = proj[:, 2 * LRU_WIDTH:]

    ux_buf[CONV_HALO:CONV_HALO + nrows, :] = ux
    xc = conv_b_ref[...] + conv_w_ref[3:4, :] * ux
    for k in range(CONV_WIDTH - 1):
        xc = xc + conv_w_ref[k:k + 1, :] * ux_buf[k * BATCH:k * BATCH + nrows, :]
    ux_buf[0:CONV_HALO, :] = ux_buf[nrows:nrows + CONV_HALO, :]

    half = LRU_WIDTH // 2
    z = [_bdot(xc[:, j * half:(j + 1) * half], w_gate_ref[j]) for j in range(2)]
    za = jnp.concatenate([z[0][:, :half], z[1][:, :half]], axis=-1) + b_gate_ref[0:1, :]
    zx = jnp.concatenate([z[0][:, half:], z[1][:, half:]], axis=-1) + b_gate_ref[1:2, :]
    r = jax.nn.sigmoid(za)
    gi = jax.nn.sigmoid(zx)
    log_a = lam8_ref[...] * r
    a = jnp.exp(log_a)
    a_buf[0:nrows, :] = a
    b_buf[0:nrows, :] = jnp.sqrt(1.0 - a * a) * (gi * xc)

    def step(t, h):
        r0 = pl.multiple_of(t * SUBLANES, SUBLANES)
        h = a_buf[pl.ds(r0, SUBLANES), :] * h + b_buf[pl.ds(r0, SUBLANES), :]
        hs_buf[pl.ds(r0, SUBLANES), :] = h
        return h

    hstate[...] = lax.fori_loop(0, nrows // SUBLANES, step, hstate[...], unroll=8)

    up_buf[POOL_HALO:POOL_HALO + nrows, :] = up
    if not with_output:
        up_buf[0:POOL_HALO, :] = up_buf[nrows:nrows + POOL_HALO, :]
        return None

    ug = proj[:, LRU_WIDTH:2 * LRU_WIDTH]
    y_lru = _rms(hs_buf[0:nrows, :] * jax.nn.gelu(ug), lru_gain_ref[...])

    pm = []
    for g, w in enumerate(POOL_WINDOWS):
        lo, hi = g * POOL_GROUP, (g + 1) * POOL_GROUP
        cur = up[:, lo:hi]
        s = cur
        for j in range(1, w):
            off = POOL_HALO - j * BATCH
            s = s + up_buf[off:off + nrows, lo:hi]
        pm.append(s * (1.0 / w) - cur)
    up_buf[0:POOL_HALO, :] = up_buf[nrows:nrows + POOL_HALO, :]
    yp = [_bdot(jnp.concatenate(pm[2 * j:2 * j + 2], axis=-1), w_pool_ref[j]) for j in range(2)]
    y_pool = _rms(jnp.concatenate(yp, axis=-1), pool_scale_ref[...])

    y = _bdot(y_lru, w_out_ref[0:LRU_WIDTH, :]) + _bdot(y_pool, w_out_ref[LRU_WIDTH:, :])
    return x + y


def _mixer_kernel(x_ref, meta_ref, g1_ref, w_in_ref, conv_w_ref, conv_b_ref, w_gate_ref,
                  b_gate_ref, lam8_ref, lru_gain_ref, w_pool_ref, pool_scale_ref, w_out_ref,
                  g2_ref, w_rt_ref, b_rt_ref, tri_ref,
                  h1_ref, xs_hbm, cnt_ref,
                  ux_buf, up_buf, a_buf, b_buf, hs_buf, hstate, running,
                  rowbuf, dest_v, dest_s, row_sem, idx_sem):
    i = pl.program_id(0)
    n_rows = TOP_K * MIX_ROWS
    mix = functools.partial(
        _mix_rows, g1_ref=g1_ref, w_in_ref=w_in_ref, conv_w_ref=conv_w_ref,
        conv_b_ref=conv_b_ref, w_gate_ref=w_gate_ref, b_gate_ref=b_gate_ref, lam8_ref=lam8_ref,
        lru_gain_ref=lru_gain_ref, w_pool_ref=w_pool_ref, pool_scale_ref=pool_scale_ref,
        w_out_ref=w_out_ref, ux_buf=ux_buf, up_buf=up_buf, a_buf=a_buf, b_buf=b_buf,
        hs_buf=hs_buf, hstate=hstate)
    idx_copy = pltpu.make_async_copy(dest_v, dest_s, idx_sem)
    round_copy = pltpu.make_async_copy(rowbuf, xs_hbm.at[pl.ds(0, n_rows * ROW_SUB)], row_sem)

    @pl.when(i == 0)
    def _():
        ux_buf[0:CONV_HALO, :] = jnp.zeros((CONV_HALO, LRU_WIDTH), jnp.float32)
        up_buf[0:POOL_HALO, :] = jnp.zeros((POOL_HALO, POOL_WIDTH), jnp.float32)
        hstate[...] = jnp.zeros_like(hstate)
        running[...] = jnp.zeros_like(running)
        mix(meta_ref[...], META_ROWS, False)
        rowbuf[...] = jnp.zeros_like(rowbuf)
        col = lax.broadcasted_iota(jnp.int32, (SUBLANES, MIX_ROWS), 1)
        sub = lax.broadcasted_iota(jnp.int32, (SUBLANES, MIX_ROWS), 0)
        dest_v[...] = (XS_DUMP + jnp.minimum(sub, TOP_K - 1) * MIX_ROWS + col) * ROW_SUB
        idx_copy.start()

    idx_copy.wait()
    for k in range(TOP_K):
        for r in range(MIX_ROWS):
            d = pl.multiple_of(dest_s[k, r], ROW_SUB)
            pltpu.make_async_copy(rowbuf.at[pl.ds((k * MIX_ROWS + r) * ROW_SUB, ROW_SUB)],
                                  xs_hbm.at[pl.ds(d, ROW_SUB)], row_sem).start()

    @pl.when(i < MIX_STEPS)
    def _():
        h1 = mix(x_ref[...], MIX_ROWS, True)
        h1_ref[...] = h1

        hn2 = _rms(h1, g2_ref[...])
        logits = _bdot(hn2, w_rt_ref[...]) + b_rt_ref[...]
        lane = lax.broadcasted_iota(jnp.int32, (MIX_ROWS, LANES), 1)
        lane_f = lane.astype(jnp.float32)
        neg = jnp.float32(-jnp.inf)
        big = jnp.float32(4 * LANES)

        is_g = (lane >= N_EXPERTS) & (lane < N_EXPERTS + N_GROUPS)
        gl = jnp.where(is_g, logits, neg)
        gmax = jnp.max(gl, axis=-1, keepdims=True)
        gidx = jnp.min(jnp.where(gl == gmax, lane_f, big), axis=-1, keepdims=True)
        p_g = 1.0 / jnp.sum(jnp.exp(gl - gmax), axis=-1, keepdims=True)
        grp = gidx.astype(jnp.int32) - N_EXPERTS

        el = jnp.where((lane >> 3) == grp, logits, neg)
        m1 = jnp.max(el, axis=-1, keepdims=True)
        i1 = jnp.min(jnp.where(el == m1, lane_f, big), axis=-1, keepdims=True)
        el2 = jnp.where(lane_f == i1, neg, el)
        m2 = jnp.max(el2, axis=-1, keepdims=True)
        i2 = jnp.min(jnp.where(el2 == m2, lane_f, big), axis=-1, keepdims=True)
        e21 = jnp.exp(m2 - m1)
        den = 1.0 + e21
        gate = (p_g * (1.0 / den), p_g * (e21 / den))

        sel = (lane_f == i1, lane_f == i2)
        onehot = jnp.where(sel[0] | sel[1], 1.0, 0.0)
        before = jnp.dot(tri_ref[...], onehot.astype(jnp.bfloat16),
                         preferred_element_type=jnp.float32) + running[0:1, :]
        rank = [jnp.sum(jnp.where(sel[k], before, 0.0), axis=-1, keepdims=True) for k in range(TOP_K)]
        running[0:1, :] = running[0:1, :] + jnp.sum(onehot, axis=0, keepdims=True)
        cnt_ref[...] = running[...]

        eid = (i1.astype(jnp.int32), i2.astype(jnp.int32))
        dst = [(eid[k] * EXPERT_CAP + rank[k].astype(jnp.int32)) * ROW_SUB for k in range(TOP_K)]
        dst_tile = jnp.where(lane == 0, dst[0], jnp.where(lane == 1, dst[1], 0))
        dst_rows = jnp.transpose(dst_tile)[0:SUBLANES, :]

        data = _pack_bf16_pairs(hn2[:, :HALF], hn2[:, HALF:])
        tok = i * MIX_ROWS + lax.broadcasted_iota(jnp.int32, (MIX_ROWS, LANES), 0)
        round_copy.wait()
        for k in range(TOP_K):
            base = k * MIX_ROWS * ROW_SUB
            for c in range(DATA_SUB):
                rowbuf[pl.ds(base + c, MIX_ROWS, stride=ROW_SUB), :] = data[:, c * LANES:(c + 1) * LANES]
            ret = (tok * TOP_K + k) * DATA_SUB
            gbits = lax.bitcast_convert_type(jnp.broadcast_to(gate[k], (MIX_ROWS, LANES)), jnp.int32)
            meta = jnp.where(lane == 0, ret, jnp.where(lane == 1, gbits, 0))
            rowbuf[pl.ds(base + DATA_SUB, MIX_ROWS, stride=ROW_SUB), :] = lax.bitcast_convert_type(
                meta, jnp.uint32)
        dest_v[...] = dst_rows
        idx_copy.start()

    @pl.when(i == MIX_STEPS)
    def _():
        round_copy.wait()


def _mixer(x_t, meta_t, g1, w_in, conv_w, conv_b, w_gate, b_gate, lam8, lru_gain, w_pool,
           pool_scale, w_out, g2, w_rt, b_rt, tri):
    last = MIX_STEPS - 1
    full = lambda a: pl.BlockSpec(a.shape, lambda i: (0,) * a.ndim)
    row_blk = lambda w: pl.BlockSpec((MIX_ROWS, w), lambda i: (jnp.minimum(i, last), 0))
    consts = (meta_t, g1, w_in, conv_w, conv_b, w_gate, b_gate, lam8, lru_gain, w_pool,
              pool_scale, w_out, g2, w_rt, b_rt, tri)
    return pl.pallas_call(
        _mixer_kernel,
        grid=(MIX_STEPS + 1,),
        in_specs=[row_blk(D_MODEL)] + [full(a) for a in consts],
        out_specs=[row_blk(D_MODEL),
                   pl.BlockSpec(memory_space=pl.ANY),
                   pl.BlockSpec((SUBLANES, LANES), lambda i: (0, 0))],
        out_shape=[jax.ShapeDtypeStruct((N_TOK, D_MODEL), jnp.float32),
                   jax.ShapeDtypeStruct((XS_ROWS * ROW_SUB, LANES), jnp.uint32),
                   jax.ShapeDtypeStruct((SUBLANES, LANES), jnp.float32)],
        scratch_shapes=[pltpu.VMEM((CONV_HALO + MIX_ROWS, LRU_WIDTH), jnp.float32),
                        pltpu.VMEM((POOL_HALO + MIX_ROWS, POOL_WIDTH), jnp.float32),
                        pltpu.VMEM((MIX_ROWS, LRU_WIDTH), jnp.float32),
                        pltpu.VMEM((MIX_ROWS, LRU_WIDTH), jnp.float32),
                        pltpu.VMEM((MIX_ROWS, LRU_WIDTH), jnp.float32),
                        pltpu.VMEM((SUBLANES, LRU_WIDTH), jnp.float32),
                        pltpu.VMEM((SUBLANES, LANES), jnp.float32),
                        pltpu.VMEM((TOP_K * MIX_ROWS * ROW_SUB, LANES), jnp.uint32),
                        pltpu.VMEM((SUBLANES, MIX_ROWS), jnp.int32),
                        pltpu.SMEM((SUBLANES, MIX_ROWS), jnp.int32),
                        pltpu.SemaphoreType.DMA,
                        pltpu.SemaphoreType.DMA],
        compiler_params=pltpu.CompilerParams(dimension_semantics=("arbitrary",),
                                             vmem_limit_bytes=VMEM_LIMIT),
        name="mixer",
    )(x_t, *consts)


def _experts_kernel(blk_ref, bexp_ref, nval_ref, x_ref, wg_ref, wu_ref, wd_ref, y_hbm,
                    wg_b, wu_b, wd_b, obuf, addr_v, addr_s, row_sem, idx_sem):
    i = pl.program_id(0)
    last = pl.num_programs(0) - 1
    idx_copy = pltpu.make_async_copy(addr_v, addr_s, idx_sem)
    round_copy = pltpu.make_async_copy(obuf, y_hbm.at[pl.ds(0, MOE_BM * DATA_SUB)], row_sem)

    @pl.when(i == 0)
    def _():
        obuf[...] = jnp.zeros_like(obuf)
        addr_v[...] = (Y_DUMP + lax.broadcasted_iota(jnp.int32, (SUBLANES, MOE_BM), 1)) * DATA_SUB
        idx_copy.start()

    idx_copy.wait()
    for r in range(MOE_BM):
        d = pl.multiple_of(addr_s[0, r], DATA_SUB)
        pltpu.make_async_copy(obuf.at[pl.ds(r * DATA_SUB, DATA_SUB)],
                              y_hbm.at[pl.ds(d, DATA_SUB)], row_sem).start()

    j = jnp.minimum(i, last - 1)

    @pl.when((i == 0) | ((i < last) & (bexp_ref[j] != bexp_ref[jnp.maximum(j - 1, 0)])))
    def _():
        wg_b[...] = wg_ref[0].astype(jnp.bfloat16)
        wu_b[...] = wu_ref[0].astype(jnp.bfloat16)
        wd_b[...] = wd_ref[0].astype(jnp.bfloat16)

    @pl.when(i < last)
    def _():
        chunks = [x_ref[pl.ds(c, MOE_BM, stride=ROW_SUB), :] for c in range(DATA_SUB)]
        parts = [_unpack_bf16_pairs(u) for u in chunks]
        x_lo = jnp.concatenate([p[0] for p in parts], axis=-1).astype(jnp.bfloat16)
        x_hi = jnp.concatenate([p[1] for p in parts], axis=-1).astype(jnp.bfloat16)
        meta = lax.bitcast_convert_type(x_ref[pl.ds(DATA_SUB, MOE_BM, stride=ROW_SUB), :], jnp.int32)

        def wdot(w_ref):
            return (jnp.dot(x_lo, w_ref[0:HALF, :], preferred_element_type=jnp.float32)
                    + jnp.dot(x_hi, w_ref[HALF:, :], preferred_element_type=jnp.float32))

        g = wdot(wg_b)
        u = wdot(wu_b)
        hid = (g * jax.nn.sigmoid(g)) * u
        gate = lax.bitcast_convert_type(meta[:, 1:2], jnp.float32)
        o = _bdot(hid, wd_b[...]) * gate
        packed = _pack_bf16_pairs(o[:, :HALF], o[:, HALF:])

        row = lax.broadcasted_iota(jnp.int32, (MOE_BM, LANES), 0)
        lane = lax.broadcasted_iota(jnp.int32, (MOE_BM, LANES), 1)
        ret = jnp.where(row < nval_ref[i], meta, (Y_DUMP + row) * DATA_SUB)
        ret_rows = jnp.transpose(jnp.where(lane == 0, ret, 0))[0:SUBLANES, :]

        round_copy.wait()
        for c in range(DATA_SUB):
            obuf[pl.ds(c, MOE_BM, stride=DATA_SUB), :] = packed[:, c * LANES:(c + 1) * LANES]
        addr_v[...] = ret_rows
        idx_copy.start()

    @pl.when(i == last)
    def _():
        round_copy.wait()


def _experts(blk_idx, blk_expert, blk_nvalid, xs, w_gate, w_up, w_down):
    nb = N_BLOCKS
    wmap = lambda i, blk, bexp, nval: (bexp[jnp.minimum(i, nb - 1)], 0, 0)
    return pl.pallas_call(
        _experts_kernel,
        grid_spec=pltpu.PrefetchScalarGridSpec(
            num_scalar_prefetch=3,
            grid=(nb + 1,),
            in_specs=[pl.BlockSpec((MOE_BM * ROW_SUB, LANES),
                                   lambda i, blk, bexp, nval: (blk[jnp.minimum(i, nb - 1)], 0)),
                      pl.BlockSpec((1, D_MODEL, EXPERT_FF), wmap),
                      pl.BlockSpec((1, D_MODEL, EXPERT_FF), wmap),
                      pl.BlockSpec((1, EXPERT_FF, D_MODEL), wmap)],
            out_specs=pl.BlockSpec(memory_space=pl.ANY),
            scratch_shapes=[pltpu.VMEM((D_MODEL, EXPERT_FF), jnp.bfloat16),
                            pltpu.VMEM((D_MODEL, EXPERT_FF), jnp.bfloat16),
                            pltpu.VMEM((EXPERT_FF, D_MODEL), jnp.bfloat16),
                            pltpu.VMEM((MOE_BM * DATA_SUB, LANES), jnp.uint32),
                            pltpu.VMEM((SUBLANES, MOE_BM), jnp.int32),
                            pltpu.SMEM((SUBLANES, MOE_BM), jnp.int32),
                            pltpu.SemaphoreType.DMA,
                            pltpu.SemaphoreType.DMA]),
        out_shape=jax.ShapeDtypeStruct((Y_ROWS * DATA_SUB, LANES), jnp.uint32),
        compiler_params=pltpu.CompilerParams(dimension_semantics=("arbitrary",),
                                             vmem_limit_bytes=VMEM_LIMIT),
        name="experts",
    )(blk_idx, blk_expert, blk_nvalid, xs, w_gate, w_up, w_down)


def _final_kernel(h1_ref, y_ref, gain_ref, out_ref):
    stride = TOP_K * DATA_SUB
    lo, hi = [], []
    for c in range(DATA_SUB):
        a = _unpack_bf16_pairs(y_ref[pl.ds(c, FIN_TM, stride=stride), :])
        b = _unpack_bf16_pairs(y_ref[pl.ds(DATA_SUB + c, FIN_TM, stride=stride), :])
        lo.append(a[0] + b[0])
        hi.append(a[1] + b[1])
    y = jnp.concatenate(lo + hi, axis=-1)
    out_ref[...] = _rms(h1_ref[...] + y, gain_ref[...])


def _final(h1, y2, final_gain):
    rows = FIN_TM * TOP_K * DATA_SUB
    return pl.pallas_call(
        _final_kernel,
        grid=(N_TOK // FIN_TM,),
        in_specs=[pl.BlockSpec((FIN_TM, D_MODEL), lambda i: (i, 0)),
                  pl.BlockSpec((rows, LANES), lambda i: (i, 0)),
                  pl.BlockSpec((1, D_MODEL), lambda i: (0, 0))],
        out_specs=pl.BlockSpec((FIN_TM, D_MODEL), lambda i: (i, 0)),
        out_shape=jax.ShapeDtypeStruct((N_TOK, D_MODEL), jnp.float32),
        compiler_params=pltpu.CompilerParams(dimension_semantics=("arbitrary",),
                                             vmem_limit_bytes=VMEM_LIMIT),
        name="final",
    )(h1, y2, final_gain)


def _block_diag(blocks):
    n = len(blocks)
    r, c = blocks[0].shape
    out = jnp.zeros((n * r, n * c), blocks[0].dtype)
    for k, blk in enumerate(blocks):
        out = out.at[k * r:(k + 1) * r, k * c:(k + 1) * c].set(blk)
    return out


def kernel(x, meta_tokens, norm1_gain, w_in, conv_w, conv_b, lru_wa, lru_ba, lru_wx, lru_bx,
           lru_lambda, lru_out_gain, pool_w, pool_scale, w_out, norm2_gain, w_group, b_group,
           w_router, b_router, w_gate, w_up, w_down, final_gain):
    assert x.shape == (BATCH, SEQ, D_MODEL) and norm1_gain.shape[0] == 1
    f32, bf16 = jnp.float32, jnp.bfloat16
    row = lambda v: v.reshape(1, -1).astype(f32)

    x_t = jnp.transpose(x, (1, 0, 2)).reshape(N_TOK, D_MODEL)
    meta_t = jnp.repeat(meta_tokens.astype(f32), BATCH, axis=0)

    heads_per = LRU_HEADS // 2
    w_gate_blk = jnp.stack([
        jnp.concatenate([_block_diag([lru_wa[0, h] for h in range(j * heads_per, (j + 1) * heads_per)]),
                         _block_diag([lru_wx[0, h] for h in range(j * heads_per, (j + 1) * heads_per)])],
                        axis=1) for j in range(2)]).astype(bf16)
    b_gate_blk = jnp.stack([lru_ba[0], lru_bx[0]]).astype(f32)
    w_pool_blk = jnp.stack([_block_diag([pool_w[0, 2 * j], pool_w[0, 2 * j + 1]])
                            for j in range(2)]).astype(bf16)
    lam8 = row(LRU_C * jax.nn.log_sigmoid(lru_lambda[0].astype(f32)))
    pad = LANES - N_EXPERTS - N_GROUPS
    w_rt = jnp.concatenate([w_router[0], w_group[0], jnp.zeros((D_MODEL, pad), f32)], axis=1).astype(bf16)
    b_rt = jnp.concatenate([b_router[0], b_group[0], jnp.zeros((pad,), f32)]).reshape(1, LANES).astype(f32)
    ridx = jnp.arange(MIX_ROWS)
    tri = (ridx[None, :] < ridx[:, None]).astype(bf16)

    h1, xs, cnt = _mixer(
        x_t, meta_t, row(norm1_gain[0]), w_in[0].astype(bf16), conv_w[0].astype(f32),
        row(conv_b[0]), w_gate_blk, b_gate_blk, lam8, row(lru_out_gain[0]), w_pool_blk,
        row(pool_scale[0]), w_out[0].astype(bf16), row(norm2_gain[0]), w_rt, b_rt, tri)

    counts = cnt[0, :N_EXPERTS].astype(jnp.int32)
    nblk = (counts + MOE_BM - 1) // MOE_BM
    blk_end = jnp.cumsum(nblk)
    step = jnp.arange(N_BLOCKS, dtype=jnp.int32)
    used = step < blk_end[-1]
    step_c = jnp.minimum(step, blk_end[-1] - 1)
    e_of = jnp.sum(blk_end[None, :] <= step_c[:, None], axis=-1).astype(jnp.int32)
    onehot_e = e_of[:, None] == jnp.arange(N_EXPERTS, dtype=jnp.int32)[None, :]
    j_of = step_c - jnp.sum(jnp.where(onehot_e, (blk_end - nblk)[None, :], 0), axis=-1)
    cnt_of = jnp.sum(jnp.where(onehot_e, counts[None, :], 0), axis=-1)
    blk_idx = (e_of * CAP_BLOCKS + j_of).astype(jnp.int32)
    blk_nvalid = jnp.where(used, jnp.clip(cnt_of - j_of * MOE_BM, 0, MOE_BM), 0).astype(jnp.int32)

    y2 = _experts(blk_idx, e_of, blk_nvalid, xs, w_gate[0], w_up[0], w_down[0])
    out_t = _final(h1, y2, row(final_gain))
    return jnp.transpose(out_t.reshape(SEQ, BATCH, D_MODEL), (1, 0, 2))
```

```python
import functools

import jax
import jax.numpy as jnp
from jax import lax
from jax.experimental import pallas as pl
from jax.experimental.pallas import tpu as pltpu

D_MODEL = 1024
BATCH = 8
SEQ = 4096
N_META = 16
LRU_WIDTH = 512
LRU_HEADS = 8
LRU_HEAD_DIM = 64
CONV_WIDTH = 4
LRU_C = 8.0
POOL_WIDTH = 512
POOL_WINDOWS = (2, 4, 8, 16)
POOL_GROUP = 128
N_GROUPS = 4
EXPERTS_PER_GROUP = 8
N_EXPERTS = 32
TOP_K = 2
EXPERT_FF = 512
RMS_EPS = 1e-6

N_TOK = BATCH * SEQ
SUBLANES = 8
LANES = 128
MIX_ROWS = 512
META_ROWS = N_META * BATCH
CONV_HALO = (CONV_WIDTH - 1) * BATCH
POOL_HALO = (max(POOL_WINDOWS) - 1) * BATCH
MIX_STEPS = N_TOK // MIX_ROWS
MIX_T = MIX_ROWS // BATCH
MOE_BM = 256
N_ASSIGN = N_TOK * TOP_K
N_BLOCKS = N_ASSIGN // MOE_BM + N_EXPERTS
EXPERT_CAP = N_TOK
CAP_BLOCKS = EXPERT_CAP // MOE_BM
ROW_SUB = 8
DATA_SUB = 4
HALF = D_MODEL // 2
XS_ROWS = N_EXPERTS * EXPERT_CAP
Y_DUMP = N_ASSIGN
Y_ROWS = Y_DUMP + 2 * MOE_BM
FIN_TM = 512
VMEM_LIMIT = 56 * 1024 * 1024


def _rms(x, gain):
    return x * lax.rsqrt(jnp.mean(x * x, axis=-1, keepdims=True) + RMS_EPS) * gain


def _bdot(a, b):
    return jnp.dot(a.astype(jnp.bfloat16), b, preferred_element_type=jnp.float32)


def _pack_bf16_pairs(lo, hi):
    lo_b = lax.bitcast_convert_type(lo.astype(jnp.bfloat16).astype(jnp.float32), jnp.uint32)
    hi_b = lax.bitcast_convert_type(hi.astype(jnp.bfloat16).astype(jnp.float32), jnp.uint32)
    return (lo_b >> 16) | (hi_b & jnp.uint32(0xFFFF0000))


def _unpack_bf16_pairs(u):
    lo = lax.bitcast_convert_type(u << 16, jnp.float32)
    hi = lax.bitcast_convert_type(u & jnp.uint32(0xFFFF0000), jnp.float32)
    return lo, hi


def _mix_rows(x, nrows, with_output, g1_ref, w_in_ref, conv_w_ref, conv_b_ref, w_gate_ref,
              b_gate_ref, lam8_ref, lru_gain_ref, w_pool_ref, pool_scale_ref, w_out_ref,
              ux_buf, up_buf, a_buf, b_buf, hs_buf, hstate):
    hn = _rms(x, g1_ref[...])
    proj = _bdot(hn, w_in_ref[...])
    ux = proj[:, :LRU_WIDTH]
    up = proj[:, 2 * LRU_WIDTH:]

    ux_buf[CONV_HALO:CONV_HALO + nrows, :] = ux
    xc = conv_b_ref[...] + conv_w_ref[3:4, :] * ux
    for k in range(CONV_WIDTH - 1):
        xc = xc + conv_w_ref[k:k + 1, :] * ux_buf[k * BATCH:k * BATCH + nrows, :]
    ux_buf[0:CONV_HALO, :] = ux_buf[nrows:nrows + CONV_HALO, :]

    half = LRU_WIDTH // 2
    z = [_bdot(xc[:, j * half:(j + 1) * half], w_gate_ref[j]) for j in range(2)]
    za = jnp.concatenate([z[0][:, :half], z[1][:, :half]], axis=-1) + b_gate_ref[0:1, :]
    zx = jnp.concatenate([z[0][:, half:], z[1][:, half:]], axis=-1) + b_gate_ref[1:2, :]
    r = jax.nn.sigmoid(za)
    gi = jax.nn.sigmoid(zx)
    log_a = lam8_ref[...] * r
    a = jnp.exp(log_a)
    a_buf[0:nrows, :] = a
    b_buf[0:nrows, :] = jnp.sqrt(1.0 - a * a) * (gi * xc)

    def step(t, h):
        r0 = pl.multiple_of(t * SUBLANES, SUBLANES)
        h = a_buf[pl.ds(r0, SUBLANES), :] * h + b_buf[pl.ds(r0, SUBLANES), :]
        hs_buf[pl.ds(r0, SUBLANES), :] = h
        return h

    hstate[...] = lax.fori_loop(0, nrows // SUBLANES, step, hstate[...], unroll=8)

    up_buf[POOL_HALO:POOL_HALO + nrows, :] = up
    if not with_output:
        up_buf[0:POOL_HALO, :] = up_buf[nrows:nrows + POOL_HALO, :]
        return None

    ug = proj[:, LRU_WIDTH:2 * LRU_WIDTH]
    y_lru = _rms(hs_buf[0:nrows, :] * jax.nn.gelu(ug), lru_gain_ref[...])

    pm = []
    for g, w in enumerate(POOL_WINDOWS):
        lo, hi = g * POOL_GROUP, (g + 1) * POOL_GROUP
        cur = up[:, lo:hi]
        s = cur
        for j in range(1, w):
            off = POOL_HALO - j * BATCH
            s = s + up_buf[off:off + nrows, lo:hi]
        pm.append(s * (1.0 / w) - cur)
    up_buf[0:POOL_HALO, :] = up_buf[nrows:nrows + POOL_HALO, :]
    yp = [_bdot(jnp.concatenate(pm[2 * j:2 * j + 2], axis=-1), w_pool_ref[j]) for j in range(2)]
    y_pool = _rms(jnp.concatenate(yp, axis=-1), pool_scale_ref[...])

    y = _bdot(y_lru, w_out_ref[0:LRU_WIDTH, :]) + _bdot(y_pool, w_out_ref[LRU_WIDTH:, :])
    return x + y


def _mixer_kernel(x_hbm, meta_ref, g1_ref, w_in_ref, conv_w_ref, conv_b_ref, w_gate_ref,
                  b_gate_ref, lam8_ref, lru_gain_ref, w_pool_ref, pool_scale_ref, w_out_ref,
                  g2_ref, w_rt_ref, b_rt_ref, tri_ref,
                  h1_ref, xs_hbm, cnt_ref,
                  ux_buf, up_buf, a_buf, b_buf, hs_buf, hstate, running,
                  xin, rowbuf, dest_v, dest_s, in_sem, row_sem, idx_sem):
    i = pl.program_id(0)
    slot = i % 2
    n_rows = TOP_K * MIX_ROWS
    mix = functools.partial(
        _mix_rows, g1_ref=g1_ref, w_in_ref=w_in_ref, conv_w_ref=conv_w_ref,
        conv_b_ref=conv_b_ref, w_gate_ref=w_gate_ref, b_gate_ref=b_gate_ref, lam8_ref=lam8_ref,
        lru_gain_ref=lru_gain_ref, w_pool_ref=w_pool_ref, pool_scale_ref=pool_scale_ref,
        w_out_ref=w_out_ref, ux_buf=ux_buf, up_buf=up_buf, a_buf=a_buf, b_buf=b_buf,
        hs_buf=hs_buf, hstate=hstate)
    idx_copy = pltpu.make_async_copy(dest_v, dest_s, idx_sem)
    round_copy = pltpu.make_async_copy(rowbuf, xs_hbm.at[pl.ds(0, n_rows * ROW_SUB)], row_sem)

    def x_copy(step, s, b):
        return pltpu.make_async_copy(x_hbm.at[b, pl.ds(step * MIX_T, MIX_T), :],
                                     xin.at[s, :, b, :], in_sem.at[s])

    @pl.when(i == 0)
    def _():
        for b in range(BATCH):
            x_copy(0, 0, b).start()
        ux_buf[0:CONV_HALO, :] = jnp.zeros((CONV_HALO, LRU_WIDTH), jnp.float32)
        up_buf[0:POOL_HALO, :] = jnp.zeros((POOL_HALO, POOL_WIDTH), jnp.float32)
        hstate[...] = jnp.zeros_like(hstate)
        running[...] = jnp.zeros_like(running)
        rowbuf[...] = jnp.zeros_like(rowbuf)
        mix(meta_ref[...], META_ROWS, False)

    @pl.when(i + 1 < MIX_STEPS)
    def _():
        for b in range(BATCH):
            x_copy(i + 1, 1 - slot, b).start()

    for b in range(BATCH):
        x_copy(i, slot, b).wait()
    x = xin[slot].reshape(MIX_ROWS, D_MODEL)
    h1 = mix(x, MIX_ROWS, True)
    h1_ref[...] = h1

    hn2 = _rms(h1, g2_ref[...])
    logits = _bdot(hn2, w_rt_ref[...]) + b_rt_ref[...]
    lane = lax.broadcasted_iota(jnp.int32, (MIX_ROWS, LANES), 1)
    lane_f = lane.astype(jnp.float32)
    neg = jnp.float32(-jnp.inf)
    big = jnp.float32(4 * LANES)

    is_g = (lane >= N_EXPERTS) & (lane < N_EXPERTS + N_GROUPS)
    gl = jnp.where(is_g, logits, neg)
    gmax = jnp.max(gl, axis=-1, keepdims=True)
    gidx = jnp.min(jnp.where(gl == gmax, lane_f, big), axis=-1, keepdims=True)
    p_g = 1.0 / jnp.sum(jnp.exp(gl - gmax), axis=-1, keepdims=True)
    grp = gidx.astype(jnp.int32) - N_EXPERTS

    el = jnp.where((lane >> 3) == grp, logits, neg)
    m1 = jnp.max(el, axis=-1, keepdims=True)
    i1 = jnp.min(jnp.where(el == m1, lane_f, big), axis=-1, keepdims=True)
    el2 = jnp.where(lane_f == i1, neg, el)
    m2 = jnp.max(el2, axis=-1, keepdims=True)
    i2 = jnp.min(jnp.where(el2 == m2, lane_f, big), axis=-1, keepdims=True)
    e21 = jnp.exp(m2 - m1)
    den = 1.0 + e21
    gate = (p_g * (1.0 / den), p_g * (e21 / den))

    sel = (lane_f == i1, lane_f == i2)
    onehot = jnp.where(sel[0] | sel[1], 1.0, 0.0)
    before = jnp.dot(tri_ref[...], onehot.astype(jnp.bfloat16),
                     preferred_element_type=jnp.float32) + running[0:1, :]
    rank = [jnp.sum(jnp.where(sel[k], before, 0.0), axis=-1, keepdims=True) for k in range(TOP_K)]
    running[0:1, :] = running[0:1, :] + jnp.sum(onehot, axis=0, keepdims=True)
    cnt_ref[...] = running[...]

    eid = (i1.astype(jnp.int32), i2.astype(jnp.int32))
    dst = [(eid[k] * EXPERT_CAP + rank[k].astype(jnp.int32)) * ROW_SUB for k in range(TOP_K)]
    dst_tile = jnp.where(lane == 0, dst[0], jnp.where(lane == 1, dst[1], 0))
    dest_v[...] = jnp.transpose(dst_tile)[0:SUBLANES, :]
    idx_copy.start()

    data = _pack_bf16_pairs(hn2[:, :HALF], hn2[:, HALF:])
    tok = i * MIX_ROWS + lax.broadcasted_iota(jnp.int32, (MIX_ROWS, LANES), 0)

    @pl.when(i > 0)
    def _():
        round_copy.wait()

    for k in range(TOP_K):
        base = k * MIX_ROWS * ROW_SUB
        for c in range(DATA_SUB):
            rowbuf[pl.ds(base + c, MIX_ROWS, stride=ROW_SUB), :] = data[:, c * LANES:(c + 1) * LANES]
        ret = (tok * TOP_K + k) * DATA_SUB
        gbits = lax.bitcast_convert_type(jnp.broadcast_to(gate[k], (MIX_ROWS, LANES)), jnp.int32)
        meta = jnp.where(lane == 0, ret, jnp.where(lane == 1, gbits, 0))
        rowbuf[pl.ds(base + DATA_SUB, MIX_ROWS, stride=ROW_SUB), :] = lax.bitcast_convert_type(
            meta, jnp.uint32)

    idx_copy.wait()
    for k in range(TOP_K):
        for r in range(MIX_ROWS):
            d = pl.multiple_of(dest_s[k, r], ROW_SUB)
            pltpu.make_async_copy(rowbuf.at[pl.ds((k * MIX_ROWS + r) * ROW_SUB, ROW_SUB)],
                                  xs_hbm.at[pl.ds(d, ROW_SUB)], row_sem).start()

    @pl.when(i == MIX_STEPS - 1)
    def _():
        round_copy.wait()


def _mixer(x, meta_t, g1, w_in, conv_w, conv_b, w_gate, b_gate, lam8, lru_gain, w_pool,
           pool_scale, w_out, g2, w_rt, b_rt, tri):
    full = lambda a: pl.BlockSpec(a.shape, lambda i: (0,) * a.ndim)
    consts = (meta_t, g1, w_in, conv_w, conv_b, w_gate, b_gate, lam8, lru_gain, w_pool,
              pool_scale, w_out, g2, w_rt, b_rt, tri)
    return pl.pallas_call(
        _mixer_kernel,
        grid=(MIX_STEPS,),
        in_specs=[pl.BlockSpec(memory_space=pl.ANY)] + [full(a) for a in consts],
        out_specs=[pl.BlockSpec((MIX_ROWS, D_MODEL), lambda i: (i, 0)),
                   pl.BlockSpec(memory_space=pl.ANY),
                   pl.BlockSpec((SUBLANES, LANES), lambda i: (0, 0))],
        out_shape=[jax.ShapeDtypeStruct((N_TOK, D_MODEL), jnp.float32),
                   jax.ShapeDtypeStruct((XS_ROWS * ROW_SUB, LANES), jnp.uint32),
                   jax.ShapeDtypeStruct((SUBLANES, LANES), jnp.float32)],
        scratch_shapes=[pltpu.VMEM((CONV_HALO + MIX_ROWS, LRU_WIDTH), jnp.float32),
                        pltpu.VMEM((POOL_HALO + MIX_ROWS, POOL_WIDTH), jnp.float32),
                        pltpu.VMEM((MIX_ROWS, LRU_WIDTH), jnp.float32),
                        pltpu.VMEM((MIX_ROWS, LRU_WIDTH), jnp.float32),
                        pltpu.VMEM((MIX_ROWS, LRU_WIDTH), jnp.float32),
                        pltpu.VMEM((SUBLANES, LRU_WIDTH), jnp.float32),
                        pltpu.VMEM((SUBLANES, LANES), jnp.float32),
                        pltpu.VMEM((2, MIX_T, BATCH, D_MODEL), jnp.float32),
                        pltpu.VMEM((TOP_K * MIX_ROWS * ROW_SUB, LANES), jnp.uint32),
                        pltpu.VMEM((SUBLANES, MIX_ROWS), jnp.int32),
                        pltpu.SMEM((SUBLANES, MIX_ROWS), jnp.int32),
                        pltpu.SemaphoreType.DMA((2,)),
                        pltpu.SemaphoreType.DMA,
                        pltpu.SemaphoreType.DMA],
        compiler_params=pltpu.CompilerParams(dimension_semantics=("arbitrary",),
                                             vmem_limit_bytes=VMEM_LIMIT),
        name="mixer",
    )(x, *consts)


def _experts_kernel(blk_ref, bexp_ref, nval_ref, x_ref, wg_ref, wu_ref, wd_ref, y_hbm,
                    wg_b, wu_b, wd_b, obuf, addr_v, addr_s, row_sem, idx_sem):
    i = pl.program_id(0)
    last = pl.num_programs(0) - 1
    slot = i % 2
    idx_copy = pltpu.make_async_copy(addr_v, addr_s, idx_sem)

    def round_copy(s):
        return pltpu.make_async_copy(obuf.at[s], y_hbm.at[pl.ds(0, MOE_BM * DATA_SUB)], row_sem.at[s])

    @pl.when((i == 0) | (bexp_ref[i] != bexp_ref[jnp.maximum(i - 1, 0)]))
    def _():
        wg_b[...] = wg_ref[0].astype(jnp.bfloat16)
        wu_b[...] = wu_ref[0].astype(jnp.bfloat16)
        wd_b[...] = wd_ref[0].astype(jnp.bfloat16)

    meta = lax.bitcast_convert_type(x_ref[pl.ds(DATA_SUB, MOE_BM, stride=ROW_SUB), :], jnp.int32)
    row = lax.broadcasted_iota(jnp.int32, (MOE_BM, LANES), 0)
    lane = lax.broadcasted_iota(jnp.int32, (MOE_BM, LANES), 1)
    ret = jnp.where(row < nval_ref[i], meta, (Y_DUMP + slot * MOE_BM + row) * DATA_SUB)
    addr_v[...] = jnp.transpose(jnp.where(lane == 0, ret, 0))[0:SUBLANES, :]
    idx_copy.start()

    chunks = [x_ref[pl.ds(c, MOE_BM, stride=ROW_SUB), :] for c in range(DATA_SUB)]
    parts = [_unpack_bf16_pairs(u) for u in chunks]
    x_lo = jnp.concatenate([p[0] for p in parts], axis=-1).astype(jnp.bfloat16)
    x_hi = jnp.concatenate([p[1] for p in parts], axis=-1).astype(jnp.bfloat16)

    def wdot(w_ref):
        return (jnp.dot(x_lo, w_ref[0:HALF, :], preferred_element_type=jnp.float32)
                + jnp.dot(x_hi, w_ref[HALF:, :], preferred_element_type=jnp.float32))

    g = wdot(wg_b)
    u = wdot(wu_b)
    hid = (g * jax.nn.sigmoid(g)) * u
    gate = lax.bitcast_convert_type(meta[:, 1:2], jnp.float32)
    o = _bdot(hid, wd_b[...]) * gate
    packed = _pack_bf16_pairs(o[:, :HALF], o[:, HALF:])

    @pl.when(i >= 2)
    def _():
        round_copy(slot).wait()

    for c in range(DATA_SUB):
        obuf[slot, pl.ds(c, MOE_BM, stride=DATA_SUB), :] = packed[:, c * LANES:(c + 1) * LANES]

    idx_copy.wait()
    for r in range(MOE_BM):
        d = pl.multiple_of(addr_s[0, r], DATA_SUB)
        pltpu.make_async_copy(obuf.at[slot, pl.ds(r * DATA_SUB, DATA_SUB)],
                              y_hbm.at[pl.ds(d, DATA_SUB)], row_sem.at[slot]).start()

    @pl.when(i == last)
    def _():
        round_copy(1 - slot).wait()
        round_copy(slot).wait()


def _experts(blk_idx, blk_expert, blk_nvalid, xs, w_gate, w_up, w_down):
    wmap = lambda i, blk, bexp, nval: (bexp[i], 0, 0)
    return pl.pallas_call(
        _experts_kernel,
        grid_spec=pltpu.PrefetchScalarGridSpec(
            num_scalar_prefetch=3,
            grid=(N_BLOCKS,),
            in_specs=[pl.BlockSpec((MOE_BM * ROW_SUB, LANES), lambda i, blk, bexp, nval: (blk[i], 0)),
                      pl.BlockSpec((1, D_MODEL, EXPERT_FF), wmap),
                      pl.BlockSpec((1, D_MODEL, EXPERT_FF), wmap),
                      pl.BlockSpec((1, EXPERT_FF, D_MODEL), wmap)],
            out_specs=pl.BlockSpec(memory_space=pl.ANY),
            scratch_shapes=[pltpu.VMEM((D_MODEL, EXPERT_FF), jnp.bfloat16),
                            pltpu.VMEM((D_MODEL, EXPERT_FF), jnp.bfloat16),
                            pltpu.VMEM((EXPERT_FF, D_MODEL), jnp.bfloat16),
                            pltpu.VMEM((2, MOE_BM * DATA_SUB, LANES), jnp.uint32),
                            pltpu.VMEM((SUBLANES, MOE_BM), jnp.int32),
                            pltpu.SMEM((SUBLANES, MOE_BM), jnp.int32),
                            pltpu.SemaphoreType.DMA((2,)),
                            pltpu.SemaphoreType.DMA]),
        out_shape=jax.ShapeDtypeStruct((Y_ROWS * DATA_SUB, LANES), jnp.uint32),
        compiler_params=pltpu.CompilerParams(dimension_semantics=("arbitrary",),
                                             vmem_limit_bytes=VMEM_LIMIT),
        name="experts",
    )(blk_idx, blk_expert, blk_nvalid, xs, w_gate, w_up, w_down)


def _final_kernel(h1_ref, y_ref, gain_ref, out_hbm, obuf, sem):
    i = pl.program_id(0)
    slot = i % 2
    stride = TOP_K * DATA_SUB

    def out_copy(step, s, b):
        return pltpu.make_async_copy(obuf.at[s, :, b, :],
                                     out_hbm.at[b, pl.ds(step * MIX_T, MIX_T), :], sem.at[s])

    lo, hi = [], []
    for c in range(DATA_SUB):
        y0 = _unpack_bf16_pairs(y_ref[pl.ds(c, FIN_TM, stride=stride), :])
        y1 = _unpack_bf16_pairs(y_ref[pl.ds(DATA_SUB + c, FIN_TM, stride=stride), :])
        lo.append(y0[0] + y1[0])
        hi.append(y0[1] + y1[1])
    y = jnp.concatenate(lo + hi, axis=-1)
    res = _rms(h1_ref[...] + y, gain_ref[...])

    @pl.when(i >= 2)
    def _():
        for b in range(BATCH):
            out_copy(i - 2, slot, b).wait()

    obuf[slot] = res.reshape(MIX_T, BATCH, D_MODEL)
    for b in range(BATCH):
        out_copy(i, slot, b).start()

    @pl.when(i == pl.num_programs(0) - 1)
    def _():
        for b in range(BATCH):
            out_copy(i - 1, 1 - slot, b).wait()
        for b in range(BATCH):
            out_copy(i, slot, b).wait()


def _final(h1, y2, final_gain):
    rows = FIN_TM * TOP_K * DATA_SUB
    return pl.pallas_call(
        _final_kernel,
        grid=(N_TOK // FIN_TM,),
        in_specs=[pl.BlockSpec((FIN_TM, D_MODEL), lambda i: (i, 0)),
                  pl.BlockSpec((rows, LANES), lambda i: (i, 0)),
                  pl.BlockSpec((1, D_MODEL), lambda i: (0, 0))],
        out_specs=pl.BlockSpec(memory_space=pl.ANY),
        out_shape=jax.ShapeDtypeStruct((BATCH, SEQ, D_MODEL), jnp.float32),
        scratch_shapes=[pltpu.VMEM((2, MIX_T, BATCH, D_MODEL), jnp.float32),
                        pltpu.SemaphoreType.DMA((2,))],
        compiler_params=pltpu.CompilerParams(dimension_semantics=("arbitrary",),
                                             vmem_limit_bytes=VMEM_LIMIT),
        name="final",
    )(h1, y2, final_gain)


def _block_diag(blocks):
    n = len(blocks)
    r, c = blocks[0].shape
    out = jnp.zeros((n * r, n * c), blocks[0].dtype)
    for k, blk in enumerate(blocks):
        out = out.at[k * r:(k + 1) * r, k * c:(k + 1) * c].set(blk)
    return out


def kernel(x, meta_tokens, norm1_gain, w_in, conv_w, conv_b, lru_wa, lru_ba, lru_wx, lru_bx,
           lru_lambda, lru_out_gain, pool_w, pool_scale, w_out, norm2_gain, w_group, b_group,
           w_router, b_router, w_gate, w_up, w_down, final_gain):
    assert x.shape == (BATCH, SEQ, D_MODEL) and norm1_gain.shape[0] == 1
    f32, bf16 = jnp.float32, jnp.bfloat16
    row = lambda v: v.reshape(1, -1).astype(f32)

    meta_t = jnp.repeat(meta_tokens.astype(f32), BATCH, axis=0)

    heads_per = LRU_HEADS // 2
    w_gate_blk = jnp.stack([
        jnp.concatenate([_block_diag([lru_wa[0, h] for h in range(j * heads_per, (j + 1) * heads_per)]),
                         _block_diag([lru_wx[0, h] for h in range(j * heads_per, (j + 1) * heads_per)])],
                        axis=1) for j in range(2)]).astype(bf16)
    b_gate_blk = jnp.stack([lru_ba[0], lru_bx[0]]).astype(f32)
    w_pool_blk = jnp.stack([_block_diag([pool_w[0, 2 * j], pool_w[0, 2 * j + 1]])
                            for j in range(2)]).astype(bf16)
    lam8 = row(LRU_C * jax.nn.log_sigmoid(lru_lambda[0].astype(f32)))
    pad = LANES - N_EXPERTS - N_GROUPS
    w_rt = jnp.concatenate([w_router[0], w_group[0], jnp.zeros((D_MODEL, pad), f32)], axis=1).astype(bf16)
    b_rt = jnp.concatenate([b_router[0], b_group[0], jnp.zeros((pad,), f32)]).reshape(1, LANES).astype(f32)
    ridx = jnp.arange(MIX_ROWS)
    tri = (ridx[None, :] < ridx[:, None]).astype(bf16)

    h1, xs, cnt = _mixer(
        x, meta_t, row(norm1_gain[0]), w_in[0].astype(bf16), conv_w[0].astype(f32),
        row(conv_b[0]), w_gate_blk, b_gate_blk, lam8, row(lru_out_gain[0]), w_pool_blk,
        row(pool_scale[0]), w_out[0].astype(bf16), row(norm2_gain[0]), w_rt, b_rt, tri)

    counts = cnt[0, :N_EXPERTS].astype(jnp.int32)
    nblk = (counts + MOE_BM - 1) // MOE_BM
    blk_end = jnp.cumsum(nblk)
    step = jnp.arange(N_BLOCKS, dtype=jnp.int32)
    used = step < blk_end[-1]
    step_c = jnp.minimum(step, blk_end[-1] - 1)
    e_of = jnp.sum(blk_end[None, :] <= step_c[:, None], axis=-1).astype(jnp.int32)
    onehot_e = e_of[:, None] == jnp.arange(N_EXPERTS, dtype=jnp.int32)[None, :]
    j_of = step_c - jnp.sum(jnp.where(onehot_e, (blk_end - nblk)[None, :], 0), axis=-1)
    cnt_of = jnp.sum(jnp.where(onehot_e, counts[None, :], 0), axis=-1)
    blk_idx = (e_of * CAP_BLOCKS + j_of).astype(jnp.int32)
    blk_nvalid = jnp.where(used, jnp.clip(cnt_of - j_of * MOE_BM, 0, MOE_BM), 0).astype(jnp.int32)

    y2 = _experts(blk_idx, e_of, blk_nvalid, xs, w_gate[0], w_up[0], w_down[0])
    return _final(h1, y2, row(final_gain))
```

```python
import functools

import jax
import jax.numpy as jnp
from jax import lax
from jax.experimental import pallas as pl
from jax.experimental.pallas import tpu as pltpu

D_MODEL = 1024
BATCH = 8
SEQ = 4096
N_META = 16
LRU_WIDTH = 512
LRU_HEADS = 8
LRU_HEAD_DIM = 64
CONV_WIDTH = 4
LRU_C = 8.0
POOL_WIDTH = 512
POOL_WINDOWS = (2, 4, 8, 16)
POOL_GROUP = 128
N_GROUPS = 4
EXPERTS_PER_GROUP = 8
N_EXPERTS = 32
TOP_K = 2
EXPERT_FF = 512
RMS_EPS = 1e-6

N_TOK = BATCH * SEQ
SUBLANES = 8
LANES = 128
MIX_ROWS = 512
META_ROWS = N_META * BATCH
CONV_HALO = (CONV_WIDTH - 1) * BATCH
POOL_HALO = (max(POOL_WINDOWS) - 1) * BATCH
MIX_STEPS = N_TOK // MIX_ROWS
MIX_T = MIX_ROWS // BATCH
MOE_BM = 256
N_ASSIGN = N_TOK * TOP_K
N_BLOCKS = N_ASSIGN // MOE_BM + N_EXPERTS
EXPERT_CAP = N_TOK
CAP_BLOCKS = EXPERT_CAP // MOE_BM
ROW_SUB = 8
DATA_SUB = 4
HALF = D_MODEL // 2
XS_ROWS = N_EXPERTS * EXPERT_CAP
Y_DUMP = N_ASSIGN
Y_ROWS = Y_DUMP + 2 * MOE_BM
FIN_TM = 512
VMEM_LIMIT = 56 * 1024 * 1024


def _rms(x, gain):
    return x * lax.rsqrt(jnp.mean(x * x, axis=-1, keepdims=True) + RMS_EPS) * gain


def _bdot(a, b):
    return jnp.dot(a.astype(jnp.bfloat16), b, preferred_element_type=jnp.float32)


def _pack_bf16_pairs(lo, hi):
    lo_b = lax.bitcast_convert_type(lo.astype(jnp.bfloat16).astype(jnp.float32), jnp.uint32)
    hi_b = lax.bitcast_convert_type(hi.astype(jnp.bfloat16).astype(jnp.float32), jnp.uint32)
    return (lo_b >> 16) | (hi_b & jnp.uint32(0xFFFF0000))


def _unpack_bf16_pairs(u):
    lo = lax.bitcast_convert_type(u << 16, jnp.float32)
    hi = lax.bitcast_convert_type(u & jnp.uint32(0xFFFF0000), jnp.float32)
    return lo, hi


def _mix_rows(x, nrows, with_output, g1_ref, w_in_ref, conv_w_ref, conv_b_ref, w_gate_ref,
              b_gate_ref, lam8_ref, lru_gain_ref, w_pool_ref, pool_scale_ref, w_out_ref,
              ux_buf, up_buf, a_buf, b_buf, hs_buf, hstate):
    hn = _rms(x, g1_ref[...])
    proj = _bdot(hn, w_in_ref[...])
    ux = proj[:, :LRU_WIDTH]
    up = proj[:, 2 * LRU_WIDTH:]

    ux_buf[CONV_HALO:CONV_HALO + nrows, :] = ux
    xc = conv_b_ref[...] + conv_w_ref[3:4, :] * ux
    for k in range(CONV_WIDTH - 1):
        xc = xc + conv_w_ref[k:k + 1, :] * ux_buf[k * BATCH:k * BATCH + nrows, :]
    ux_buf[0:CONV_HALO, :] = ux_buf[nrows:nrows + CONV_HALO, :]

    half = LRU_WIDTH // 2
    z = [_bdot(xc[:, j * half:(j + 1) * half], w_gate_ref[j]) for j in range(2)]
    za = jnp.concatenate([z[0][:, :half], z[1][:, :half]], axis=-1) + b_gate_ref[0:1, :]
    zx = jnp.concatenate([z[0][:, half:], z[1][:, half:]], axis=-1) + b_gate_ref[1:2, :]
    r = jax.nn.sigmoid(za)
    gi = jax.nn.sigmoid(zx)
    log_a = lam8_ref[...] * r
    a = jnp.exp(log_a)
    a_buf[0:nrows, :] = a
    b_buf[0:nrows, :] = jnp.sqrt(1.0 - a * a) * (gi * xc)

    def step(t, h):
        r0 = pl.multiple_of(t * SUBLANES, SUBLANES)
        h = a_buf[pl.ds(r0, SUBLANES), :] * h + b_buf[pl.ds(r0, SUBLANES), :]
        hs_buf[pl.ds(r0, SUBLANES), :] = h
        return h

    hstate[...] = lax.fori_loop(0, nrows // SUBLANES, step, hstate[...], unroll=8)

    up_buf[POOL_HALO:POOL_HALO + nrows, :] = up
    if not with_output:
        up_buf[0:POOL_HALO, :] = up_buf[nrows:nrows + POOL_HALO, :]
        return None

    ug = proj[:, LRU_WIDTH:2 * LRU_WIDTH]
    y_lru = _rms(hs_buf[0:nrows, :] * jax.nn.gelu(ug), lru_gain_ref[...])

    pm = []
    for g, w in enumerate(POOL_WINDOWS):
        lo, hi = g * POOL_GROUP, (g + 1) * POOL_GROUP
        cur = up[:, lo:hi]
        s = cur
        for j in range(1, w):
            off = POOL_HALO - j * BATCH
            s = s + up_buf[off:off + nrows, lo:hi]
        pm.append(s * (1.0 / w) - cur)
    up_buf[0:POOL_HALO, :] = up_buf[nrows:nrows + POOL_HALO, :]
    yp = [_bdot(jnp.concatenate(pm[2 * j:2 * j + 2], axis=-1), w_pool_ref[j]) for j in range(2)]
    y_pool = _rms(jnp.concatenate(yp, axis=-1), pool_scale_ref[...])

    y = _bdot(y_lru, w_out_ref[0:LRU_WIDTH, :]) + _bdot(y_pool, w_out_ref[LRU_WIDTH:, :])
    return x + y


def _mixer_kernel(x_hbm, meta_ref, g1_ref, w_in_ref, conv_w_ref, conv_b_ref, w_gate_ref,
                  b_gate_ref, lam8_ref, lru_gain_ref, w_pool_ref, pool_scale_ref, w_out_ref,
                  g2_ref, w_rt_ref, b_rt_ref, tri_ref,
                  h1_ref, xs_hbm, cnt_ref,
                  ux_buf, up_buf, a_buf, b_buf, hs_buf, hstate, running,
                  xin, rowbuf, dest_v, dest_s, in_sem, row_sem, idx_sem):
    i = pl.program_id(0)
    slot = i % 2
    n_rows = TOP_K * MIX_ROWS
    mix = functools.partial(
        _mix_rows, g1_ref=g1_ref, w_in_ref=w_in_ref, conv_w_ref=conv_w_ref,
        conv_b_ref=conv_b_ref, w_gate_ref=w_gate_ref, b_gate_ref=b_gate_ref, lam8_ref=lam8_ref,
        lru_gain_ref=lru_gain_ref, w_pool_ref=w_pool_ref, pool_scale_ref=pool_scale_ref,
        w_out_ref=w_out_ref, ux_buf=ux_buf, up_buf=up_buf, a_buf=a_buf, b_buf=b_buf,
        hs_buf=hs_buf, hstate=hstate)
    idx_copy = pltpu.make_async_copy(dest_v, dest_s, idx_sem)
    round_copy = pltpu.make_async_copy(rowbuf, xs_hbm.at[pl.ds(0, n_rows * ROW_SUB)], row_sem)

    def x_copy(step, s, b):
        return pltpu.make_async_copy(x_hbm.at[b, pl.ds(step * MIX_T, MIX_T), :],
                                     xin.at[s, :, b, :], in_sem.at[s])

    @pl.when(i == 0)
    def _():
        for b in range(BATCH):
            x_copy(0, 0, b).start()
        ux_buf[0:CONV_HALO, :] = jnp.zeros((CONV_HALO, LRU_WIDTH), jnp.float32)
        up_buf[0:POOL_HALO, :] = jnp.zeros((POOL_HALO, POOL_WIDTH), jnp.float32)
        hstate[...] = jnp.zeros_like(hstate)
        running[...] = jnp.zeros_like(running)
        rowbuf[...] = jnp.zeros_like(rowbuf)
        mix(meta_ref[...], META_ROWS, False)

    @pl.when(i + 1 < MIX_STEPS)
    def _():
        for b in range(BATCH):
            x_copy(i + 1, 1 - slot, b).start()

    for b in range(BATCH):
        x_copy(i, slot, b).wait()
    x = xin[slot].reshape(MIX_ROWS, D_MODEL)
    h1 = mix(x, MIX_ROWS, True)
    h1_ref[...] = h1

    hn2 = _rms(h1, g2_ref[...])
    logits = _bdot(hn2, w_rt_ref[...]) + b_rt_ref[...]
    lane = lax.broadcasted_iota(jnp.int32, (MIX_ROWS, LANES), 1)
    lane_f = lane.astype(jnp.float32)
    neg = jnp.float32(-jnp.inf)
    big = jnp.float32(4 * LANES)

    is_g = (lane >= N_EXPERTS) & (lane < N_EXPERTS + N_GROUPS)
    gl = jnp.where(is_g, logits, neg)
    gmax = jnp.max(gl, axis=-1, keepdims=True)
    gidx = jnp.min(jnp.where(gl == gmax, lane_f, big), axis=-1, keepdims=True)
    p_g = 1.0 / jnp.sum(jnp.exp(gl - gmax), axis=-1, keepdims=True)
    grp = gidx.astype(jnp.int32) - N_EXPERTS

    el = jnp.where((lane >> 3) == grp, logits, neg)
    m1 = jnp.max(el, axis=-1, keepdims=True)
    i1 = jnp.min(jnp.where(el == m1, lane_f, big), axis=-1, keepdims=True)
    el2 = jnp.where(lane_f == i1, neg, el)
    m2 = jnp.max(el2, axis=-1, keepdims=True)
    i2 = jnp.min(jnp.where(el2 == m2, lane_f, big), axis=-1, keepdims=True)
    e21 = jnp.exp(m2 - m1)
    den = 1.0 + e21
    gate = (p_g * (1.0 / den), p_g * (e21 / den))

    sel = (lane_f == i1, lane_f == i2)
    onehot = jnp.where(sel[0] | sel[1], 1.0, 0.0)
    before = jnp.dot(tri_ref[...], onehot.astype(jnp.bfloat16),
                     preferred_element_type=jnp.float32) + running[0:1, :]
    rank = [jnp.sum(jnp.where(sel[k], before, 0.0), axis=-1, keepdims=True) for k in range(TOP_K)]
    running[0:1, :] = running[0:1, :] + jnp.sum(onehot, axis=0, keepdims=True)
    cnt_ref[...] = running[...]

    eid = (i1.astype(jnp.int32), i2.astype(jnp.int32))
    dst = [(eid[k] * EXPERT_CAP + rank[k].astype(jnp.int32)) * ROW_SUB for k in range(TOP_K)]
    dst_tile = jnp.where(lane == 0, dst[0], jnp.where(lane == 1, dst[1], 0))
    dest_v[...] = jnp.transpose(dst_tile)[0:SUBLANES, :]
    idx_copy.start()

    data = _pack_bf16_pairs(hn2[:, :HALF], hn2[:, HALF:])
    tok = i * MIX_ROWS + lax.broadcasted_iota(jnp.int32, (MIX_ROWS, LANES), 0)

    @pl.when(i > 0)
    def _():
        round_copy.wait()

    for k in range(TOP_K):
        base = k * MIX_ROWS * ROW_SUB
        for c in range(DATA_SUB):
            rowbuf[pl.ds(base + c, MIX_ROWS, stride=ROW_SUB), :] = data[:, c * LANES:(c + 1) * LANES]
        ret = (tok * TOP_K + k) * DATA_SUB
        gbits = lax.bitcast_convert_type(jnp.broadcast_to(gate[k], (MIX_ROWS, LANES)), jnp.int32)
        meta = jnp.where(lane == 0, ret, jnp.where(lane == 1, gbits, 0))
        rowbuf[pl.ds(base + DATA_SUB, MIX_ROWS, stride=ROW_SUB), :] = lax.bitcast_convert_type(
            meta, jnp.uint32)

    idx_copy.wait()
    for k in range(TOP_K):
        for r in range(MIX_ROWS):
            d = pl.multiple_of(dest_s[k, r], ROW_SUB)
            pltpu.make_async_copy(rowbuf.at[pl.ds((k * MIX_ROWS + r) * ROW_SUB, ROW_SUB)],
                                  xs_hbm.at[pl.ds(d, ROW_SUB)], row_sem).start(priority=r % 2)

    @pl.when(i == MIX_STEPS - 1)
    def _():
        round_copy.wait()


def _mixer(x, meta_t, g1, w_in, conv_w, conv_b, w_gate, b_gate, lam8, lru_gain, w_pool,
           pool_scale, w_out, g2, w_rt, b_rt, tri):
    full = lambda a: pl.BlockSpec(a.shape, lambda i: (0,) * a.ndim)
    consts = (meta_t, g1, w_in, conv_w, conv_b, w_gate, b_gate, lam8, lru_gain, w_pool,
              pool_scale, w_out, g2, w_rt, b_rt, tri)
    return pl.pallas_call(
        _mixer_kernel,
        grid=(MIX_STEPS,),
        in_specs=[pl.BlockSpec(memory_space=pl.ANY)] + [full(a) for a in consts],
        out_specs=[pl.BlockSpec((MIX_ROWS, D_MODEL), lambda i: (i, 0)),
                   pl.BlockSpec(memory_space=pl.ANY),
                   pl.BlockSpec((SUBLANES, LANES), lambda i: (0, 0))],
        out_shape=[jax.ShapeDtypeStruct((N_TOK, D_MODEL), jnp.float32),
                   jax.ShapeDtypeStruct((XS_ROWS * ROW_SUB, LANES), jnp.uint32),
                   jax.ShapeDtypeStruct((SUBLANES, LANES), jnp.float32)],
        scratch_shapes=[pltpu.VMEM((CONV_HALO + MIX_ROWS, LRU_WIDTH), jnp.float32),
                        pltpu.VMEM((POOL_HALO + MIX_ROWS, POOL_WIDTH), jnp.float32),
                        pltpu.VMEM((MIX_ROWS, LRU_WIDTH), jnp.float32),
                        pltpu.VMEM((MIX_ROWS, LRU_WIDTH), jnp.float32),
                        pltpu.VMEM((MIX_ROWS, LRU_WIDTH), jnp.float32),
                        pltpu.VMEM((SUBLANES, LRU_WIDTH), jnp.float32),
                        pltpu.VMEM((SUBLANES, LANES), jnp.float32),
                        pltpu.VMEM((2, MIX_T, BATCH, D_MODEL), jnp.float32),
                        pltpu.VMEM((TOP_K * MIX_ROWS * ROW_SUB, LANES), jnp.uint32),
                        pltpu.VMEM((SUBLANES, MIX_ROWS), jnp.int32),
                        pltpu.SMEM((SUBLANES, MIX_ROWS), jnp.int32),
                        pltpu.SemaphoreType.DMA((2,)),
                        pltpu.SemaphoreType.DMA,
                        pltpu.SemaphoreType.DMA],
        compiler_params=pltpu.CompilerParams(dimension_semantics=("arbitrary",),
                                             vmem_limit_bytes=VMEM_LIMIT),
        name="mixer",
    )(x, *consts)


def _experts_kernel(blk_ref, bexp_ref, nval_ref, x_ref, wg_ref, wu_ref, wd_ref, y_hbm,
                    wg_b, wu_b, wd_b, obuf, addr_v, addr_s, row_sem, idx_sem):
    i = pl.program_id(0)
    last = pl.num_programs(0) - 1
    slot = i % 2
    idx_copy = pltpu.make_async_copy(addr_v, addr_s, idx_sem)

    def round_copy(s):
        return pltpu.make_async_copy(obuf.at[s], y_hbm.at[pl.ds(0, MOE_BM * DATA_SUB)], row_sem.at[s])

    @pl.when((i == 0) | (bexp_ref[i] != bexp_ref[jnp.maximum(i - 1, 0)]))
    def _():
        wg_b[...] = wg_ref[0].astype(jnp.bfloat16)
        wu_b[...] = wu_ref[0].astype(jnp.bfloat16)
        wd_b[...] = wd_ref[0].astype(jnp.bfloat16)

    meta = lax.bitcast_convert_type(x_ref[pl.ds(DATA_SUB, MOE_BM, stride=ROW_SUB), :], jnp.int32)
    row = lax.broadcasted_iota(jnp.int32, (MOE_BM, LANES), 0)
    lane = lax.broadcasted_iota(jnp.int32, (MOE_BM, LANES), 1)
    ret = jnp.where(row < nval_ref[i], meta, (Y_DUMP + slot * MOE_BM + row) * DATA_SUB)
    addr_v[...] = jnp.transpose(jnp.where(lane == 0, ret, 0))[0:SUBLANES, :]
    idx_copy.start()

    chunks = [x_ref[pl.ds(c, MOE_BM, stride=ROW_SUB), :] for c in range(DATA_SUB)]
    parts = [_unpack_bf16_pairs(u) for u in chunks]
    x_lo = jnp.concatenate([p[0] for p in parts], axis=-1).astype(jnp.bfloat16)
    x_hi = jnp.concatenate([p[1] for p in parts], axis=-1).astype(jnp.bfloat16)

    def wdot(w_ref):
        return (jnp.dot(x_lo, w_ref[0:HALF, :], preferred_element_type=jnp.float32)
                + jnp.dot(x_hi, w_ref[HALF:, :], preferred_element_type=jnp.float32))

    g = wdot(wg_b)
    u = wdot(wu_b)
    hid = (g * jax.nn.sigmoid(g)) * u
    gate = lax.bitcast_convert_type(meta[:, 1:2], jnp.float32)
    o = _bdot(hid, wd_b[...]) * gate
    packed = _pack_bf16_pairs(o[:, :HALF], o[:, HALF:])

    @pl.when(i >= 2)
    def _():
        round_copy(slot).wait()

    for c in range(DATA_SUB):
        obuf[slot, pl.ds(c, MOE_BM, stride=DATA_SUB), :] = packed[:, c * LANES:(c + 1) * LANES]

    idx_copy.wait()
    for r in range(MOE_BM):
        d = pl.multiple_of(addr_s[0, r], DATA_SUB)
        pltpu.make_async_copy(obuf.at[slot, pl.ds(r * DATA_SUB, DATA_SUB)],
                              y_hbm.at[pl.ds(d, DATA_SUB)], row_sem.at[slot]).start(priority=r % 2)

    @pl.when(i == last)
    def _():
        round_copy(1 - slot).wait()
        round_copy(slot).wait()


def _experts(blk_idx, blk_expert, blk_nvalid, xs, w_gate, w_up, w_down):
    wmap = lambda i, blk, bexp, nval: (bexp[i], 0, 0)
    return pl.pallas_call(
        _experts_kernel,
        grid_spec=pltpu.PrefetchScalarGridSpec(
            num_scalar_prefetch=3,
            grid=(N_BLOCKS,),
            in_specs=[pl.BlockSpec((MOE_BM * ROW_SUB, LANES), lambda i, blk, bexp, nval: (blk[i], 0)),
                      pl.BlockSpec((1, D_MODEL, EXPERT_FF), wmap),
                      pl.BlockSpec((1, D_MODEL, EXPERT_FF), wmap),
                      pl.BlockSpec((1, EXPERT_FF, D_MODEL), wmap)],
            out_specs=pl.BlockSpec(memory_space=pl.ANY),
            scratch_shapes=[pltpu.VMEM((D_MODEL, EXPERT_FF), jnp.bfloat16),
                            pltpu.VMEM((D_MODEL, EXPERT_FF), jnp.bfloat16),
                            pltpu.VMEM((EXPERT_FF, D_MODEL), jnp.bfloat16),
                            pltpu.VMEM((2, MOE_BM * DATA_SUB, LANES), jnp.uint32),
                            pltpu.VMEM((SUBLANES, MOE_BM), jnp.int32),
                            pltpu.SMEM((SUBLANES, MOE_BM), jnp.int32),
                            pltpu.SemaphoreType.DMA((2,)),
                            pltpu.SemaphoreType.DMA]),
        out_shape=jax.ShapeDtypeStruct((Y_ROWS * DATA_SUB, LANES), jnp.uint32),
        compiler_params=pltpu.CompilerParams(dimension_semantics=("arbitrary",),
                                             vmem_limit_bytes=VMEM_LIMIT),
        name="experts",
    )(blk_idx, blk_expert, blk_nvalid, xs, w_gate, w_up, w_down)


def _final_kernel(h1_ref, y_ref, gain_ref, out_hbm, obuf, sem):
    i = pl.program_id(0)
    slot = i % 2
    stride = TOP_K * DATA_SUB

    def out_copy(step, s, b):
        return pltpu.make_async_copy(obuf.at[s, :, b, :],
                                     out_hbm.at[b, pl.ds(step * MIX_T, MIX_T), :], sem.at[s])

    lo, hi = [], []
    for c in range(DATA_SUB):
        y0 = _unpack_bf16_pairs(y_ref[pl.ds(c, FIN_TM, stride=stride), :])
        y1 = _unpack_bf16_pairs(y_ref[pl.ds(DATA_SUB + c, FIN_TM, stride=stride), :])
        lo.append(y0[0] + y1[0])
        hi.append(y0[1] + y1[1])
    y = jnp.concatenate(lo + hi, axis=-1)
    res = _rms(h1_ref[...] + y, gain_ref[...])

    @pl.when(i >= 2)
    def _():
        for b in range(BATCH):
            out_copy(i - 2, slot, b).wait()

    obuf[slot] = res.reshape(MIX_T, BATCH, D_MODEL)
    for b in range(BATCH):
        out_copy(i, slot, b).start()

    @pl.when(i == pl.num_programs(0) - 1)
    def _():
        for b in range(BATCH):
            out_copy(i - 1, 1 - slot, b).wait()
        for b in range(BATCH):
            out_copy(i, slot, b).wait()


def _final(h1, y2, final_gain):
    rows = FIN_TM * TOP_K * DATA_SUB
    return pl.pallas_call(
        _final_kernel,
        grid=(N_TOK // FIN_TM,),
        in_specs=[pl.BlockSpec((FIN_TM, D_MODEL), lambda i: (i, 0)),
                  pl.BlockSpec((rows, LANES), lambda i: (i, 0)),
                  pl.BlockSpec((1, D_MODEL), lambda i: (0, 0))],
        out_specs=pl.BlockSpec(memory_space=pl.ANY),
        out_shape=jax.ShapeDtypeStruct((BATCH, SEQ, D_MODEL), jnp.float32),
        scratch_shapes=[pltpu.VMEM((2, MIX_T, BATCH, D_MODEL), jnp.float32),
                        pltpu.SemaphoreType.DMA((2,))],
        compiler_params=pltpu.CompilerParams(dimension_semantics=("arbitrary",),
                                             vmem_limit_bytes=VMEM_LIMIT),
        name="final",
    )(h1, y2, final_gain)


def _block_diag(blocks):
    n = len(blocks)
    r, c = blocks[0].shape
    out = jnp.zeros((n * r, n * c), blocks[0].dtype)
    for k, blk in enumerate(blocks):
        out = out.at[k * r:(k + 1) * r, k * c:(k + 1) * c].set(blk)
    return out


def kernel(x, meta_tokens, norm1_gain, w_in, conv_w, conv_b, lru_wa, lru_ba, lru_wx, lru_bx,
           lru_lambda, lru_out_gain, pool_w, pool_scale, w_out, norm2_gain, w_group, b_group,
           w_router, b_router, w_gate, w_up, w_down, final_gain):
    assert x.shape == (BATCH, SEQ, D_MODEL) and norm1_gain.shape[0] == 1
    f32, bf16 = jnp.float32, jnp.bfloat16
    row = lambda v: v.reshape(1, -1).astype(f32)

    meta_t = jnp.repeat(meta_tokens.astype(f32), BATCH, axis=0)

    heads_per = LRU_HEADS // 2
    w_gate_blk = jnp.stack([
        jnp.concatenate([_block_diag([lru_wa[0, h] for h in range(j * heads_per, (j + 1) * heads_per)]),
                         _block_diag([lru_wx[0, h] for h in range(j * heads_per, (j + 1) * heads_per)])],
                        axis=1) for j in range(2)]).astype(bf16)
    b_gate_blk = jnp.stack([lru_ba[0], lru_bx[0]]).astype(f32)
    w_pool_blk = jnp.stack([_block_diag([pool_w[0, 2 * j], pool_w[0, 2 * j + 1]])
                            for j in range(2)]).astype(bf16)
    lam8 = row(LRU_C * jax.nn.log_sigmoid(lru_lambda[0].astype(f32)))
    pad = LANES - N_EXPERTS - N_GROUPS
    w_rt = jnp.concatenate([w_router[0], w_group[0], jnp.zeros((D_MODEL, pad), f32)], axis=1).astype(bf16)
    b_rt = jnp.concatenate([b_router[0], b_group[0], jnp.zeros((pad,), f32)]).reshape(1, LANES).astype(f32)
    ridx = jnp.arange(MIX_ROWS)
    tri = (ridx[None, :] < ridx[:, None]).astype(bf16)

    h1, xs, cnt = _mixer(
        x, meta_t, row(norm1_gain[0]), w_in[0].astype(bf16), conv_w[0].astype(f32),
        row(conv_b[0]), w_gate_blk, b_gate_blk, lam8, row(lru_out_gain[0]), w_pool_blk,
        row(pool_scale[0]), w_out[0].astype(bf16), row(norm2_gain[0]), w_rt, b_rt, tri)

    counts = cnt[0, :N_EXPERTS].astype(jnp.int32)
    nblk = (counts + MOE_BM - 1) // MOE_BM
    blk_end = jnp.cumsum(nblk)
    step = jnp.arange(N_BLOCKS, dtype=jnp.int32)
    used = step < blk_end[-1]
    step_c = jnp.minimum(step, blk_end[-1] - 1)
    e_of = jnp.sum(blk_end[None, :] <= step_c[:, None], axis=-1).astype(jnp.int32)
    onehot_e = e_of[:, None] == jnp.arange(N_EXPERTS, dtype=jnp.int32)[None, :]
    j_of = step_c - jnp.sum(jnp.where(onehot_e, (blk_end - nblk)[None, :], 0), axis=-1)
    cnt_of = jnp.sum(jnp.where(onehot_e, counts[None, :], 0), axis=-1)
    blk_idx = (e_of * CAP_BLOCKS + j_of).astype(jnp.int32)
    blk_nvalid = jnp.where(used, jnp.clip(cnt_of - j_of * MOE_BM, 0, MOE_BM), 0).astype(jnp.int32)

    y2 = _experts(blk_idx, e_of, blk_nvalid, xs, w_gate[0], w_up[0], w_down[0])
    return _final(h1, y2, row(final_gain))
```

```python
import functools

import jax
import jax.numpy as jnp
from jax import lax
from jax.experimental import pallas as pl
from jax.experimental.pallas import tpu as pltpu

D_MODEL = 1024
BATCH = 8
SEQ = 4096
N_META = 16
LRU_WIDTH = 512
LRU_HEADS = 8
LRU_HEAD_DIM = 64
CONV_WIDTH = 4
LRU_C = 8.0
POOL_WIDTH = 512
POOL_WINDOWS = (2, 4, 8, 16)
POOL_GROUP = 128
N_GROUPS = 4
EXPERTS_PER_GROUP = 8
N_EXPERTS = 32
TOP_K = 2
EXPERT_FF = 512
RMS_EPS = 1e-6

N_TOK = BATCH * SEQ
SUBLANES = 8
LANES = 128
MIX_ROWS = 512
META_ROWS = N_META * BATCH
CONV_HALO = (CONV_WIDTH - 1) * BATCH
POOL_HALO = (max(POOL_WINDOWS) - 1) * BATCH
MIX_STEPS = N_TOK // MIX_ROWS
MIX_T = MIX_ROWS // BATCH
MOE_BM = 512
N_ASSIGN = N_TOK * TOP_K
N_BLOCKS = N_ASSIGN // MOE_BM + N_EXPERTS
EXPERT_CAP = N_TOK
CAP_BLOCKS = EXPERT_CAP // MOE_BM
ROW_SUB = 8
DATA_SUB = 4
HALF = D_MODEL // 2
XS_ROWS = N_EXPERTS * EXPERT_CAP
Y_DUMP = N_ASSIGN
Y_ROWS = Y_DUMP + 2 * MOE_BM
FIN_TM = 512
VMEM_LIMIT = 56 * 1024 * 1024


def _rms(x, gain):
    return x * lax.rsqrt(jnp.mean(x * x, axis=-1, keepdims=True) + RMS_EPS) * gain


def _bdot(a, b):
    return jnp.dot(a.astype(jnp.bfloat16), b, preferred_element_type=jnp.float32)


def _pack_bf16_pairs(lo, hi):
    lo_b = lax.bitcast_convert_type(lo.astype(jnp.bfloat16).astype(jnp.float32), jnp.uint32)
    hi_b = lax.bitcast_convert_type(hi.astype(jnp.bfloat16).astype(jnp.float32), jnp.uint32)
    return (lo_b >> 16) | (hi_b & jnp.uint32(0xFFFF0000))


def _unpack_bf16_pairs(u):
    lo = lax.bitcast_convert_type(u << 16, jnp.float32)
    hi = lax.bitcast_convert_type(u & jnp.uint32(0xFFFF0000), jnp.float32)
    return lo, hi


def _mix_rows(x, nrows, with_output, g1_ref, w_in_ref, conv_w_ref, conv_b_ref, w_gate_ref,
              b_gate_ref, lam8_ref, lru_gain_ref, w_pool_ref, pool_scale_ref, w_out_ref,
              ux_buf, up_buf, a_buf, b_buf, hs_buf, hstate):
    hn = _rms(x, g1_ref[...])
    proj = _bdot(hn, w_in_ref[...])
    ux = proj[:, :LRU_WIDTH]
    up = proj[:, 2 * LRU_WIDTH:]

    ux_buf[CONV_HALO:CONV_HALO + nrows, :] = ux
    xc = conv_b_ref[...] + conv_w_ref[3:4, :] * ux
    for k in range(CONV_WIDTH - 1):
        xc = xc + conv_w_ref[k:k + 1, :] * ux_buf[k * BATCH:k * BATCH + nrows, :]
    ux_buf[0:CONV_HALO, :] = ux_buf[nrows:nrows + CONV_HALO, :]

    half = LRU_WIDTH // 2
    z = [_bdot(xc[:, j * half:(j + 1) * half], w_gate_ref[j]) for j in range(2)]
    za = jnp.concatenate([z[0][:, :half], z[1][:, :half]], axis=-1) + b_gate_ref[0:1, :]
    zx = jnp.concatenate([z[0][:, half:], z[1][:, half:]], axis=-1) + b_gate_ref[1:2, :]
    r = jax.nn.sigmoid(za)
    gi = jax.nn.sigmoid(zx)
    log_a = lam8_ref[...] * r
    a = jnp.exp(log_a)
    a_buf[0:nrows, :] = a
    b_buf[0:nrows, :] = jnp.sqrt(1.0 - a * a) * (gi * xc)

    def step(t, h):
        r0 = pl.multiple_of(t * SUBLANES, SUBLANES)
        h = a_buf[pl.ds(r0, SUBLANES), :] * h + b_buf[pl.ds(r0, SUBLANES), :]
        hs_buf[pl.ds(r0, SUBLANES), :] = h
        return h

    hstate[...] = lax.fori_loop(0, nrows // SUBLANES, step, hstate[...], unroll=8)

    up_buf[POOL_HALO:POOL_HALO + nrows, :] = up
    if not with_output:
        up_buf[0:POOL_HALO, :] = up_buf[nrows:nrows + POOL_HALO, :]
        return None

    ug = proj[:, LRU_WIDTH:2 * LRU_WIDTH]
    y_lru = _rms(hs_buf[0:nrows, :] * jax.nn.gelu(ug), lru_gain_ref[...])

    pm = []
    for g, w in enumerate(POOL_WINDOWS):
        lo, hi = g * POOL_GROUP, (g + 1) * POOL_GROUP
        cur = up[:, lo:hi]
        s = cur
        for j in range(1, w):
            off = POOL_HALO - j * BATCH
            s = s + up_buf[off:off + nrows, lo:hi]
        pm.append(s * (1.0 / w) - cur)
    up_buf[0:POOL_HALO, :] = up_buf[nrows:nrows + POOL_HALO, :]
    yp = [_bdot(jnp.concatenate(pm[2 * j:2 * j + 2], axis=-1), w_pool_ref[j]) for j in range(2)]
    y_pool = _rms(jnp.concatenate(yp, axis=-1), pool_scale_ref[...])

    y = _bdot(jnp.concatenate([y_lru, y_pool], axis=-1), w_out_ref[...])
    return x + y


def _mixer_kernel(x_hbm, meta_ref, g1_ref, w_in_ref, conv_w_ref, conv_b_ref, w_gate_ref,
                  b_gate_ref, lam8_ref, lru_gain_ref, w_pool_ref, pool_scale_ref, w_out_ref,
                  g2_ref, w_rt_ref, b_rt_ref, tri_ref,
                  h1_ref, xs_hbm, cnt_ref,
                  ux_buf, up_buf, a_buf, b_buf, hs_buf, hstate, running,
                  xin, rowbuf, dest_v, dest_s, in_sem, row_sem, idx_sem):
    i = pl.program_id(0)
    slot = i % 2
    n_rows = TOP_K * MIX_ROWS
    mix = functools.partial(
        _mix_rows, g1_ref=g1_ref, w_in_ref=w_in_ref, conv_w_ref=conv_w_ref,
        conv_b_ref=conv_b_ref, w_gate_ref=w_gate_ref, b_gate_ref=b_gate_ref, lam8_ref=lam8_ref,
        lru_gain_ref=lru_gain_ref, w_pool_ref=w_pool_ref, pool_scale_ref=pool_scale_ref,
        w_out_ref=w_out_ref, ux_buf=ux_buf, up_buf=up_buf, a_buf=a_buf, b_buf=b_buf,
        hs_buf=hs_buf, hstate=hstate)
    idx_copy = pltpu.make_async_copy(dest_v, dest_s, idx_sem)
    round_copy = pltpu.make_async_copy(rowbuf, xs_hbm.at[pl.ds(0, n_rows * ROW_SUB)], row_sem)

    def x_copy(step, s, b):
        return pltpu.make_async_copy(x_hbm.at[b, pl.ds(step * MIX_T, MIX_T), :],
                                     xin.at[s, :, b, :], in_sem.at[s])

    @pl.when(i == 0)
    def _():
        for b in range(BATCH):
            x_copy(0, 0, b).start()
        ux_buf[0:CONV_HALO, :] = jnp.zeros((CONV_HALO, LRU_WIDTH), jnp.float32)
        up_buf[0:POOL_HALO, :] = jnp.zeros((POOL_HALO, POOL_WIDTH), jnp.float32)
        hstate[...] = jnp.zeros_like(hstate)
        running[...] = jnp.zeros_like(running)
        rowbuf[...] = jnp.zeros_like(rowbuf)
        mix(meta_ref[...], META_ROWS, False)

    @pl.when(i + 1 < MIX_STEPS)
    def _():
        for b in range(BATCH):
            x_copy(i + 1, 1 - slot, b).start()

    for b in range(BATCH):
        x_copy(i, slot, b).wait()
    x = xin[slot].reshape(MIX_ROWS, D_MODEL)
    h1 = mix(x, MIX_ROWS, True)
    h1_ref[...] = h1

    hn2 = _rms(h1, g2_ref[...])
    logits = _bdot(hn2, w_rt_ref[...]) + b_rt_ref[...]
    lane = lax.broadcasted_iota(jnp.int32, (MIX_ROWS, LANES), 1)
    lane_f = lane.astype(jnp.float32)
    neg = jnp.float32(-jnp.inf)
    big = jnp.float32(4 * LANES)

    is_g = (lane >= N_EXPERTS) & (lane < N_EXPERTS + N_GROUPS)
    gl = jnp.where(is_g, logits, neg)
    gmax = jnp.max(gl, axis=-1, keepdims=True)
    gidx = jnp.min(jnp.where(gl == gmax, lane_f, big), axis=-1, keepdims=True)
    p_g = 1.0 / jnp.sum(jnp.exp(gl - gmax), axis=-1, keepdims=True)
    grp = gidx.astype(jnp.int32) - N_EXPERTS

    el = jnp.where((lane >> 3) == grp, logits, neg)
    m1 = jnp.max(el, axis=-1, keepdims=True)
    i1 = jnp.min(jnp.where(el == m1, lane_f, big), axis=-1, keepdims=True)
    el2 = jnp.where(lane_f == i1, neg, el)
    m2 = jnp.max(el2, axis=-1, keepdims=True)
    i2 = jnp.min(jnp.where(el2 == m2, lane_f, big), axis=-1, keepdims=True)
    e21 = jnp.exp(m2 - m1)
    den = 1.0 + e21
    gate = (p_g * (1.0 / den), p_g * (e21 / den))

    sel = (lane_f == i1, lane_f == i2)
    onehot = jnp.where(sel[0] | sel[1], 1.0, 0.0)
    before = jnp.dot(tri_ref[...], onehot.astype(jnp.bfloat16),
                     preferred_element_type=jnp.float32) + running[0:1, :]
    rank = [jnp.sum(jnp.where(sel[k], before, 0.0), axis=-1, keepdims=True) for k in range(TOP_K)]
    running[0:1, :] = running[0:1, :] + jnp.sum(onehot, axis=0, keepdims=True)
    cnt_ref[...] = running[...]

    eid = (i1.astype(jnp.int32), i2.astype(jnp.int32))
    dst = [(eid[k] * EXPERT_CAP + rank[k].astype(jnp.int32)) * ROW_SUB for k in range(TOP_K)]
    dst_tile = jnp.where(lane == 0, dst[0], jnp.where(lane == 1, dst[1], 0))
    dest_v[...] = jnp.transpose(dst_tile)[0:SUBLANES, :]
    idx_copy.start()

    data = _pack_bf16_pairs(hn2[:, :HALF], hn2[:, HALF:])
    tok = i * MIX_ROWS + lax.broadcasted_iota(jnp.int32, (MIX_ROWS, LANES), 0)

    @pl.when(i > 0)
    def _():
        round_copy.wait()

    for k in range(TOP_K):
        base = k * MIX_ROWS * ROW_SUB
        for c in range(DATA_SUB):
            rowbuf[pl.ds(base + c, MIX_ROWS, stride=ROW_SUB), :] = data[:, c * LANES:(c + 1) * LANES]
        ret = (tok * TOP_K + k) * DATA_SUB
        gbits = lax.bitcast_convert_type(jnp.broadcast_to(gate[k], (MIX_ROWS, LANES)), jnp.int32)
        meta = jnp.where(lane == 0, ret, jnp.where(lane == 1, gbits, 0))
        rowbuf[pl.ds(base + DATA_SUB, MIX_ROWS, stride=ROW_SUB), :] = lax.bitcast_convert_type(
            meta, jnp.uint32)

    idx_copy.wait()
    for k in range(TOP_K):
        for r in range(MIX_ROWS):
            d = pl.multiple_of(dest_s[k, r], ROW_SUB)
            pltpu.make_async_copy(rowbuf.at[pl.ds((k * MIX_ROWS + r) * ROW_SUB, ROW_SUB)],
                                  xs_hbm.at[pl.ds(d, ROW_SUB)], row_sem).start(priority=r % 2)

    @pl.when(i == MIX_STEPS - 1)
    def _():
        round_copy.wait()


def _mixer(x, meta_t, g1, w_in, conv_w, conv_b, w_gate, b_gate, lam8, lru_gain, w_pool,
           pool_scale, w_out, g2, w_rt, b_rt, tri):
    full = lambda a: pl.BlockSpec(a.shape, lambda i: (0,) * a.ndim)
    consts = (meta_t, g1, w_in, conv_w, conv_b, w_gate, b_gate, lam8, lru_gain, w_pool,
              pool_scale, w_out, g2, w_rt, b_rt, tri)
    return pl.pallas_call(
        _mixer_kernel,
        grid=(MIX_STEPS,),
        in_specs=[pl.BlockSpec(memory_space=pl.ANY)] + [full(a) for a in consts],
        out_specs=[pl.BlockSpec((MIX_ROWS, D_MODEL), lambda i: (i, 0)),
                   pl.BlockSpec(memory_space=pl.ANY),
                   pl.BlockSpec((SUBLANES, LANES), lambda i: (0, 0))],
        out_shape=[jax.ShapeDtypeStruct((N_TOK, D_MODEL), jnp.float32),
                   jax.ShapeDtypeStruct((XS_ROWS * ROW_SUB, LANES), jnp.uint32),
                   jax.ShapeDtypeStruct((SUBLANES, LANES), jnp.float32)],
        scratch_shapes=[pltpu.VMEM((CONV_HALO + MIX_ROWS, LRU_WIDTH), jnp.float32),
                        pltpu.VMEM((POOL_HALO + MIX_ROWS, POOL_WIDTH), jnp.float32),
                        pltpu.VMEM((MIX_ROWS, LRU_WIDTH), jnp.float32),
                        pltpu.VMEM((MIX_ROWS, LRU_WIDTH), jnp.float32),
                        pltpu.VMEM((MIX_ROWS, LRU_WIDTH), jnp.float32),
                        pltpu.VMEM((SUBLANES, LRU_WIDTH), jnp.float32),
                        pltpu.VMEM((SUBLANES, LANES), jnp.float32),
                        pltpu.VMEM((2, MIX_T, BATCH, D_MODEL), jnp.float32),
                        pltpu.VMEM((TOP_K * MIX_ROWS * ROW_SUB, LANES), jnp.uint32),
                        pltpu.VMEM((SUBLANES, MIX_ROWS), jnp.int32),
                        pltpu.SMEM((SUBLANES, MIX_ROWS), jnp.int32),
                        pltpu.SemaphoreType.DMA((2,)),
                        pltpu.SemaphoreType.DMA,
                        pltpu.SemaphoreType.DMA],
        compiler_params=pltpu.CompilerParams(dimension_semantics=("arbitrary",),
                                             vmem_limit_bytes=VMEM_LIMIT),
        name="mixer",
    )(x, *consts)


def _experts_kernel(blk_ref, bexp_ref, nval_ref, ntot_ref, x_ref, wg_ref, wu_ref, wd_ref, y_hbm,
                    wgu_b, wd_b, obuf, addr_v, addr_s, row_sem, idx_sem):
    i = pl.program_id(0)
    slot = i % 2
    n_used = ntot_ref[0]
    idx_copy = pltpu.make_async_copy(addr_v, addr_s, idx_sem)

    def round_copy(s):
        return pltpu.make_async_copy(obuf.at[s], y_hbm.at[pl.ds(0, MOE_BM * DATA_SUB)], row_sem.at[s])

    @pl.when((i < n_used) & ((i == 0) | (bexp_ref[i] != bexp_ref[jnp.maximum(i - 1, 0)])))
    def _():
        wgu_b[:, 0:EXPERT_FF] = wg_ref[0].astype(jnp.bfloat16)
        wgu_b[:, EXPERT_FF:] = wu_ref[0].astype(jnp.bfloat16)
        wd_b[...] = wd_ref[0].astype(jnp.bfloat16)

    @pl.when(i < n_used)
    def _():
        meta = lax.bitcast_convert_type(x_ref[pl.ds(DATA_SUB, MOE_BM, stride=ROW_SUB), :], jnp.int32)
        row = lax.broadcasted_iota(jnp.int32, (MOE_BM, LANES), 0)
        lane = lax.broadcasted_iota(jnp.int32, (MOE_BM, LANES), 1)
        ret = jnp.where(row < nval_ref[i], meta, (Y_DUMP + slot * MOE_BM + row) * DATA_SUB)
        addr_v[...] = jnp.transpose(jnp.where(lane == 0, ret, 0))[0:SUBLANES, :]
        idx_copy.start()

        parts = [_unpack_bf16_pairs(x_ref[pl.ds(c, MOE_BM, stride=ROW_SUB), :]) for c in range(DATA_SUB)]
        x = jnp.concatenate([p[0] for p in parts] + [p[1] for p in parts], axis=-1)
        gu = _bdot(x, wgu_b[...])
        g = gu[:, :EXPERT_FF]
        hid = (g * jax.nn.sigmoid(g)) * gu[:, EXPERT_FF:]
        gate = lax.bitcast_convert_type(meta[:, 1:2], jnp.float32)
        o = _bdot(hid, wd_b[...]) * gate
        packed = _pack_bf16_pairs(o[:, :HALF], o[:, HALF:])

        @pl.when(i >= 2)
        def _():
            round_copy(slot).wait()

        for c in range(DATA_SUB):
            obuf[slot, pl.ds(c, MOE_BM, stride=DATA_SUB), :] = packed[:, c * LANES:(c + 1) * LANES]

        idx_copy.wait()
        for r in range(MOE_BM):
            d = pl.multiple_of(addr_s[0, r], DATA_SUB)
            pltpu.make_async_copy(obuf.at[slot, pl.ds(r * DATA_SUB, DATA_SUB)],
                                  y_hbm.at[pl.ds(d, DATA_SUB)], row_sem.at[slot]).start(priority=r % 2)

    @pl.when(i == pl.num_programs(0) - 1)
    def _():
        @pl.when(n_used >= 2)
        def _():
            round_copy(n_used % 2).wait()
        round_copy((n_used - 1) % 2).wait()


def _experts(blk_idx, blk_expert, blk_nvalid, n_used, xs, w_gate, w_up, w_down):
    wmap = lambda i, blk, bexp, nval, ntot: (bexp[i], 0, 0)
    return pl.pallas_call(
        _experts_kernel,
        grid_spec=pltpu.PrefetchScalarGridSpec(
            num_scalar_prefetch=4,
            grid=(N_BLOCKS,),
            in_specs=[pl.BlockSpec((MOE_BM * ROW_SUB, LANES),
                                   lambda i, blk, bexp, nval, ntot: (blk[i], 0)),
                      pl.BlockSpec((1, D_MODEL, EXPERT_FF), wmap),
                      pl.BlockSpec((1, D_MODEL, EXPERT_FF), wmap),
                      pl.BlockSpec((1, EXPERT_FF, D_MODEL), wmap)],
            out_specs=pl.BlockSpec(memory_space=pl.ANY),
            scratch_shapes=[pltpu.VMEM((D_MODEL, 2 * EXPERT_FF), jnp.bfloat16),
                            pltpu.VMEM((EXPERT_FF, D_MODEL), jnp.bfloat16),
                            pltpu.VMEM((2, MOE_BM * DATA_SUB, LANES), jnp.uint32),
                            pltpu.VMEM((SUBLANES, MOE_BM), jnp.int32),
                            pltpu.SMEM((SUBLANES, MOE_BM), jnp.int32),
                            pltpu.SemaphoreType.DMA((2,)),
                            pltpu.SemaphoreType.DMA]),
        out_shape=jax.ShapeDtypeStruct((Y_ROWS * DATA_SUB, LANES), jnp.uint32),
        compiler_params=pltpu.CompilerParams(dimension_semantics=("arbitrary",),
                                             vmem_limit_bytes=VMEM_LIMIT),
        name="experts",
    )(blk_idx, blk_expert, blk_nvalid, n_used, xs, w_gate, w_up, w_down)


def _final_kernel(h1_ref, y_ref, gain_ref, out_hbm, obuf, sem):
    i = pl.program_id(0)
    slot = i % 2
    stride = TOP_K * DATA_SUB

    def out_copy(step, s, b):
        return pltpu.make_async_copy(obuf.at[s, :, b, :],
                                     out_hbm.at[b, pl.ds(step * MIX_T, MIX_T), :], sem.at[s])

    lo, hi = [], []
    for c in range(DATA_SUB):
        y0 = _unpack_bf16_pairs(y_ref[pl.ds(c, FIN_TM, stride=stride), :])
        y1 = _unpack_bf16_pairs(y_ref[pl.ds(DATA_SUB + c, FIN_TM, stride=stride), :])
        lo.append(y0[0] + y1[0])
        hi.append(y0[1] + y1[1])
    y = jnp.concatenate(lo + hi, axis=-1)
    res = _rms(h1_ref[...] + y, gain_ref[...])

    @pl.when(i >= 2)
    def _():
        for b in range(BATCH):
            out_copy(i - 2, slot, b).wait()

    obuf[slot] = res.reshape(MIX_T, BATCH, D_MODEL)
    for b in range(BATCH):
        out_copy(i, slot, b).start()

    @pl.when(i == pl.num_programs(0) - 1)
    def _():
        for b in range(BATCH):
            out_copy(i - 1, 1 - slot, b).wait()
        for b in range(BATCH):
            out_copy(i, slot, b).wait()


def _final(h1, y2, final_gain):
    rows = FIN_TM * TOP_K * DATA_SUB
    return pl.pallas_call(
        _final_kernel,
        grid=(N_TOK // FIN_TM,),
        in_specs=[pl.BlockSpec((FIN_TM, D_MODEL), lambda i: (i, 0)),
                  pl.BlockSpec((rows, LANES), lambda i: (i, 0)),
                  pl.BlockSpec((1, D_MODEL), lambda i: (0, 0))],
        out_specs=pl.BlockSpec(memory_space=pl.ANY),
        out_shape=jax.ShapeDtypeStruct((BATCH, SEQ, D_MODEL), jnp.float32),
        scratch_shapes=[pltpu.VMEM((2, MIX_T, BATCH, D_MODEL), jnp.float32),
                        pltpu.SemaphoreType.DMA((2,))],
        compiler_params=pltpu.CompilerParams(dimension_semantics=("arbitrary",),
                                             vmem_limit_bytes=VMEM_LIMIT),
        name="final",
    )(h1, y2, final_gain)


def _block_diag(blocks):
    n = len(blocks)
    r, c = blocks[0].shape
    out = jnp.zeros((n * r, n * c), blocks[0].dtype)
    for k, blk in enumerate(blocks):
        out = out.at[k * r:(k + 1) * r, k * c:(k + 1) * c].set(blk)
    return out


def kernel(x, meta_tokens, norm1_gain, w_in, conv_w, conv_b, lru_wa, lru_ba, lru_wx, lru_bx,
           lru_lambda, lru_out_gain, pool_w, pool_scale, w_out, norm2_gain, w_group, b_group,
           w_router, b_router, w_gate, w_up, w_down, final_gain):
    assert x.shape == (BATCH, SEQ, D_MODEL) and norm1_gain.shape[0] == 1
    f32, bf16 = jnp.float32, jnp.bfloat16
    row = lambda v: v.reshape(1, -1).astype(f32)

    meta_t = jnp.repeat(meta_tokens.astype(f32), BATCH, axis=0)

    heads_per = LRU_HEADS // 2
    w_gate_blk = jnp.stack([
        jnp.concatenate([_block_diag([lru_wa[0, h] for h in range(j * heads_per, (j + 1) * heads_per)]),
                         _block_diag([lru_wx[0, h] for h in range(j * heads_per, (j + 1) * heads_per)])],
                        axis=1) for j in range(2)]).astype(bf16)
    b_gate_blk = jnp.stack([lru_ba[0], lru_bx[0]]).astype(f32)
    w_pool_blk = jnp.stack([_block_diag([pool_w[0, 2 * j], pool_w[0, 2 * j + 1]])
                            for j in range(2)]).astype(bf16)
    lam8 = row(LRU_C * jax.nn.log_sigmoid(lru_lambda[0].astype(f32)))
    pad = LANES - N_EXPERTS - N_GROUPS
    w_rt = jnp.concatenate([w_router[0], w_group[0], jnp.zeros((D_MODEL, pad), f32)], axis=1).astype(bf16)
    b_rt = jnp.concatenate([b_router[0], b_group[0], jnp.zeros((pad,), f32)]).reshape(1, LANES).astype(f32)
    ridx = jnp.arange(MIX_ROWS)
    tri = (ridx[None, :] < ridx[:, None]).astype(bf16)

    h1, xs, cnt = _mixer(
        x, meta_t, row(norm1_gain[0]), w_in[0].astype(bf16), conv_w[0].astype(f32),
        row(conv_b[0]), w_gate_blk, b_gate_blk, lam8, row(lru_out_gain[0]), w_pool_blk,
        row(pool_scale[0]), w_out[0].astype(bf16), row(norm2_gain[0]), w_rt, b_rt, tri)

    counts = cnt[0, :N_EXPERTS].astype(jnp.int32)
    nblk = (counts + MOE_BM - 1) // MOE_BM
    blk_end = jnp.cumsum(nblk)
    step = jnp.arange(N_BLOCKS, dtype=jnp.int32)
    used = step < blk_end[-1]
    step_c = jnp.minimum(step, blk_end[-1] - 1)
    e_of = jnp.sum(blk_end[None, :] <= step_c[:, None], axis=-1).astype(jnp.int32)
    onehot_e = e_of[:, None] == jnp.arange(N_EXPERTS, dtype=jnp.int32)[None, :]
    j_of = step_c - jnp.sum(jnp.where(onehot_e, (blk_end - nblk)[None, :], 0), axis=-1)
    cnt_of = jnp.sum(jnp.where(onehot_e, counts[None, :], 0), axis=-1)
    blk_idx = (e_of * CAP_BLOCKS + j_of).astype(jnp.int32)
    blk_nvalid = jnp.where(used, jnp.clip(cnt_of - j_of * MOE_BM, 0, MOE_BM), 0).astype(jnp.int32)

    y2 = _experts(blk_idx, e_of, blk_nvalid, blk_end[-1:].astype(jnp.int32), xs,
                  w_gate[0], w_up[0], w_down[0])
    return _final(h1, y2, row(final_gain))
```

```python
import functools

import jax
import jax.numpy as jnp
from jax import lax
from jax.experimental import pallas as pl
from jax.experimental.pallas import tpu as pltpu

D_MODEL = 1024
BATCH = 8
SEQ = 4096
N_META = 16
LRU_WIDTH = 512
LRU_HEADS = 8
LRU_HEAD_DIM = 64
CONV_WIDTH = 4
LRU_C = 8.0
POOL_WIDTH = 512
POOL_WINDOWS = (2, 4, 8, 16)
POOL_GROUP = 128
N_GROUPS = 4
EXPERTS_PER_GROUP = 8
N_EXPERTS = 32
TOP_K = 2
EXPERT_FF = 512
RMS_EPS = 1e-6
SQRT_FLOOR = 1e-30

N_TOK = BATCH * SEQ
SUBLANES = 8
LANES = 128
MIX_ROWS = 512
META_ROWS = N_META * BATCH
CONV_HALO = (CONV_WIDTH - 1) * BATCH
POOL_HALO = (max(POOL_WINDOWS) - 1) * BATCH
MIX_STEPS = N_TOK // MIX_ROWS
MIX_T = MIX_ROWS // BATCH
MOE_BM = 512
N_ASSIGN = N_TOK * TOP_K
N_BLOCKS = N_ASSIGN // MOE_BM + N_EXPERTS
EXPERT_CAP = N_TOK
CAP_BLOCKS = EXPERT_CAP // MOE_BM
ROW_SUB = 8
DATA_SUB = 4
HALF = D_MODEL // 2
XS_ROWS = N_EXPERTS * EXPERT_CAP
Y_DUMP = N_ASSIGN
Y_ROWS = Y_DUMP + 2 * MOE_BM
FIN_TM = 512
VMEM_LIMIT = 56 * 1024 * 1024


def _rms(x, gain):
    return x * lax.rsqrt(jnp.mean(x * x, axis=-1, keepdims=True) + RMS_EPS) * gain


def _bdot(a, b):
    return jnp.dot(a.astype(jnp.bfloat16), b, preferred_element_type=jnp.float32)


def _pack_bf16_pairs(lo, hi):
    lo_b = lax.bitcast_convert_type(lo.astype(jnp.bfloat16).astype(jnp.float32), jnp.uint32)
    hi_b = lax.bitcast_convert_type(hi.astype(jnp.bfloat16).astype(jnp.float32), jnp.uint32)
    return (lo_b >> 16) | (hi_b & jnp.uint32(0xFFFF0000))


def _unpack_bf16_pairs(u):
    lo = lax.bitcast_convert_type(u << 16, jnp.float32)
    hi = lax.bitcast_convert_type(u & jnp.uint32(0xFFFF0000), jnp.float32)
    return lo, hi


def _mix_rows(x, nrows, with_output, g1_ref, w_in_ref, conv_w_ref, conv_b_ref, w_gate_ref,
              b_gate_ref, lam8_ref, lru_gain_ref, w_pool_ref, pool_scale_ref, w_out_ref,
              ux_buf, up_buf, a_buf, b_buf, hs_buf, hstate):
    hn = _rms(x, g1_ref[...])
    proj = _bdot(hn, w_in_ref[...])
    ux = proj[:, :LRU_WIDTH]
    up = proj[:, 2 * LRU_WIDTH:]

    ux_buf[CONV_HALO:CONV_HALO + nrows, :] = ux
    xc = conv_b_ref[...] + conv_w_ref[3:4, :] * ux
    for k in range(CONV_WIDTH - 1):
        xc = xc + conv_w_ref[k:k + 1, :] * ux_buf[k * BATCH:k * BATCH + nrows, :]
    ux_buf[0:CONV_HALO, :] = ux_buf[nrows:nrows + CONV_HALO, :]

    half = LRU_WIDTH // 2
    z = [_bdot(xc[:, j * half:(j + 1) * half], w_gate_ref[j]) for j in range(2)]
    za = jnp.concatenate([z[0][:, :half], z[1][:, :half]], axis=-1) + b_gate_ref[0:1, :]
    zx = jnp.concatenate([z[0][:, half:], z[1][:, half:]], axis=-1) + b_gate_ref[1:2, :]
    r = jax.nn.sigmoid(za)
    gi = jax.nn.sigmoid(zx)
    log_a = lam8_ref[...] * r
    a = jnp.exp(log_a)
    a_buf[0:nrows, :] = a
    v = 1.0 - a * a
    root = v * lax.rsqrt(jnp.maximum(v, SQRT_FLOOR))
    b_buf[0:nrows, :] = root * (gi * xc)

    def step(t, h):
        r0 = pl.multiple_of(t * SUBLANES, SUBLANES)
        h = a_buf[pl.ds(r0, SUBLANES), :] * h + b_buf[pl.ds(r0, SUBLANES), :]
        hs_buf[pl.ds(r0, SUBLANES), :] = h
        return h

    hstate[...] = lax.fori_loop(0, nrows // SUBLANES, step, hstate[...], unroll=True)

    up_buf[POOL_HALO:POOL_HALO + nrows, :] = up
    if not with_output:
        up_buf[0:POOL_HALO, :] = up_buf[nrows:nrows + POOL_HALO, :]
        return None

    ug = proj[:, LRU_WIDTH:2 * LRU_WIDTH]
    y_lru = _rms(hs_buf[0:nrows, :] * jax.nn.gelu(ug), lru_gain_ref[...])

    pm = []
    for g, w in enumerate(POOL_WINDOWS):
        lo, hi = g * POOL_GROUP, (g + 1) * POOL_GROUP
        cur = up[:, lo:hi]
        s = cur
        for j in range(1, w):
            off = POOL_HALO - j * BATCH
            s = s + up_buf[off:off + nrows, lo:hi]
        pm.append(s * (1.0 / w) - cur)
    up_buf[0:POOL_HALO, :] = up_buf[nrows:nrows + POOL_HALO, :]
    yp = [_bdot(jnp.concatenate(pm[2 * j:2 * j + 2], axis=-1), w_pool_ref[j]) for j in range(2)]
    y_pool = _rms(jnp.concatenate(yp, axis=-1), pool_scale_ref[...])

    y = _bdot(jnp.concatenate([y_lru, y_pool], axis=-1), w_out_ref[...])
    return x + y


def _mixer_kernel(x_hbm, meta_ref, g1_ref, w_in_ref, conv_w_ref, conv_b_ref, w_gate_ref,
                  b_gate_ref, lam8_ref, lru_gain_ref, w_pool_ref, pool_scale_ref, w_out_ref,
                  g2_ref, w_rt_ref, b_rt_ref, tri_ref,
                  h1_ref, xs_hbm, cnt_ref,
                  ux_buf, up_buf, a_buf, b_buf, hs_buf, hstate, running,
                  xin, rowbuf, dest_v, dest_s, in_sem, row_sem, idx_sem):
    i = pl.program_id(0)
    slot = i % 2
    n_rows = TOP_K * MIX_ROWS
    mix = functools.partial(
        _mix_rows, g1_ref=g1_ref, w_in_ref=w_in_ref, conv_w_ref=conv_w_ref,
        conv_b_ref=conv_b_ref, w_gate_ref=w_gate_ref, b_gate_ref=b_gate_ref, lam8_ref=lam8_ref,
        lru_gain_ref=lru_gain_ref, w_pool_ref=w_pool_ref, pool_scale_ref=pool_scale_ref,
        w_out_ref=w_out_ref, ux_buf=ux_buf, up_buf=up_buf, a_buf=a_buf, b_buf=b_buf,
        hs_buf=hs_buf, hstate=hstate)
    idx_copy = pltpu.make_async_copy(dest_v, dest_s, idx_sem)
    round_copy = pltpu.make_async_copy(rowbuf, xs_hbm.at[pl.ds(0, n_rows * ROW_SUB)], row_sem)

    def x_copy(step, s, b):
        return pltpu.make_async_copy(x_hbm.at[b, pl.ds(step * MIX_T, MIX_T), :],
                                     xin.at[s, :, b, :], in_sem.at[s])

    @pl.when(i == 0)
    def _():
        for b in range(BATCH):
            x_copy(0, 0, b).start()
        ux_buf[0:CONV_HALO, :] = jnp.zeros((CONV_HALO, LRU_WIDTH), jnp.float32)
        up_buf[0:POOL_HALO, :] = jnp.zeros((POOL_HALO, POOL_WIDTH), jnp.float32)
        hstate[...] = jnp.zeros_like(hstate)
        running[...] = jnp.zeros_like(running)
        rowbuf[...] = jnp.zeros_like(rowbuf)
        mix(meta_ref[...], META_ROWS, False)

    @pl.when(i + 1 < MIX_STEPS)
    def _():
        for b in range(BATCH):
            x_copy(i + 1, 1 - slot, b).start()

    for b in range(BATCH):
        x_copy(i, slot, b).wait()
    x = xin[slot].reshape(MIX_ROWS, D_MODEL)
    h1 = mix(x, MIX_ROWS, True)
    h1_ref[...] = h1

    hn2 = _rms(h1, g2_ref[...])
    logits = _bdot(hn2, w_rt_ref[...]) + b_rt_ref[...]
    lane = lax.broadcasted_iota(jnp.int32, (MIX_ROWS, LANES), 1)
    lane_f = lane.astype(jnp.float32)
    neg = jnp.float32(-jnp.inf)
    big = jnp.float32(4 * LANES)

    is_g = (lane >= N_EXPERTS) & (lane < N_EXPERTS + N_GROUPS)
    gl = jnp.where(is_g, logits, neg)
    gmax = jnp.max(gl, axis=-1, keepdims=True)
    gidx = jnp.min(jnp.where(gl == gmax, lane_f, big), axis=-1, keepdims=True)
    p_g = 1.0 / jnp.sum(jnp.exp(gl - gmax), axis=-1, keepdims=True)
    grp = gidx.astype(jnp.int32) - N_EXPERTS

    el = jnp.where((lane >> 3) == grp, logits, neg)
    m1 = jnp.max(el, axis=-1, keepdims=True)
    i1 = jnp.min(jnp.where(el == m1, lane_f, big), axis=-1, keepdims=True)
    el2 = jnp.where(lane_f == i1, neg, el)
    m2 = jnp.max(el2, axis=-1, keepdims=True)
    i2 = jnp.min(jnp.where(el2 == m2, lane_f, big), axis=-1, keepdims=True)
    e21 = jnp.exp(m2 - m1)
    den = 1.0 + e21
    gate = (p_g * (1.0 / den), p_g * (e21 / den))

    sel = (lane_f == i1, lane_f == i2)
    onehot = jnp.where(sel[0] | sel[1], 1.0, 0.0)
    before = jnp.dot(tri_ref[...], onehot.astype(jnp.bfloat16),
                     preferred_element_type=jnp.float32) + running[0:1, :]
    rank = [jnp.sum(jnp.where(sel[k], before, 0.0), axis=-1, keepdims=True) for k in range(TOP_K)]
    running[0:1, :] = running[0:1, :] + jnp.sum(onehot, axis=0, keepdims=True)
    cnt_ref[...] = running[...]

    eid = (i1.astype(jnp.int32), i2.astype(jnp.int32))
    dst = [(eid[k] * EXPERT_CAP + rank[k].astype(jnp.int32)) * ROW_SUB for k in range(TOP_K)]
    dst_tile = jnp.where(lane == 0, dst[0], jnp.where(lane == 1, dst[1], 0))
    dest_v[...] = jnp.transpose(dst_tile)[0:SUBLANES, :]
    idx_copy.start()

    data = _pack_bf16_pairs(hn2[:, :HALF], hn2[:, HALF:])
    tok = i * MIX_ROWS + lax.broadcasted_iota(jnp.int32, (MIX_ROWS, LANES), 0)

    @pl.when(i > 0)
    def _():
        round_copy.wait()

    for k in range(TOP_K):
        base = k * MIX_ROWS * ROW_SUB
        for c in range(DATA_SUB):
            rowbuf[pl.ds(base + c, MIX_ROWS, stride=ROW_SUB), :] = data[:, c * LANES:(c + 1) * LANES]
        ret = (tok * TOP_K + k) * DATA_SUB
        gbits = lax.bitcast_convert_type(jnp.broadcast_to(gate[k], (MIX_ROWS, LANES)), jnp.int32)
        meta = jnp.where(lane == 0, ret, jnp.where(lane == 1, gbits, 0))
        rowbuf[pl.ds(base + DATA_SUB, MIX_ROWS, stride=ROW_SUB), :] = lax.bitcast_convert_type(
            meta, jnp.uint32)

    idx_copy.wait()
    for k in range(TOP_K):
        for r in range(MIX_ROWS):
            d = pl.multiple_of(dest_s[k, r], ROW_SUB)
            pltpu.make_async_copy(rowbuf.at[pl.ds((k * MIX_ROWS + r) * ROW_SUB, ROW_SUB)],
                                  xs_hbm.at[pl.ds(d, ROW_SUB)], row_sem).start(priority=r % 2)

    @pl.when(i == MIX_STEPS - 1)
    def _():
        round_copy.wait()


def _mixer(x, meta_t, g1, w_in, conv_w, conv_b, w_gate, b_gate, lam8, lru_gain, w_pool,
           pool_scale, w_out, g2, w_rt, b_rt, tri):
    full = lambda a: pl.BlockSpec(a.shape, lambda i: (0,) * a.ndim)
    consts = (meta_t, g1, w_in, conv_w, conv_b, w_gate, b_gate, lam8, lru_gain, w_pool,
              pool_scale, w_out, g2, w_rt, b_rt, tri)
    return pl.pallas_call(
        _mixer_kernel,
        grid=(MIX_STEPS,),
        in_specs=[pl.BlockSpec(memory_space=pl.ANY)] + [full(a) for a in consts],
        out_specs=[pl.BlockSpec((MIX_ROWS, D_MODEL), lambda i: (i, 0)),
                   pl.BlockSpec(memory_space=pl.ANY),
                   pl.BlockSpec((SUBLANES, LANES), lambda i: (0, 0))],
        out_shape=[jax.ShapeDtypeStruct((N_TOK, D_MODEL), jnp.float32),
                   jax.ShapeDtypeStruct((XS_ROWS * ROW_SUB, LANES), jnp.uint32),
                   jax.ShapeDtypeStruct((SUBLANES, LANES), jnp.float32)],
        scratch_shapes=[pltpu.VMEM((CONV_HALO + MIX_ROWS, LRU_WIDTH), jnp.float32),
                        pltpu.VMEM((POOL_HALO + MIX_ROWS, POOL_WIDTH), jnp.float32),
                        pltpu.VMEM((MIX_ROWS, LRU_WIDTH), jnp.float32),
                        pltpu.VMEM((MIX_ROWS, LRU_WIDTH), jnp.float32),
                        pltpu.VMEM((MIX_ROWS, LRU_WIDTH), jnp.float32),
                        pltpu.VMEM((SUBLANES, LRU_WIDTH), jnp.float32),
                        pltpu.VMEM((SUBLANES, LANES), jnp.float32),
                        pltpu.VMEM((2, MIX_T, BATCH, D_MODEL), jnp.float32),
                        pltpu.VMEM((TOP_K * MIX_ROWS * ROW_SUB, LANES), jnp.uint32),
                        pltpu.VMEM((SUBLANES, MIX_ROWS), jnp.int32),
                        pltpu.SMEM((SUBLANES, MIX_ROWS), jnp.int32),
                        pltpu.SemaphoreType.DMA((2,)),
                        pltpu.SemaphoreType.DMA,
                        pltpu.SemaphoreType.DMA],
        compiler_params=pltpu.CompilerParams(dimension_semantics=("arbitrary",),
                                             vmem_limit_bytes=VMEM_LIMIT),
        name="mixer",
    )(x, *consts)


def _experts_kernel(blk_ref, bexp_ref, nval_ref, ntot_ref, x_ref, wg_ref, wu_ref, wd_ref, y_hbm,
                    wgu_b, wd_b, obuf, addr_v, addr_s, row_sem, idx_sem):
    i = pl.program_id(0)
    slot = i % 2
    n_used = ntot_ref[0]
    idx_copy = pltpu.make_async_copy(addr_v, addr_s, idx_sem)

    def round_copy(s):
        return pltpu.make_async_copy(obuf.at[s], y_hbm.at[pl.ds(0, MOE_BM * DATA_SUB)], row_sem.at[s])

    @pl.when((i < n_used) & ((i == 0) | (bexp_ref[i] != bexp_ref[jnp.maximum(i - 1, 0)])))
    def _():
        wgu_b[:, 0:EXPERT_FF] = wg_ref[0].astype(jnp.bfloat16)
        wgu_b[:, EXPERT_FF:] = wu_ref[0].astype(jnp.bfloat16)
        wd_b[...] = wd_ref[0].astype(jnp.bfloat16)

    @pl.when(i < n_used)
    def _():
        meta = lax.bitcast_convert_type(x_ref[pl.ds(DATA_SUB, MOE_BM, stride=ROW_SUB), :], jnp.int32)
        row = lax.broadcasted_iota(jnp.int32, (MOE_BM, LANES), 0)
        lane = lax.broadcasted_iota(jnp.int32, (MOE_BM, LANES), 1)
        ret = jnp.where(row < nval_ref[i], meta, (Y_DUMP + slot * MOE_BM + row) * DATA_SUB)
        addr_v[...] = jnp.transpose(jnp.where(lane == 0, ret, 0))[0:SUBLANES, :]
        idx_copy.start()

        parts = [_unpack_bf16_pairs(x_ref[pl.ds(c, MOE_BM, stride=ROW_SUB), :]) for c in range(DATA_SUB)]
        x = jnp.concatenate([p[0] for p in parts] + [p[1] for p in parts], axis=-1)
        gu = _bdot(x, wgu_b[...])
        g = gu[:, :EXPERT_FF]
        hid = (g * jax.nn.sigmoid(g)) * gu[:, EXPERT_FF:]
        gate = lax.bitcast_convert_type(meta[:, 1:2], jnp.float32)
        o = _bdot(hid, wd_b[...]) * gate
        packed = _pack_bf16_pairs(o[:, :HALF], o[:, HALF:])

        @pl.when(i >= 2)
        def _():
            round_copy(slot).wait()

        for c in range(DATA_SUB):
            obuf[slot, pl.ds(c, MOE_BM, stride=DATA_SUB), :] = packed[:, c * LANES:(c + 1) * LANES]

        idx_copy.wait()
        for r in range(MOE_BM):
            d = pl.multiple_of(addr_s[0, r], DATA_SUB)
            pltpu.make_async_copy(obuf.at[slot, pl.ds(r * DATA_SUB, DATA_SUB)],
                                  y_hbm.at[pl.ds(d, DATA_SUB)], row_sem.at[slot]).start(priority=r % 2)

    @pl.when(i == pl.num_programs(0) - 1)
    def _():
        @pl.when(n_used >= 2)
        def _():
            round_copy(n_used % 2).wait()
        round_copy((n_used - 1) % 2).wait()


def _experts(blk_idx, blk_expert, blk_nvalid, n_used, xs, w_gate, w_up, w_down):
    wmap = lambda i, blk, bexp, nval, ntot: (bexp[i], 0, 0)
    return pl.pallas_call(
        _experts_kernel,
        grid_spec=pltpu.PrefetchScalarGridSpec(
            num_scalar_prefetch=4,
            grid=(N_BLOCKS,),
            in_specs=[pl.BlockSpec((MOE_BM * ROW_SUB, LANES),
                                   lambda i, blk, bexp, nval, ntot: (blk[i], 0)),
                      pl.BlockSpec((1, D_MODEL, EXPERT_FF), wmap),
                      pl.BlockSpec((1, D_MODEL, EXPERT_FF), wmap),
                      pl.BlockSpec((1, EXPERT_FF, D_MODEL), wmap)],
            out_specs=pl.BlockSpec(memory_space=pl.ANY),
            scratch_shapes=[pltpu.VMEM((D_MODEL, 2 * EXPERT_FF), jnp.bfloat16),
                            pltpu.VMEM((EXPERT_FF, D_MODEL), jnp.bfloat16),
                            pltpu.VMEM((2, MOE_BM * DATA_SUB, LANES), jnp.uint32),
                            pltpu.VMEM((SUBLANES, MOE_BM), jnp.int32),
                            pltpu.SMEM((SUBLANES, MOE_BM), jnp.int32),
                            pltpu.SemaphoreType.DMA((2,)),
                            pltpu.SemaphoreType.DMA]),
        out_shape=jax.ShapeDtypeStruct((Y_ROWS * DATA_SUB, LANES), jnp.uint32),
        compiler_params=pltpu.CompilerParams(dimension_semantics=("arbitrary",),
                                             vmem_limit_bytes=VMEM_LIMIT),
        name="experts",
    )(blk_idx, blk_expert, blk_nvalid, n_used, xs, w_gate, w_up, w_down)


def _final_kernel(h1_ref, y_ref, gain_ref, out_hbm, obuf, sem):
    i = pl.program_id(0)
    slot = i % 2
    stride = TOP_K * DATA_SUB

    def out_copy(step, s, b):
        return pltpu.make_async_copy(obuf.at[s, :, b, :],
                                     out_hbm.at[b, pl.ds(step * MIX_T, MIX_T), :], sem.at[s])

    lo, hi = [], []
    for c in range(DATA_SUB):
        y0 = _unpack_bf16_pairs(y_ref[pl.ds(c, FIN_TM, stride=stride), :])
        y1 = _unpack_bf16_pairs(y_ref[pl.ds(DATA_SUB + c, FIN_TM, stride=stride), :])
        lo.append(y0[0] + y1[0])
        hi.append(y0[1] + y1[1])
    y = jnp.concatenate(lo + hi, axis=-1)
    res = _rms(h1_ref[...] + y, gain_ref[...])

    @pl.when(i >= 2)
    def _():
        for b in range(BATCH):
            out_copy(i - 2, slot, b).wait()

    obuf[slot] = res.reshape(MIX_T, BATCH, D_MODEL)
    for b in range(BATCH):
        out_copy(i, slot, b).start()

    @pl.when(i == pl.num_programs(0) - 1)
    def _():
        for b in range(BATCH):
            out_copy(i - 1, 1 - slot, b).wait()
        for b in range(BATCH):
            out_copy(i, slot, b).wait()


def _final(h1, y2, final_gain):
    rows = FIN_TM * TOP_K * DATA_SUB
    return pl.pallas_call(
        _final_kernel,
        grid=(N_TOK // FIN_TM,),
        in_specs=[pl.BlockSpec((FIN_TM, D_MODEL), lambda i: (i, 0)),
                  pl.BlockSpec((rows, LANES), lambda i: (i, 0)),
                  pl.BlockSpec((1, D_MODEL), lambda i: (0, 0))],
        out_specs=pl.BlockSpec(memory_space=pl.ANY),
        out_shape=jax.ShapeDtypeStruct((BATCH, SEQ, D_MODEL), jnp.float32),
        scratch_shapes=[pltpu.VMEM((2, MIX_T, BATCH, D_MODEL), jnp.float32),
                        pltpu.SemaphoreType.DMA((2,))],
        compiler_params=pltpu.CompilerParams(dimension_semantics=("arbitrary",),
                                             vmem_limit_bytes=VMEM_LIMIT),
        name="final",
    )(h1, y2, final_gain)


def _block_diag(blocks):
    n = len(blocks)
    r, c = blocks[0].shape
    out = jnp.zeros((n * r, n * c), blocks[0].dtype)
    for k, blk in enumerate(blocks):
        out = out.at[k * r:(k + 1) * r, k * c:(k + 1) * c].set(blk)
    return out


def kernel(x, meta_tokens, norm1_gain, w_in, conv_w, conv_b, lru_wa, lru_ba, lru_wx, lru_bx,
           lru_lambda, lru_out_gain, pool_w, pool_scale, w_out, norm2_gain, w_group, b_group,
           w_router, b_router, w_gate, w_up, w_down, final_gain):
    assert x.shape == (BATCH, SEQ, D_MODEL) and norm1_gain.shape[0] == 1
    f32, bf16 = jnp.float32, jnp.bfloat16
    row = lambda v: v.reshape(1, -1).astype(f32)

    meta_t = jnp.repeat(meta_tokens.astype(f32), BATCH, axis=0)

    heads_per = LRU_HEADS // 2
    w_gate_blk = jnp.stack([
        jnp.concatenate([_block_diag([lru_wa[0, h] for h in range(j * heads_per, (j + 1) * heads_per)]),
                         _block_diag([lru_wx[0, h] for h in range(j * heads_per, (j + 1) * heads_per)])],
                        axis=1) for j in range(2)]).astype(bf16)
    b_gate_blk = jnp.stack([lru_ba[0], lru_bx[0]]).astype(f32)
    w_pool_blk = jnp.stack([_block_diag([pool_w[0, 2 * j], pool_w[0, 2 * j + 1]])
                            for j in range(2)]).astype(bf16)
    lam8 = row(LRU_C * jax.nn.log_sigmoid(lru_lambda[0].astype(f32)))
    pad = LANES - N_EXPERTS - N_GROUPS
    w_rt = jnp.concatenate([w_router[0], w_group[0], jnp.zeros((D_MODEL, pad), f32)], axis=1).astype(bf16)
    b_rt = jnp.concatenate([b_router[0], b_group[0], jnp.zeros((pad,), f32)]).reshape(1, LANES).astype(f32)
    ridx = jnp.arange(MIX_ROWS)
    tri = (ridx[None, :] < ridx[:, None]).astype(bf16)

    h1, xs, cnt = _mixer(
        x, meta_t, row(norm1_gain[0]), w_in[0].astype(bf16), conv_w[0].astype(f32),
        row(conv_b[0]), w_gate_blk, b_gate_blk, lam8, row(lru_out_gain[0]), w_pool_blk,
        row(pool_scale[0]), w_out[0].astype(bf16), row(norm2_gain[0]), w_rt, b_rt, tri)

    counts = cnt[0, :N_EXPERTS].astype(jnp.int32)
    nblk = (counts + MOE_BM - 1) // MOE_BM
    blk_end = jnp.cumsum(nblk)
    step = jnp.arange(N_BLOCKS, dtype=jnp.int32)
    used = step < blk_end[-1]
    step_c = jnp.minimum(step, blk_end[-1] - 1)
    e_of = jnp.sum(blk_end[None, :] <= step_c[:, None], axis=-1).astype(jnp.int32)
    onehot_e = e_of[:, None] == jnp.arange(N_EXPERTS, dtype=jnp.int32)[None, :]
    j_of = step_c - jnp.sum(jnp.where(onehot_e, (blk_end - nblk)[None, :], 0), axis=-1)
    cnt_of = jnp.sum(jnp.where(onehot_e, counts[None, :], 0), axis=-1)
    blk_idx = (e_of * CAP_BLOCKS + j_of).astype(jnp.int32)
    blk_nvalid = jnp.where(used, jnp.clip(cnt_of - j_of * MOE_BM, 0, MOE_BM), 0).astype(jnp.int32)

    y2 = _experts(blk_idx, e_of, blk_nvalid, blk_end[-1:].astype(jnp.int32), xs,
                  w_gate[0], w_up[0], w_down[0])
    return _final(h1, y2, row(final_gain))
```

```python
import functools

import jax
import jax.numpy as jnp
from jax import lax
from jax.experimental import pallas as pl
from jax.experimental.pallas import tpu as pltpu

D_MODEL = 1024
BATCH = 8
SEQ = 4096
N_META = 16
LRU_WIDTH = 512
LRU_HEADS = 8
LRU_HEAD_DIM = 64
CONV_WIDTH = 4
LRU_C = 8.0
POOL_WIDTH = 512
POOL_WINDOWS = (2, 4, 8, 16)
POOL_GROUP = 128
N_GROUPS = 4
EXPERTS_PER_GROUP = 8
N_EXPERTS = 32
TOP_K = 2
EXPERT_FF = 512
RMS_EPS = 1e-6
SQRT_FLOOR = 1e-30

N_TOK = BATCH * SEQ
SUBLANES = 8
LANES = 128
MIX_ROWS = 512
META_ROWS = N_META * BATCH
CONV_HALO = (CONV_WIDTH - 1) * BATCH
POOL_HALO = (max(POOL_WINDOWS) - 1) * BATCH
MIX_STEPS = N_TOK // MIX_ROWS
MIX_T = MIX_ROWS // BATCH
MOE_BM = 512
N_ASSIGN = N_TOK * TOP_K
N_BLOCKS = N_ASSIGN // MOE_BM + N_EXPERTS
EXPERT_CAP = N_TOK
CAP_BLOCKS = EXPERT_CAP // MOE_BM
ROW_SUB = 8
DATA_SUB = 4
HALF = D_MODEL // 2
XS_DUMP = N_EXPERTS * EXPERT_CAP
XS_ROWS = XS_DUMP + TOP_K * MIX_ROWS
SCATTER_BATCH = TOP_K * MIX_ROWS // 8
Y_DUMP = N_ASSIGN
Y_ROWS = Y_DUMP + 2 * MOE_BM
FIN_TM = 512
VMEM_LIMIT = 56 * 1024 * 1024


def _rms(x, gain):
    return x * lax.rsqrt(jnp.mean(x * x, axis=-1, keepdims=True) + RMS_EPS) * gain


def _bdot(a, b):
    return jnp.dot(a.astype(jnp.bfloat16), b, preferred_element_type=jnp.float32)


def _pack_bf16_pairs(lo, hi):
    lo_b = lax.bitcast_convert_type(lo.astype(jnp.bfloat16).astype(jnp.float32), jnp.uint32)
    hi_b = lax.bitcast_convert_type(hi.astype(jnp.bfloat16).astype(jnp.float32), jnp.uint32)
    return (lo_b >> 16) | (hi_b & jnp.uint32(0xFFFF0000))


def _unpack_bf16_pairs(u):
    lo = lax.bitcast_convert_type(u << 16, jnp.float32)
    hi = lax.bitcast_convert_type(u & jnp.uint32(0xFFFF0000), jnp.float32)
    return lo, hi


def _mix_rows(x, nrows, with_output, g1_ref, w_in_ref, conv_w_ref, conv_b_ref, w_gate_ref,
              b_gate_ref, lam8_ref, lru_gain_ref, w_pool_ref, pool_scale_ref, w_out_ref,
              ux_buf, up_buf, a_buf, b_buf, hs_buf, hstate, between=lambda: None):
    hn = _rms(x, g1_ref[...])
    proj = _bdot(hn, w_in_ref[...])
    between()
    ux = proj[:, :LRU_WIDTH]
    up = proj[:, 2 * LRU_WIDTH:]

    ux_buf[CONV_HALO:CONV_HALO + nrows, :] = ux
    xc = conv_b_ref[...] + conv_w_ref[3:4, :] * ux
    for k in range(CONV_WIDTH - 1):
        xc = xc + conv_w_ref[k:k + 1, :] * ux_buf[k * BATCH:k * BATCH + nrows, :]
    ux_buf[0:CONV_HALO, :] = ux_buf[nrows:nrows + CONV_HALO, :]
    between()

    half = LRU_WIDTH // 2
    z = [_bdot(xc[:, j * half:(j + 1) * half], w_gate_ref[j]) for j in range(2)]
    za = jnp.concatenate([z[0][:, :half], z[1][:, :half]], axis=-1) + b_gate_ref[0:1, :]
    zx = jnp.concatenate([z[0][:, half:], z[1][:, half:]], axis=-1) + b_gate_ref[1:2, :]
    r = jax.nn.sigmoid(za)
    gi = jax.nn.sigmoid(zx)
    log_a = lam8_ref[...] * r
    a = jnp.exp(log_a)
    a_buf[0:nrows, :] = a
    v = 1.0 - a * a
    root = v * lax.rsqrt(jnp.maximum(v, SQRT_FLOOR))
    b_buf[0:nrows, :] = root * (gi * xc)
    between()

    def step(t, h):
        r0 = pl.multiple_of(t * SUBLANES, SUBLANES)
        h = a_buf[pl.ds(r0, SUBLANES), :] * h + b_buf[pl.ds(r0, SUBLANES), :]
        hs_buf[pl.ds(r0, SUBLANES), :] = h
        return h

    hstate[...] = lax.fori_loop(0, nrows // SUBLANES, step, hstate[...], unroll=True)

    up_buf[POOL_HALO:POOL_HALO + nrows, :] = up
    if not with_output:
        up_buf[0:POOL_HALO, :] = up_buf[nrows:nrows + POOL_HALO, :]
        return None

    ug = proj[:, LRU_WIDTH:2 * LRU_WIDTH]
    y_lru = _rms(hs_buf[0:nrows, :] * jax.nn.gelu(ug), lru_gain_ref[...])

    pm = []
    for g, w in enumerate(POOL_WINDOWS):
        lo, hi = g * POOL_GROUP, (g + 1) * POOL_GROUP
        s = up_buf[POOL_HALO - (w - 1) * BATCH:POOL_HALO + nrows, lo:hi]
        m = 1
        while m < w:
            s = s[m * BATCH:, :] + s[:-m * BATCH, :]
            m *= 2
        pm.append(s * (1.0 / w) - up[:, lo:hi])
    up_buf[0:POOL_HALO, :] = up_buf[nrows:nrows + POOL_HALO, :]
    yp = [_bdot(jnp.concatenate(pm[2 * j:2 * j + 2], axis=-1), w_pool_ref[j]) for j in range(2)]
    y_pool = _rms(jnp.concatenate(yp, axis=-1), pool_scale_ref[...])
    between()

    y = _bdot(jnp.concatenate([y_lru, y_pool], axis=-1), w_out_ref[...])
    between()
    return x + y


def _mixer_kernel(x_hbm, meta_ref, g1_ref, w_in_ref, conv_w_ref, conv_b_ref, w_gate_ref,
                  b_gate_ref, lam8_ref, lru_gain_ref, w_pool_ref, pool_scale_ref, w_out_ref,
                  g2_ref, w_rt_ref, b_rt_ref, tri_ref,
                  h1_ref, xs_hbm, cnt_ref,
                  ux_buf, up_buf, a_buf, b_buf, hs_buf, hstate, running,
                  xin, rowbuf, dest_v, dest_s, in_sem, row_sem, idx_sem):
    i = pl.program_id(0)
    slot = i % 2
    n_rows = TOP_K * MIX_ROWS
    mix = functools.partial(
        _mix_rows, g1_ref=g1_ref, w_in_ref=w_in_ref, conv_w_ref=conv_w_ref,
        conv_b_ref=conv_b_ref, w_gate_ref=w_gate_ref, b_gate_ref=b_gate_ref, lam8_ref=lam8_ref,
        lru_gain_ref=lru_gain_ref, w_pool_ref=w_pool_ref, pool_scale_ref=pool_scale_ref,
        w_out_ref=w_out_ref, ux_buf=ux_buf, up_buf=up_buf, a_buf=a_buf, b_buf=b_buf,
        hs_buf=hs_buf, hstate=hstate)
    idx_copy = pltpu.make_async_copy(dest_v, dest_s, idx_sem)
    round_copy = pltpu.make_async_copy(rowbuf, xs_hbm.at[pl.ds(0, n_rows * ROW_SUB)], row_sem)

    def x_copy(step, s, b):
        return pltpu.make_async_copy(x_hbm.at[b, pl.ds(step * MIX_T, MIX_T), :],
                                     xin.at[s, :, b, :], in_sem.at[s])

    def scatter_rows(first, count):
        for n in range(first, first + count):
            k, r = divmod(n, MIX_ROWS)
            d = pl.multiple_of(dest_s[k, r], ROW_SUB)
            pltpu.make_async_copy(rowbuf.at[pl.ds(n * ROW_SUB, ROW_SUB)],
                                  xs_hbm.at[pl.ds(d, ROW_SUB)], row_sem).start(priority=n % 2)

    issued = [0]

    def scatter_batch():
        scatter_rows(issued[0], SCATTER_BATCH)
        issued[0] += SCATTER_BATCH

    @pl.when(i == 0)
    def _():
        for b in range(BATCH):
            x_copy(0, 0, b).start()
        ux_buf[0:CONV_HALO, :] = jnp.zeros((CONV_HALO, LRU_WIDTH), jnp.float32)
        up_buf[0:POOL_HALO, :] = jnp.zeros((POOL_HALO, POOL_WIDTH), jnp.float32)
        hstate[...] = jnp.zeros_like(hstate)
        running[...] = jnp.zeros_like(running)
        mix(meta_ref[...], META_ROWS, False)
        rowbuf[...] = jnp.zeros_like(rowbuf)
        col = lax.broadcasted_iota(jnp.int32, (SUBLANES, MIX_ROWS), 1)
        sub = lax.broadcasted_iota(jnp.int32, (SUBLANES, MIX_ROWS), 0)
        dest_v[...] = (XS_DUMP + jnp.minimum(sub, TOP_K - 1) * MIX_ROWS + col) * ROW_SUB
        idx_copy.start()

    @pl.when(i + 1 < MIX_STEPS)
    def _():
        for b in range(BATCH):
            x_copy(i + 1, 1 - slot, b).start()

    for b in range(BATCH):
        x_copy(i, slot, b).wait()
    x = xin[slot].reshape(MIX_ROWS, D_MODEL)
    idx_copy.wait()
    h1 = mix(x, MIX_ROWS, True, between=scatter_batch)
    h1_ref[...] = h1

    hn2 = _rms(h1, g2_ref[...])
    logits = _bdot(hn2, w_rt_ref[...]) + b_rt_ref[...]
    scatter_batch()
    lane = lax.broadcasted_iota(jnp.int32, (MIX_ROWS, LANES), 1)
    lane_f = lane.astype(jnp.float32)
    neg = jnp.float32(-jnp.inf)
    big = jnp.float32(4 * LANES)

    is_g = (lane >= N_EXPERTS) & (lane < N_EXPERTS + N_GROUPS)
    gl = jnp.where(is_g, logits, neg)
    gmax = jnp.max(gl, axis=-1, keepdims=True)
    gidx = jnp.min(jnp.where(gl == gmax, lane_f, big), axis=-1, keepdims=True)
    p_g = 1.0 / jnp.sum(jnp.exp(gl - gmax), axis=-1, keepdims=True)
    grp = gidx.astype(jnp.int32) - N_EXPERTS

    el = jnp.where((lane >> 3) == grp, logits, neg)
    m1 = jnp.max(el, axis=-1, keepdims=True)
    i1 = jnp.min(jnp.where(el == m1, lane_f, big), axis=-1, keepdims=True)
    el2 = jnp.where(lane_f == i1, neg, el)
    m2 = jnp.max(el2, axis=-1, keepdims=True)
    i2 = jnp.min(jnp.where(el2 == m2, lane_f, big), axis=-1, keepdims=True)
    e21 = jnp.exp(m2 - m1)
    den = 1.0 + e21
    gate = (p_g * (1.0 / den), p_g * (e21 / den))
    scatter_batch()

    sel = (lane_f == i1, lane_f == i2)
    onehot = jnp.where(sel[0] | sel[1], 1.0, 0.0)
    before = jnp.dot(tri_ref[...], onehot.astype(jnp.bfloat16),
                     preferred_element_type=jnp.float32) + running[0:1, :]
    rank = [jnp.sum(jnp.where(sel[k], before, 0.0), axis=-1, keepdims=True) for k in range(TOP_K)]
    running[0:1, :] = running[0:1, :] + jnp.sum(onehot, axis=0, keepdims=True)
    cnt_ref[...] = running[...]
    scatter_batch()
    assert issued[0] == n_rows

    eid = (i1.astype(jnp.int32), i2.astype(jnp.int32))
    dst = [(eid[k] * EXPERT_CAP + rank[k].astype(jnp.int32)) * ROW_SUB for k in range(TOP_K)]
    dst_tile = jnp.where(lane == 0, dst[0], jnp.where(lane == 1, dst[1], 0))
    dest_v[...] = jnp.transpose(dst_tile)[0:SUBLANES, :]
    idx_copy.start()

    data = _pack_bf16_pairs(hn2[:, :HALF], hn2[:, HALF:])
    tok = i * MIX_ROWS + lax.broadcasted_iota(jnp.int32, (MIX_ROWS, LANES), 0)

    round_copy.wait()
    for k in range(TOP_K):
        base = k * MIX_ROWS * ROW_SUB
        for c in range(DATA_SUB):
            rowbuf[pl.ds(base + c, MIX_ROWS, stride=ROW_SUB), :] = data[:, c * LANES:(c + 1) * LANES]
        ret = (tok * TOP_K + k) * DATA_SUB
        gbits = lax.bitcast_convert_type(jnp.broadcast_to(gate[k], (MIX_ROWS, LANES)), jnp.int32)
        meta = jnp.where(lane == 0, ret, jnp.where(lane == 1, gbits, 0))
        rowbuf[pl.ds(base + DATA_SUB, MIX_ROWS, stride=ROW_SUB), :] = lax.bitcast_convert_type(
            meta, jnp.uint32)

    @pl.when(i == MIX_STEPS - 1)
    def _():
        idx_copy.wait()
        scatter_rows(0, n_rows)
        round_copy.wait()


def _mixer(x, meta_t, g1, w_in, conv_w, conv_b, w_gate, b_gate, lam8, lru_gain, w_pool,
           pool_scale, w_out, g2, w_rt, b_rt, tri):
    full = lambda a: pl.BlockSpec(a.shape, lambda i: (0,) * a.ndim)
    consts = (meta_t, g1, w_in, conv_w, conv_b, w_gate, b_gate, lam8, lru_gain, w_pool,
              pool_scale, w_out, g2, w_rt, b_rt, tri)
    return pl.pallas_call(
        _mixer_kernel,
        grid=(MIX_STEPS,),
        in_specs=[pl.BlockSpec(memory_space=pl.ANY)] + [full(a) for a in consts],
        out_specs=[pl.BlockSpec((MIX_ROWS, D_MODEL), lambda i: (i, 0)),
                   pl.BlockSpec(memory_space=pl.ANY),
                   pl.BlockSpec((SUBLANES, LANES), lambda i: (0, 0))],
        out_shape=[jax.ShapeDtypeStruct((N_TOK, D_MODEL), jnp.float32),
                   jax.ShapeDtypeStruct((XS_ROWS * ROW_SUB, LANES), jnp.uint32),
                   jax.ShapeDtypeStruct((SUBLANES, LANES), jnp.float32)],
        scratch_shapes=[pltpu.VMEM((CONV_HALO + MIX_ROWS, LRU_WIDTH), jnp.float32),
                        pltpu.VMEM((POOL_HALO + MIX_ROWS, POOL_WIDTH), jnp.float32),
                        pltpu.VMEM((MIX_ROWS, LRU_WIDTH), jnp.float32),
                        pltpu.VMEM((MIX_ROWS, LRU_WIDTH), jnp.float32),
                        pltpu.VMEM((MIX_ROWS, LRU_WIDTH), jnp.float32),
                        pltpu.VMEM((SUBLANES, LRU_WIDTH), jnp.float32),
                        pltpu.VMEM((SUBLANES, LANES), jnp.float32),
                        pltpu.VMEM((2, MIX_T, BATCH, D_MODEL), jnp.float32),
                        pltpu.VMEM((TOP_K * MIX_ROWS * ROW_SUB, LANES), jnp.uint32),
                        pltpu.VMEM((SUBLANES, MIX_ROWS), jnp.int32),
                        pltpu.SMEM((SUBLANES, MIX_ROWS), jnp.int32),
                        pltpu.SemaphoreType.DMA((2,)),
                        pltpu.SemaphoreType.DMA,
                        pltpu.SemaphoreType.DMA],
        compiler_params=pltpu.CompilerParams(dimension_semantics=("arbitrary",),
                                             vmem_limit_bytes=VMEM_LIMIT),
        name="mixer",
    )(x, *consts)


def _experts_kernel(blk_ref, bexp_ref, nval_ref, ntot_ref, x_ref, wg_ref, wu_ref, wd_ref, y_hbm,
                    wgu_b, wd_b, obuf, addr_v, addr_s, row_sem, idx_sem):
    i = pl.program_id(0)
    slot = i % 2
    n_used = ntot_ref[0]
    idx_copy = pltpu.make_async_copy(addr_v, addr_s, idx_sem)

    def round_copy(s):
        return pltpu.make_async_copy(obuf.at[s], y_hbm.at[pl.ds(0, MOE_BM * DATA_SUB)], row_sem.at[s])

    @pl.when((i < n_used) & ((i == 0) | (bexp_ref[i] != bexp_ref[jnp.maximum(i - 1, 0)])))
    def _():
        wgu_b[:, 0:EXPERT_FF] = wg_ref[0].astype(jnp.bfloat16)
        wgu_b[:, EXPERT_FF:] = wu_ref[0].astype(jnp.bfloat16)
        wd_b[...] = wd_ref[0].astype(jnp.bfloat16)

    @pl.when(i < n_used)
    def _():
        meta = lax.bitcast_convert_type(x_ref[pl.ds(DATA_SUB, MOE_BM, stride=ROW_SUB), :], jnp.int32)
        row = lax.broadcasted_iota(jnp.int32, (MOE_BM, LANES), 0)
        lane = lax.broadcasted_iota(jnp.int32, (MOE_BM, LANES), 1)
        ret = jnp.where(row < nval_ref[i], meta, (Y_DUMP + slot * MOE_BM + row) * DATA_SUB)
        addr_v[...] = jnp.transpose(jnp.where(lane == 0, ret, 0))[0:SUBLANES, :]
        idx_copy.start()

        parts = [_unpack_bf16_pairs(x_ref[pl.ds(c, MOE_BM, stride=ROW_SUB), :]) for c in range(DATA_SUB)]
        x = jnp.concatenate([p[0] for p in parts] + [p[1] for p in parts], axis=-1)
        gu = _bdot(x, wgu_b[...])
        g = gu[:, :EXPERT_FF]
        hid = (g * jax.nn.sigmoid(g)) * gu[:, EXPERT_FF:]
        gate = lax.bitcast_convert_type(meta[:, 1:2], jnp.float32)
        o = _bdot(hid, wd_b[...]) * gate
        packed = _pack_bf16_pairs(o[:, :HALF], o[:, HALF:])

        @pl.when(i >= 2)
        def _():
            round_copy(slot).wait()

        for c in range(DATA_SUB):
            obuf[slot, pl.ds(c, MOE_BM, stride=DATA_SUB), :] = packed[:, c * LANES:(c + 1) * LANES]

        idx_copy.wait()
        for r in range(MOE_BM):
            d = pl.multiple_of(addr_s[0, r], DATA_SUB)
            pltpu.make_async_copy(obuf.at[slot, pl.ds(r * DATA_SUB, DATA_SUB)],
                                  y_hbm.at[pl.ds(d, DATA_SUB)], row_sem.at[slot]).start(priority=r % 2)

    @pl.when(i == pl.num_programs(0) - 1)
    def _():
        @pl.when(n_used >= 2)
        def _():
            round_copy(n_used % 2).wait()
        round_copy((n_used - 1) % 2).wait()


def _experts(blk_idx, blk_expert, blk_nvalid, n_used, xs, w_gate, w_up, w_down):
    wmap = lambda i, blk, bexp, nval, ntot: (bexp[i], 0, 0)
    return pl.pallas_call(
        _experts_kernel,
        grid_spec=pltpu.PrefetchScalarGridSpec(
            num_scalar_prefetch=4,
            grid=(N_BLOCKS,),
            in_specs=[pl.BlockSpec((MOE_BM * ROW_SUB, LANES),
                                   lambda i, blk, bexp, nval, ntot: (blk[i], 0)),
                      pl.BlockSpec((1, D_MODEL, EXPERT_FF), wmap),
                      pl.BlockSpec((1, D_MODEL, EXPERT_FF), wmap),
                      pl.BlockSpec((1, EXPERT_FF, D_MODEL), wmap)],
            out_specs=pl.BlockSpec(memory_space=pl.ANY),
            scratch_shapes=[pltpu.VMEM((D_MODEL, 2 * EXPERT_FF), jnp.bfloat16),
                            pltpu.VMEM((EXPERT_FF, D_MODEL), jnp.bfloat16),
                            pltpu.VMEM((2, MOE_BM * DATA_SUB, LANES), jnp.uint32),
                            pltpu.VMEM((SUBLANES, MOE_BM), jnp.int32),
                            pltpu.SMEM((SUBLANES, MOE_BM), jnp.int32),
                            pltpu.SemaphoreType.DMA((2,)),
                            pltpu.SemaphoreType.DMA]),
        out_shape=jax.ShapeDtypeStruct((Y_ROWS * DATA_SUB, LANES), jnp.uint32),
        compiler_params=pltpu.CompilerParams(dimension_semantics=("arbitrary",),
                                             vmem_limit_bytes=VMEM_LIMIT),
        name="experts",
    )(blk_idx, blk_expert, blk_nvalid, n_used, xs, w_gate, w_up, w_down)


def _final_kernel(h1_ref, y_ref, gain_ref, out_hbm, obuf, sem):
    i = pl.program_id(0)
    slot = i % 2
    stride = TOP_K * DATA_SUB

    def out_copy(step, s, b):
        return pltpu.make_async_copy(obuf.at[s, :, b, :],
                                     out_hbm.at[b, pl.ds(step * MIX_T, MIX_T), :], sem.at[s])

    lo, hi = [], []
    for c in range(DATA_SUB):
        y0 = _unpack_bf16_pairs(y_ref[pl.ds(c, FIN_TM, stride=stride), :])
        y1 = _unpack_bf16_pairs(y_ref[pl.ds(DATA_SUB + c, FIN_TM, stride=stride), :])
        lo.append(y0[0] + y1[0])
        hi.append(y0[1] + y1[1])
    y = jnp.concatenate(lo + hi, axis=-1)
    res = _rms(h1_ref[...] + y, gain_ref[...])

    @pl.when(i >= 2)
    def _():
        for b in range(BATCH):
            out_copy(i - 2, slot, b).wait()

    obuf[slot] = res.reshape(MIX_T, BATCH, D_MODEL)
    for b in range(BATCH):
        out_copy(i, slot, b).start()

    @pl.when(i == pl.num_programs(0) - 1)
    def _():
        for b in range(BATCH):
            out_copy(i - 1, 1 - slot, b).wait()
        for b in range(BATCH):
            out_copy(i, slot, b).wait()


def _final(h1, y2, final_gain):
    rows = FIN_TM * TOP_K * DATA_SUB
    return pl.pallas_call(
        _final_kernel,
        grid=(N_TOK // FIN_TM,),
        in_specs=[pl.BlockSpec((FIN_TM, D_MODEL), lambda i: (i, 0)),
                  pl.BlockSpec((rows, LANES), lambda i: (i, 0)),
                  pl.BlockSpec((1, D_MODEL), lambda i: (0, 0))],
        out_specs=pl.BlockSpec(memory_space=pl.ANY),
        out_shape=jax.ShapeDtypeStruct((BATCH, SEQ, D_MODEL), jnp.float32),
        scratch_shapes=[pltpu.VMEM((2, MIX_T, BATCH, D_MODEL), jnp.float32),
                        pltpu.SemaphoreType.DMA((2,))],
        compiler_params=pltpu.CompilerParams(dimension_semantics=("arbitrary",),
                                             vmem_limit_bytes=VMEM_LIMIT),
        name="final",
    )(h1, y2, final_gain)


def _block_diag(blocks):
    n = len(blocks)
    r, c = blocks[0].shape
    out = jnp.zeros((n * r, n * c), blocks[0].dtype)
    for k, blk in enumerate(blocks):
        out = out.at[k * r:(k + 1) * r, k * c:(k + 1) * c].set(blk)
    return out


def kernel(x, meta_tokens, norm1_gain, w_in, conv_w, conv_b, lru_wa, lru_ba, lru_wx, lru_bx,
           lru_lambda, lru_out_gain, pool_w, pool_scale, w_out, norm2_gain, w_group, b_group,
           w_router, b_router, w_gate, w_up, w_down, final_gain):
    assert x.shape == (BATCH, SEQ, D_MODEL) and norm1_gain.shape[0] == 1
    f32, bf16 = jnp.float32, jnp.bfloat16
    row = lambda v: v.reshape(1, -1).astype(f32)

    meta_t = jnp.repeat(meta_tokens.astype(f32), BATCH, axis=0)

    heads_per = LRU_HEADS // 2
    w_gate_blk = jnp.stack([
        jnp.concatenate([_block_diag([lru_wa[0, h] for h in range(j * heads_per, (j + 1) * heads_per)]),
                         _block_diag([lru_wx[0, h] for h in range(j * heads_per, (j + 1) * heads_per)])],
                        axis=1) for j in range(2)]).astype(bf16)
    b_gate_blk = jnp.stack([lru_ba[0], lru_bx[0]]).astype(f32)
    w_pool_blk = jnp.stack([_block_diag([pool_w[0, 2 * j], pool_w[0, 2 * j + 1]])
                            for j in range(2)]).astype(bf16)
    lam8 = row(LRU_C * jax.nn.log_sigmoid(lru_lambda[0].astype(f32)))
    pad = LANES - N_EXPERTS - N_GROUPS
    w_rt = jnp.concatenate([w_router[0], w_group[0], jnp.zeros((D_MODEL, pad), f32)], axis=1).astype(bf16)
    b_rt = jnp.concatenate([b_router[0], b_group[0], jnp.zeros((pad,), f32)]).reshape(1, LANES).astype(f32)
    ridx = jnp.arange(MIX_ROWS)
    tri = (ridx[None, :] < ridx[:, None]).astype(bf16)

    h1, xs, cnt = _mixer(
        x, meta_t, row(norm1_gain[0]), w_in[0].astype(bf16), conv_w[0].astype(f32),
        row(conv_b[0]), w_gate_blk, b_gate_blk, lam8, row(lru_out_gain[0]), w_pool_blk,
        row(pool_scale[0]), w_out[0].astype(bf16), row(norm2_gain[0]), w_rt, b_rt, tri)

    counts = cnt[0, :N_EXPERTS].astype(jnp.int32)
    nblk = (counts + MOE_BM - 1) // MOE_BM
    blk_end = jnp.cumsum(nblk)
    step = jnp.arange(N_BLOCKS, dtype=jnp.int32)
    used = step < blk_end[-1]
    step_c = jnp.minimum(step, blk_end[-1] - 1)
    e_of = jnp.sum(blk_end[None, :] <= step_c[:, None], axis=-1).astype(jnp.int32)
    onehot_e = e_of[:, None] == jnp.arange(N_EXPERTS, dtype=jnp.int32)[None, :]
    j_of = step_c - jnp.sum(jnp.where(onehot_e, (blk_end - nblk)[None, :], 0), axis=-1)
    cnt_of = jnp.sum(jnp.where(onehot_e, counts[None, :], 0), axis=-1)
    blk_idx = (e_of * CAP_BLOCKS + j_of).astype(jnp.int32)
    blk_nvalid = jnp.where(used, jnp.clip(cnt_of - j_of * MOE_BM, 0, MOE_BM), 0).astype(jnp.int32)

    y2 = _experts(blk_idx, e_of, blk_nvalid, blk_end[-1:].astype(jnp.int32), xs,
                  w_gate[0], w_up[0], w_down[0])
    return _final(h1, y2, row(final_gain))
```

```python
import functools

import jax
import jax.numpy as jnp
from jax import lax
from jax.experimental import pallas as pl
from jax.experimental.pallas import tpu as pltpu

D_MODEL = 1024
BATCH = 8
SEQ = 4096
N_META = 16
LRU_WIDTH = 512
LRU_HEADS = 8
LRU_HEAD_DIM = 64
CONV_WIDTH = 4
LRU_C = 8.0
POOL_WIDTH = 512
POOL_WINDOWS = (2, 4, 8, 16)
POOL_GROUP = 128
N_GROUPS = 4
EXPERTS_PER_GROUP = 8
N_EXPERTS = 32
TOP_K = 2
EXPERT_FF = 512
RMS_EPS = 1e-6
SQRT_FLOOR = 1e-30

N_TOK = BATCH * SEQ
SUBLANES = 8
LANES = 128
MIX_ROWS = 512
META_ROWS = N_META * BATCH
CONV_HALO = (CONV_WIDTH - 1) * BATCH
POOL_HALO = (max(POOL_WINDOWS) - 1) * BATCH
MIX_STEPS = N_TOK // MIX_ROWS
MIX_T = MIX_ROWS // BATCH
MOE_BM = 512
N_ASSIGN = N_TOK * TOP_K
N_BLOCKS = N_ASSIGN // MOE_BM + N_EXPERTS
EXPERT_CAP = N_TOK
CAP_BLOCKS = EXPERT_CAP // MOE_BM
ROW_SUB = 8
DATA_SUB = 4
HALF = D_MODEL // 2
XS_DUMP = N_EXPERTS * EXPERT_CAP
XS_ROWS = XS_DUMP + TOP_K * MIX_ROWS
SCATTER_BATCH = TOP_K * MIX_ROWS // 8
Y_DUMP = N_ASSIGN
Y_ROWS = Y_DUMP + 2 * MOE_BM
FIN_TM = 512
VMEM_LIMIT = 56 * 1024 * 1024


def _rms(x, gain):
    return x * lax.rsqrt(jnp.mean(x * x, axis=-1, keepdims=True) + RMS_EPS) * gain


def _bdot(a, b):
    return jnp.dot(a.astype(jnp.bfloat16), b, preferred_element_type=jnp.float32)


def _pack_bf16_pairs(lo, hi):
    lo_b = lax.bitcast_convert_type(lo.astype(jnp.bfloat16).astype(jnp.float32), jnp.uint32)
    hi_b = lax.bitcast_convert_type(hi.astype(jnp.bfloat16).astype(jnp.float32), jnp.uint32)
    return (lo_b >> 16) | (hi_b & jnp.uint32(0xFFFF0000))


def _unpack_bf16_pairs(u):
    lo = lax.bitcast_convert_type(u << 16, jnp.float32)
    hi = lax.bitcast_convert_type(u & jnp.uint32(0xFFFF0000), jnp.float32)
    return lo, hi


def _mix_rows(x, nrows, with_output, g1_ref, w_in_ref, conv_w_ref, conv_b_ref, w_gate_ref,
              b_gate_ref, lam8_ref, lru_gain_ref, w_pool_ref, pool_scale_ref, w_out_ref,
              ux_buf, up_buf, a_buf, b_buf, hs_buf, hstate, between=lambda: None):
    hn = _rms(x, g1_ref[...])
    proj = _bdot(hn, w_in_ref[...])
    between()
    ux = proj[:, :LRU_WIDTH]
    up = proj[:, 2 * LRU_WIDTH:]

    ux_buf[CONV_HALO:CONV_HALO + nrows, :] = ux
    xc = conv_b_ref[...] + conv_w_ref[3:4, :] * ux
    for k in range(CONV_WIDTH - 1):
        xc = xc + conv_w_ref[k:k + 1, :] * ux_buf[k * BATCH:k * BATCH + nrows, :]
    ux_buf[0:CONV_HALO, :] = ux_buf[nrows:nrows + CONV_HALO, :]
    between()

    half = LRU_WIDTH // 2
    z = [_bdot(xc[:, j * half:(j + 1) * half], w_gate_ref[j]) for j in range(2)]
    za = jnp.concatenate([z[0][:, :half], z[1][:, :half]], axis=-1) + b_gate_ref[0:1, :]
    zx = jnp.concatenate([z[0][:, half:], z[1][:, half:]], axis=-1) + b_gate_ref[1:2, :]
    r = jax.nn.sigmoid(za)
    gi = jax.nn.sigmoid(zx)
    log_a = lam8_ref[...] * r
    a = jnp.exp(log_a)
    a_buf[0:nrows, :] = a
    v = 1.0 - a * a
    root = v * lax.rsqrt(jnp.maximum(v, SQRT_FLOOR))
    b_buf[0:nrows, :] = root * (gi * xc)
    between()

    def step(t, h):
        r0 = pl.multiple_of(t * SUBLANES, SUBLANES)
        h = a_buf[pl.ds(r0, SUBLANES), :] * h + b_buf[pl.ds(r0, SUBLANES), :]
        hs_buf[pl.ds(r0, SUBLANES), :] = h
        return h

    hstate[...] = lax.fori_loop(0, nrows // SUBLANES, step, hstate[...], unroll=True)

    up_buf[POOL_HALO:POOL_HALO + nrows, :] = up
    if not with_output:
        up_buf[0:POOL_HALO, :] = up_buf[nrows:nrows + POOL_HALO, :]
        return None

    ug = proj[:, LRU_WIDTH:2 * LRU_WIDTH]
    y_lru = _rms(hs_buf[0:nrows, :] * jax.nn.gelu(ug), lru_gain_ref[...])

    pm = []
    for g, w in enumerate(POOL_WINDOWS):
        lo, hi = g * POOL_GROUP, (g + 1) * POOL_GROUP
        s = up_buf[POOL_HALO - (w - 1) * BATCH:POOL_HALO + nrows, lo:hi]
        m = 1
        while m < w:
            s = s[m * BATCH:, :] + s[:-m * BATCH, :]
            m *= 2
        pm.append(s * (1.0 / w) - up[:, lo:hi])
    up_buf[0:POOL_HALO, :] = up_buf[nrows:nrows + POOL_HALO, :]
    yp = [_bdot(jnp.concatenate(pm[2 * j:2 * j + 2], axis=-1), w_pool_ref[j]) for j in range(2)]
    y_pool = _rms(jnp.concatenate(yp, axis=-1), pool_scale_ref[...])
    between()

    y = _bdot(jnp.concatenate([y_lru, y_pool], axis=-1), w_out_ref[...])
    between()
    return x + y


def _mixer_kernel(x_hbm, meta_ref, g1_ref, w_in_ref, conv_w_ref, conv_b_ref, w_gate_ref,
                  b_gate_ref, lam8_ref, lru_gain_ref, w_pool_ref, pool_scale_ref, w_out_ref,
                  g2_ref, w_rt_ref, b_rt_ref, tri_ref,
                  h1_ref, xs_hbm, cnt_ref,
                  ux_buf, up_buf, a_buf, b_buf, hs_buf, hstate, running,
                  xin, rowbuf, dest_v, dest_s, in_sem, row_sem, idx_sem):
    i = pl.program_id(0)
    slot = i % 2
    n_rows = TOP_K * MIX_ROWS
    mix = functools.partial(
        _mix_rows, g1_ref=g1_ref, w_in_ref=w_in_ref, conv_w_ref=conv_w_ref,
        conv_b_ref=conv_b_ref, w_gate_ref=w_gate_ref, b_gate_ref=b_gate_ref, lam8_ref=lam8_ref,
        lru_gain_ref=lru_gain_ref, w_pool_ref=w_pool_ref, pool_scale_ref=pool_scale_ref,
        w_out_ref=w_out_ref, ux_buf=ux_buf, up_buf=up_buf, a_buf=a_buf, b_buf=b_buf,
        hs_buf=hs_buf, hstate=hstate)
    idx_copy = pltpu.make_async_copy(dest_v, dest_s, idx_sem)
    round_copy = pltpu.make_async_copy(rowbuf, xs_hbm.at[pl.ds(0, n_rows * ROW_SUB)], row_sem)

    def x_copy(step, s, b):
        return pltpu.make_async_copy(x_hbm.at[b, pl.ds(step * MIX_T, MIX_T), :],
                                     xin.at[s, :, b, :], in_sem.at[s])

    def scatter_rows(first, count):
        for n in range(first, first + count):
            k, r = divmod(n, MIX_ROWS)
            d = pl.multiple_of(dest_s[k, r], ROW_SUB)
            pltpu.make_async_copy(rowbuf.at[pl.ds(n * ROW_SUB, ROW_SUB)],
                                  xs_hbm.at[pl.ds(d, ROW_SUB)], row_sem).start(priority=n % 2)

    issued = [0]

    def scatter_batch():
        scatter_rows(issued[0], SCATTER_BATCH)
        issued[0] += SCATTER_BATCH

    @pl.when(i == 0)
    def _():
        for b in range(BATCH):
            x_copy(0, 0, b).start()
        ux_buf[0:CONV_HALO, :] = jnp.zeros((CONV_HALO, LRU_WIDTH), jnp.float32)
        up_buf[0:POOL_HALO, :] = jnp.zeros((POOL_HALO, POOL_WIDTH), jnp.float32)
        hstate[...] = jnp.zeros_like(hstate)
        running[...] = jnp.zeros_like(running)
        mix(meta_ref[...], META_ROWS, False)
        rowbuf[...] = jnp.zeros_like(rowbuf)
        col = lax.broadcasted_iota(jnp.int32, (SUBLANES, MIX_ROWS), 1)
        sub = lax.broadcasted_iota(jnp.int32, (SUBLANES, MIX_ROWS), 0)
        dest_v[...] = (XS_DUMP + jnp.minimum(sub, TOP_K - 1) * MIX_ROWS + col) * ROW_SUB
        idx_copy.start()

    @pl.when(i + 1 < MIX_STEPS)
    def _():
        for b in range(BATCH):
            x_copy(i + 1, 1 - slot, b).start()

    for b in range(BATCH):
        x_copy(i, slot, b).wait()
    x = xin[slot].reshape(MIX_ROWS, D_MODEL)
    idx_copy.wait()
    h1 = mix(x, MIX_ROWS, True, between=scatter_batch)
    h1_ref[...] = h1

    hn2 = _rms(h1, g2_ref[...])
    logits = _bdot(hn2, w_rt_ref[...]) + b_rt_ref[...]
    scatter_batch()
    lane = lax.broadcasted_iota(jnp.int32, (MIX_ROWS, LANES), 1)
    lane_f = lane.astype(jnp.float32)
    neg = jnp.float32(-jnp.inf)
    big = jnp.float32(4 * LANES)

    is_g = (lane >= N_EXPERTS) & (lane < N_EXPERTS + N_GROUPS)
    gl = jnp.where(is_g, logits, neg)
    gmax = jnp.max(gl, axis=-1, keepdims=True)
    gidx = jnp.min(jnp.where(gl == gmax, lane_f, big), axis=-1, keepdims=True)
    p_g = 1.0 / jnp.sum(jnp.exp(gl - gmax), axis=-1, keepdims=True)
    grp = gidx.astype(jnp.int32) - N_EXPERTS

    el = jnp.where((lane >> 3) == grp, logits, neg)
    m1 = jnp.max(el, axis=-1, keepdims=True)
    i1 = jnp.min(jnp.where(el == m1, lane_f, big), axis=-1, keepdims=True)
    el2 = jnp.where(lane_f == i1, neg, el)
    m2 = jnp.max(el2, axis=-1, keepdims=True)
    i2 = jnp.min(jnp.where(el2 == m2, lane_f, big), axis=-1, keepdims=True)
    e21 = jnp.exp(m2 - m1)
    den = 1.0 + e21
    gate = (p_g * (1.0 / den), p_g * (e21 / den))
    scatter_batch()

    sel = (lane_f == i1, lane_f == i2)
    onehot = jnp.where(sel[0] | sel[1], 1.0, 0.0)
    before = jnp.dot(tri_ref[...], onehot.astype(jnp.bfloat16),
                     preferred_element_type=jnp.float32) + running[0:1, :]
    rank = [jnp.sum(jnp.where(sel[k], before, 0.0), axis=-1, keepdims=True) for k in range(TOP_K)]
    running[0:1, :] = running[0:1, :] + jnp.sum(onehot, axis=0, keepdims=True)
    cnt_ref[...] = running[...]
    scatter_batch()
    assert issued[0] == n_rows

    eid = (i1.astype(jnp.int32), i2.astype(jnp.int32))
    dst = [(eid[k] * EXPERT_CAP + rank[k].astype(jnp.int32)) * ROW_SUB for k in range(TOP_K)]
    dst_tile = jnp.where(lane == 0, dst[0], jnp.where(lane == 1, dst[1], 0))
    dest_v[...] = jnp.transpose(dst_tile)[0:SUBLANES, :]
    idx_copy.start()

    data = _pack_bf16_pairs(hn2[:, :HALF], hn2[:, HALF:])
    tok = i * MIX_ROWS + lax.broadcasted_iota(jnp.int32, (MIX_ROWS, LANES), 0)

    round_copy.wait()
    for k in range(TOP_K):
        base = k * MIX_ROWS * ROW_SUB
        for c in range(DATA_SUB):
            rowbuf[pl.ds(base + c, MIX_ROWS, stride=ROW_SUB), :] = data[:, c * LANES:(c + 1) * LANES]
        ret = (tok * TOP_K + k) * DATA_SUB
        gbits = lax.bitcast_convert_type(jnp.broadcast_to(gate[k], (MIX_ROWS, LANES)), jnp.int32)
        meta = jnp.where(lane == 0, ret, jnp.where(lane == 1, gbits, 0))
        rowbuf[pl.ds(base + DATA_SUB, MIX_ROWS, stride=ROW_SUB), :] = lax.bitcast_convert_type(
            meta, jnp.uint32)

    @pl.when(i == MIX_STEPS - 1)
    def _():
        idx_copy.wait()
        scatter_rows(0, n_rows)
        round_copy.wait()


def _mixer(x, meta_t, g1, w_in, conv_w, conv_b, w_gate, b_gate, lam8, lru_gain, w_pool,
           pool_scale, w_out, g2, w_rt, b_rt, tri):
    full = lambda a: pl.BlockSpec(a.shape, lambda i: (0,) * a.ndim)
    consts = (meta_t, g1, w_in, conv_w, conv_b, w_gate, b_gate, lam8, lru_gain, w_pool,
              pool_scale, w_out, g2, w_rt, b_rt, tri)
    return pl.pallas_call(
        _mixer_kernel,
        grid=(MIX_STEPS,),
        in_specs=[pl.BlockSpec(memory_space=pl.ANY)] + [full(a) for a in consts],
        out_specs=[pl.BlockSpec((MIX_ROWS, D_MODEL), lambda i: (i, 0)),
                   pl.BlockSpec(memory_space=pl.ANY),
                   pl.BlockSpec((SUBLANES, LANES), lambda i: (0, 0))],
        out_shape=[jax.ShapeDtypeStruct((N_TOK, D_MODEL), jnp.float32),
                   jax.ShapeDtypeStruct((XS_ROWS * ROW_SUB, LANES), jnp.uint32),
                   jax.ShapeDtypeStruct((SUBLANES, LANES), jnp.float32)],
        scratch_shapes=[pltpu.VMEM((CONV_HALO + MIX_ROWS, LRU_WIDTH), jnp.float32),
                        pltpu.VMEM((POOL_HALO + MIX_ROWS, POOL_WIDTH), jnp.float32),
                        pltpu.VMEM((MIX_ROWS, LRU_WIDTH), jnp.float32),
                        pltpu.VMEM((MIX_ROWS, LRU_WIDTH), jnp.float32),
                        pltpu.VMEM((MIX_ROWS, LRU_WIDTH), jnp.float32),
                        pltpu.VMEM((SUBLANES, LRU_WIDTH), jnp.float32),
                        pltpu.VMEM((SUBLANES, LANES), jnp.float32),
                        pltpu.VMEM((2, MIX_T, BATCH, D_MODEL), jnp.float32),
                        pltpu.VMEM((TOP_K * MIX_ROWS * ROW_SUB, LANES), jnp.uint32),
                        pltpu.VMEM((SUBLANES, MIX_ROWS), jnp.int32),
                        pltpu.SMEM((SUBLANES, MIX_ROWS), jnp.int32),
                        pltpu.SemaphoreType.DMA((2,)),
                        pltpu.SemaphoreType.DMA,
                        pltpu.SemaphoreType.DMA],
        compiler_params=pltpu.CompilerParams(dimension_semantics=("arbitrary",),
                                             vmem_limit_bytes=VMEM_LIMIT),
        name="mixer",
    )(x, *consts)


def _experts_kernel(blk_ref, bexp_ref, nval_ref, ntot_ref, x_ref, wg_ref, wu_ref, wd_ref, y_hbm,
                    wgu_b, wd_b, obuf, addr_v, addr_s, row_sem, idx_sem):
    i = pl.program_id(0)
    slot = i % 2
    prev = 1 - slot
    n_used = ntot_ref[0]
    idx_copy = pltpu.make_async_copy(addr_v, addr_s.at[slot], idx_sem)
    prev_idx_copy = pltpu.make_async_copy(addr_v, addr_s.at[prev], idx_sem)

    def round_copy(s):
        return pltpu.make_async_copy(obuf.at[s], y_hbm.at[pl.ds(0, MOE_BM * DATA_SUB)], row_sem.at[s])

    def scatter_rows(first, count):
        for r in range(first, first + count):
            d = pl.multiple_of(addr_s[prev, 0, r], DATA_SUB)
            pltpu.make_async_copy(obuf.at[prev, pl.ds(r * DATA_SUB, DATA_SUB)],
                                  y_hbm.at[pl.ds(d, DATA_SUB)], row_sem.at[prev]).start(priority=r % 2)

    def dump_rows(s):
        return (Y_DUMP + s * MOE_BM + lax.broadcasted_iota(jnp.int32, (MOE_BM, LANES), 0)) * DATA_SUB

    @pl.when(i == 0)
    def _():
        obuf[1] = jnp.zeros((MOE_BM * DATA_SUB, LANES), jnp.uint32)
        addr_v[...] = jnp.transpose(dump_rows(1))[0:SUBLANES, :]
        prev_idx_copy.start()

    @pl.when((i < n_used) & ((i == 0) | (bexp_ref[i] != bexp_ref[jnp.maximum(i - 1, 0)])))
    def _():
        wgu_b[:, 0:EXPERT_FF] = wg_ref[0].astype(jnp.bfloat16)
        wgu_b[:, EXPERT_FF:] = wu_ref[0].astype(jnp.bfloat16)
        wd_b[...] = wd_ref[0].astype(jnp.bfloat16)

    @pl.when(i < n_used)
    def _():
        prev_idx_copy.wait()
        meta = lax.bitcast_convert_type(x_ref[pl.ds(DATA_SUB, MOE_BM, stride=ROW_SUB), :], jnp.int32)
        row = lax.broadcasted_iota(jnp.int32, (MOE_BM, LANES), 0)
        lane = lax.broadcasted_iota(jnp.int32, (MOE_BM, LANES), 1)
        ret = jnp.where(row < nval_ref[i], meta, dump_rows(slot))
        addr_v[...] = jnp.transpose(jnp.where(lane == 0, ret, 0))[0:SUBLANES, :]
        idx_copy.start()

        batch = MOE_BM // 4
        parts = [_unpack_bf16_pairs(x_ref[pl.ds(c, MOE_BM, stride=ROW_SUB), :]) for c in range(DATA_SUB)]
        x = jnp.concatenate([p[0] for p in parts] + [p[1] for p in parts], axis=-1)
        scatter_rows(0, batch)
        gu = _bdot(x, wgu_b[...])
        scatter_rows(batch, batch)
        g = gu[:, :EXPERT_FF]
        hid = (g * jax.nn.sigmoid(g)) * gu[:, EXPERT_FF:]
        scatter_rows(2 * batch, batch)
        gate = lax.bitcast_convert_type(meta[:, 1:2], jnp.float32)
        o = _bdot(hid, wd_b[...]) * gate
        scatter_rows(3 * batch, batch)
        packed = _pack_bf16_pairs(o[:, :HALF], o[:, HALF:])

        @pl.when(i >= 1)
        def _():
            round_copy(slot).wait()

        for c in range(DATA_SUB):
            obuf[slot, pl.ds(c, MOE_BM, stride=DATA_SUB), :] = packed[:, c * LANES:(c + 1) * LANES]

    @pl.when(i == n_used)
    def _():
        prev_idx_copy.wait()
        scatter_rows(0, MOE_BM)

    @pl.when(i == pl.num_programs(0) - 1)
    def _():
        round_copy(n_used % 2).wait()
        round_copy((n_used - 1) % 2).wait()


def _experts(blk_idx, blk_expert, blk_nvalid, n_used, xs, w_gate, w_up, w_down):
    wmap = lambda i, blk, bexp, nval, ntot: (bexp[i], 0, 0)
    return pl.pallas_call(
        _experts_kernel,
        grid_spec=pltpu.PrefetchScalarGridSpec(
            num_scalar_prefetch=4,
            grid=(N_BLOCKS + 1,),
            in_specs=[pl.BlockSpec((MOE_BM * ROW_SUB, LANES),
                                   lambda i, blk, bexp, nval, ntot: (blk[i], 0)),
                      pl.BlockSpec((1, D_MODEL, EXPERT_FF), wmap),
                      pl.BlockSpec((1, D_MODEL, EXPERT_FF), wmap),
                      pl.BlockSpec((1, EXPERT_FF, D_MODEL), wmap)],
            out_specs=pl.BlockSpec(memory_space=pl.ANY),
            scratch_shapes=[pltpu.VMEM((D_MODEL, 2 * EXPERT_FF), jnp.bfloat16),
                            pltpu.VMEM((EXPERT_FF, D_MODEL), jnp.bfloat16),
                            pltpu.VMEM((2, MOE_BM * DATA_SUB, LANES), jnp.uint32),
                            pltpu.VMEM((SUBLANES, MOE_BM), jnp.int32),
                            pltpu.SMEM((2, SUBLANES, MOE_BM), jnp.int32),
                            pltpu.SemaphoreType.DMA((2,)),
                            pltpu.SemaphoreType.DMA]),
        out_shape=jax.ShapeDtypeStruct((Y_ROWS * DATA_SUB, LANES), jnp.uint32),
        compiler_params=pltpu.CompilerParams(dimension_semantics=("arbitrary",),
                                             vmem_limit_bytes=VMEM_LIMIT),
        name="experts",
    )(blk_idx, blk_expert, blk_nvalid, n_used, xs, w_gate, w_up, w_down)


def _final_kernel(h1_ref, y_ref, gain_ref, out_hbm, obuf, sem):
    i = pl.program_id(0)
    slot = i % 2
    stride = TOP_K * DATA_SUB

    def out_copy(step, s, b):
        return pltpu.make_async_copy(obuf.at[s, :, b, :],
                                     out_hbm.at[b, pl.ds(step * MIX_T, MIX_T), :], sem.at[s])

    lo, hi = [], []
    for c in range(DATA_SUB):
        y0 = _unpack_bf16_pairs(y_ref[pl.ds(c, FIN_TM, stride=stride), :])
        y1 = _unpack_bf16_pairs(y_ref[pl.ds(DATA_SUB + c, FIN_TM, stride=stride), :])
        lo.append(y0[0] + y1[0])
        hi.append(y0[1] + y1[1])
    y = jnp.concatenate(lo + hi, axis=-1)
    res = _rms(h1_ref[...] + y, gain_ref[...])

    @pl.when(i >= 2)
    def _():
        for b in range(BATCH):
            out_copy(i - 2, slot, b).wait()

    obuf[slot] = res.reshape(MIX_T, BATCH, D_MODEL)
    for b in range(BATCH):
        out_copy(i, slot, b).start()

    @pl.when(i == pl.num_programs(0) - 1)
    def _():
        for b in range(BATCH):
            out_copy(i - 1, 1 - slot, b).wait()
        for b in range(BATCH):
            out_copy(i, slot, b).wait()


def _final(h1, y2, final_gain):
    rows = FIN_TM * TOP_K * DATA_SUB
    return pl.pallas_call(
        _final_kernel,
        grid=(N_TOK // FIN_TM,),
        in_specs=[pl.BlockSpec((FIN_TM, D_MODEL), lambda i: (i, 0)),
                  pl.BlockSpec((rows, LANES), lambda i: (i, 0)),
                  pl.BlockSpec((1, D_MODEL), lambda i: (0, 0))],
        out_specs=pl.BlockSpec(memory_space=pl.ANY),
        out_shape=jax.ShapeDtypeStruct((BATCH, SEQ, D_MODEL), jnp.float32),
        scratch_shapes=[pltpu.VMEM((2, MIX_T, BATCH, D_MODEL), jnp.float32),
                        pltpu.SemaphoreType.DMA((2,))],
        compiler_params=pltpu.CompilerParams(dimension_semantics=("arbitrary",),
                                             vmem_limit_bytes=VMEM_LIMIT),
        name="final",
    )(h1, y2, final_gain)


def _block_diag(blocks):
    n = len(blocks)
    r, c = blocks[0].shape
    out = jnp.zeros((n * r, n * c), blocks[0].dtype)
    for k, blk in enumerate(blocks):
        out = out.at[k * r:(k + 1) * r, k * c:(k + 1) * c].set(blk)
    return out


def kernel(x, meta_tokens, norm1_gain, w_in, conv_w, conv_b, lru_wa, lru_ba, lru_wx, lru_bx,
           lru_lambda, lru_out_gain, pool_w, pool_scale, w_out, norm2_gain, w_group, b_group,
           w_router, b_router, w_gate, w_up, w_down, final_gain):
    assert x.shape == (BATCH, SEQ, D_MODEL) and norm1_gain.shape[0] == 1
    f32, bf16 = jnp.float32, jnp.bfloat16
    row = lambda v: v.reshape(1, -1).astype(f32)

    meta_t = jnp.repeat(meta_tokens.astype(f32), BATCH, axis=0)

    heads_per = LRU_HEADS // 2
    w_gate_blk = jnp.stack([
        jnp.concatenate([_block_diag([lru_wa[0, h] for h in range(j * heads_per, (j + 1) * heads_per)]),
                         _block_diag([lru_wx[0, h] for h in range(j * heads_per, (j + 1) * heads_per)])],
                        axis=1) for j in range(2)]).astype(bf16)
    b_gate_blk = jnp.stack([lru_ba[0], lru_bx[0]]).astype(f32)
    w_pool_blk = jnp.stack([_block_diag([pool_w[0, 2 * j], pool_w[0, 2 * j + 1]])
                            for j in range(2)]).astype(bf16)
    lam8 = row(LRU_C * jax.nn.log_sigmoid(lru_lambda[0].astype(f32)))
    pad = LANES - N_EXPERTS - N_GROUPS
    w_rt = jnp.concatenate([w_router[0], w_group[0], jnp.zeros((D_MODEL, pad), f32)], axis=1).astype(bf16)
    b_rt = jnp.concatenate([b_router[0], b_group[0], jnp.zeros((pad,), f32)]).reshape(1, LANES).astype(f32)
    ridx = jnp.arange(MIX_ROWS)
    tri = (ridx[None, :] < ridx[:, None]).astype(bf16)

    h1, xs, cnt = _mixer(
        x, meta_t, row(norm1_gain[0]), w_in[0].astype(bf16), conv_w[0].astype(f32),
        row(conv_b[0]), w_gate_blk, b_gate_blk, lam8, row(lru_out_gain[0]), w_pool_blk,
        row(pool_scale[0]), w_out[0].astype(bf16), row(norm2_gain[0]), w_rt, b_rt, tri)

    counts = cnt[0, :N_EXPERTS].astype(jnp.int32)
    nblk = (counts + MOE_BM - 1) // MOE_BM
    blk_end = jnp.cumsum(nblk)
    step = jnp.arange(N_BLOCKS + 1, dtype=jnp.int32)
    used = step < blk_end[-1]
    step_c = jnp.minimum(step, blk_end[-1] - 1)
    e_of = jnp.sum(blk_end[None, :] <= step_c[:, None], axis=-1).astype(jnp.int32)
    onehot_e = e_of[:, None] == jnp.arange(N_EXPERTS, dtype=jnp.int32)[None, :]
    j_of = step_c - jnp.sum(jnp.where(onehot_e, (blk_end - nblk)[None, :], 0), axis=-1)
    cnt_of = jnp.sum(jnp.where(onehot_e, counts[None, :], 0), axis=-1)
    blk_idx = (e_of * CAP_BLOCKS + j_of).astype(jnp.int32)
    blk_nvalid = jnp.where(used, jnp.clip(cnt_of - j_of * MOE_BM, 0, MOE_BM), 0).astype(jnp.int32)

    y2 = _experts(blk_idx, e_of, blk_nvalid, blk_end[-1:].astype(jnp.int32), xs,
                  w_gate[0], w_up[0], w_down[0])
    return _final(h1, y2, row(final_gain))
```

```python
import functools

import jax
import jax.numpy as jnp
from jax import lax
from jax.experimental import pallas as pl
from jax.experimental.pallas import tpu as pltpu

D_MODEL = 1024
BATCH = 8
SEQ = 4096
N_META = 16
LRU_WIDTH = 512
LRU_HEADS = 8
LRU_HEAD_DIM = 64
CONV_WIDTH = 4
LRU_C = 8.0
POOL_WIDTH = 512
POOL_WINDOWS = (2, 4, 8, 16)
POOL_GROUP = 128
N_GROUPS = 4
EXPERTS_PER_GROUP = 8
N_EXPERTS = 32
TOP_K = 2
EXPERT_FF = 512
RMS_EPS = 1e-6
SQRT_FLOOR = 1e-30

N_TOK = BATCH * SEQ
SUBLANES = 8
LANES = 128
MIX_ROWS = 512
META_ROWS = N_META * BATCH
CONV_HALO = (CONV_WIDTH - 1) * BATCH
POOL_HALO = (max(POOL_WINDOWS) - 1) * BATCH
MIX_STEPS = N_TOK // MIX_ROWS
MIX_T = MIX_ROWS // BATCH
MOE_BM = 512
N_ASSIGN = N_TOK * TOP_K
N_BLOCKS = N_ASSIGN // MOE_BM + N_EXPERTS
EXPERT_CAP = N_TOK
CAP_BLOCKS = EXPERT_CAP // MOE_BM
ROW_SUB = 8
DATA_SUB = 4
HALF = D_MODEL // 2
XS_DUMP = N_EXPERTS * EXPERT_CAP
XS_ROWS = XS_DUMP + TOP_K * MIX_ROWS
SCATTER_BATCH = TOP_K * MIX_ROWS // 8
Y_DUMP = N_ASSIGN
Y_ROWS = Y_DUMP + 2 * MOE_BM
FIN_TM = 1024
FIN_T = FIN_TM // BATCH
VMEM_LIMIT = 56 * 1024 * 1024


def _rms(x, gain):
    return x * lax.rsqrt(jnp.mean(x * x, axis=-1, keepdims=True) + RMS_EPS) * gain


def _bdot(a, b):
    return jnp.dot(a.astype(jnp.bfloat16), b, preferred_element_type=jnp.float32)


def _pack_bf16_pairs(lo, hi):
    lo_b = lax.bitcast_convert_type(lo.astype(jnp.bfloat16).astype(jnp.float32), jnp.uint32)
    hi_b = lax.bitcast_convert_type(hi.astype(jnp.bfloat16).astype(jnp.float32), jnp.uint32)
    return (lo_b >> 16) | (hi_b & jnp.uint32(0xFFFF0000))


def _unpack_bf16_pairs(u):
    lo = lax.bitcast_convert_type(u << 16, jnp.float32)
    hi = lax.bitcast_convert_type(u & jnp.uint32(0xFFFF0000), jnp.float32)
    return lo, hi


def _mix_rows(x, nrows, with_output, g1_ref, w_in_ref, conv_w_ref, conv_b_ref, w_gate_ref,
              b_gate_ref, lam8_ref, lru_gain_ref, w_pool_ref, pool_scale_ref, w_out_ref,
              ux_buf, up_buf, a_buf, b_buf, hs_buf, hstate, between=lambda: None):
    hn = _rms(x, g1_ref[...])
    proj = _bdot(hn, w_in_ref[...])
    between()
    ux = proj[:, :LRU_WIDTH]
    up = proj[:, 2 * LRU_WIDTH:]

    ux_buf[CONV_HALO:CONV_HALO + nrows, :] = ux
    xc = conv_b_ref[...] + conv_w_ref[3:4, :] * ux
    for k in range(CONV_WIDTH - 1):
        xc = xc + conv_w_ref[k:k + 1, :] * ux_buf[k * BATCH:k * BATCH + nrows, :]
    ux_buf[0:CONV_HALO, :] = ux_buf[nrows:nrows + CONV_HALO, :]
    between()

    half = LRU_WIDTH // 2
    z = [_bdot(xc[:, j * half:(j + 1) * half], w_gate_ref[j]) for j in range(2)]
    za = jnp.concatenate([z[0][:, :half], z[1][:, :half]], axis=-1) + b_gate_ref[0:1, :]
    zx = jnp.concatenate([z[0][:, half:], z[1][:, half:]], axis=-1) + b_gate_ref[1:2, :]
    r = jax.nn.sigmoid(za)
    gi = jax.nn.sigmoid(zx)
    log_a = lam8_ref[...] * r
    a = jnp.exp(log_a)
    a_buf[0:nrows, :] = a
    v = 1.0 - a * a
    root = v * lax.rsqrt(jnp.maximum(v, SQRT_FLOOR))
    b_buf[0:nrows, :] = root * (gi * xc)
    between()

    def step(t, h):
        r0 = pl.multiple_of(t * SUBLANES, SUBLANES)
        h = a_buf[pl.ds(r0, SUBLANES), :] * h + b_buf[pl.ds(r0, SUBLANES), :]
        hs_buf[pl.ds(r0, SUBLANES), :] = h
        return h

    hstate[...] = lax.fori_loop(0, nrows // SUBLANES, step, hstate[...], unroll=True)

    up_buf[POOL_HALO:POOL_HALO + nrows, :] = up
    if not with_output:
        up_buf[0:POOL_HALO, :] = up_buf[nrows:nrows + POOL_HALO, :]
        return None

    ug = proj[:, LRU_WIDTH:2 * LRU_WIDTH]
    y_lru = _rms(hs_buf[0:nrows, :] * jax.nn.gelu(ug), lru_gain_ref[...])

    pm = []
    for g, w in enumerate(POOL_WINDOWS):
        lo, hi = g * POOL_GROUP, (g + 1) * POOL_GROUP
        s = up_buf[POOL_HALO - (w - 1) * BATCH:POOL_HALO + nrows, lo:hi]
        m = 1
        while m < w:
            s = s[m * BATCH:, :] + s[:-m * BATCH, :]
            m *= 2
        pm.append(s * (1.0 / w) - up[:, lo:hi])
    up_buf[0:POOL_HALO, :] = up_buf[nrows:nrows + POOL_HALO, :]
    yp = [_bdot(jnp.concatenate(pm[2 * j:2 * j + 2], axis=-1), w_pool_ref[j]) for j in range(2)]
    y_pool = _rms(jnp.concatenate(yp, axis=-1), pool_scale_ref[...])
    between()

    y = _bdot(jnp.concatenate([y_lru, y_pool], axis=-1), w_out_ref[...])
    between()
    return x + y


def _mixer_kernel(x_hbm, meta_ref, g1_ref, w_in_ref, conv_w_ref, conv_b_ref, w_gate_ref,
                  b_gate_ref, lam8_ref, lru_gain_ref, w_pool_ref, pool_scale_ref, w_out_ref,
                  g2_ref, w_rt_ref, b_rt_ref, tri_ref,
                  h1_ref, xs_hbm, cnt_ref,
                  ux_buf, up_buf, a_buf, b_buf, hs_buf, hstate, running,
                  xin, rowbuf, dest_v, dest_s, in_sem, row_sem, idx_sem):
    i = pl.program_id(0)
    slot = i % 2
    n_rows = TOP_K * MIX_ROWS
    mix = functools.partial(
        _mix_rows, g1_ref=g1_ref, w_in_ref=w_in_ref, conv_w_ref=conv_w_ref,
        conv_b_ref=conv_b_ref, w_gate_ref=w_gate_ref, b_gate_ref=b_gate_ref, lam8_ref=lam8_ref,
        lru_gain_ref=lru_gain_ref, w_pool_ref=w_pool_ref, pool_scale_ref=pool_scale_ref,
        w_out_ref=w_out_ref, ux_buf=ux_buf, up_buf=up_buf, a_buf=a_buf, b_buf=b_buf,
        hs_buf=hs_buf, hstate=hstate)
    idx_copy = pltpu.make_async_copy(dest_v, dest_s, idx_sem)
    round_copy = pltpu.make_async_copy(rowbuf, xs_hbm.at[pl.ds(0, n_rows * ROW_SUB)], row_sem)

    def x_copy(step, s, b):
        return pltpu.make_async_copy(x_hbm.at[b, pl.ds(step * MIX_T, MIX_T), :],
                                     xin.at[s, :, b, :], in_sem.at[s])

    def scatter_rows(first, count):
        for n in range(first, first + count):
            k, r = divmod(n, MIX_ROWS)
            d = pl.multiple_of(dest_s[k, r], ROW_SUB)
            pltpu.make_async_copy(rowbuf.at[pl.ds(n * ROW_SUB, ROW_SUB)],
                                  xs_hbm.at[pl.ds(d, ROW_SUB)], row_sem).start(priority=n % 2)

    issued = [0]

    def scatter_batch():
        scatter_rows(issued[0], SCATTER_BATCH)
        issued[0] += SCATTER_BATCH

    @pl.when(i == 0)
    def _():
        for b in range(BATCH):
            x_copy(0, 0, b).start()
        ux_buf[0:CONV_HALO, :] = jnp.zeros((CONV_HALO, LRU_WIDTH), jnp.float32)
        up_buf[0:POOL_HALO, :] = jnp.zeros((POOL_HALO, POOL_WIDTH), jnp.float32)
        hstate[...] = jnp.zeros_like(hstate)
        running[...] = jnp.zeros_like(running)
        mix(meta_ref[...], META_ROWS, False)
        rowbuf[...] = jnp.zeros_like(rowbuf)
        col = lax.broadcasted_iota(jnp.int32, (SUBLANES, MIX_ROWS), 1)
        sub = lax.broadcasted_iota(jnp.int32, (SUBLANES, MIX_ROWS), 0)
        dest_v[...] = (XS_DUMP + jnp.minimum(sub, TOP_K - 1) * MIX_ROWS + col) * ROW_SUB
        idx_copy.start()

    @pl.when(i + 1 < MIX_STEPS)
    def _():
        for b in range(BATCH):
            x_copy(i + 1, 1 - slot, b).start()

    for b in range(BATCH):
        x_copy(i, slot, b).wait()
    x = xin[slot].reshape(MIX_ROWS, D_MODEL)
    idx_copy.wait()
    h1 = mix(x, MIX_ROWS, True, between=scatter_batch)
    h1_ref[...] = h1

    hn2 = _rms(h1, g2_ref[...])
    logits = _bdot(hn2, w_rt_ref[...]) + b_rt_ref[...]
    scatter_batch()
    lane = lax.broadcasted_iota(jnp.int32, (MIX_ROWS, LANES), 1)
    lane_f = lane.astype(jnp.float32)
    neg = jnp.float32(-jnp.inf)
    big = jnp.float32(4 * LANES)

    is_g = (lane >= N_EXPERTS) & (lane < N_EXPERTS + N_GROUPS)
    gl = jnp.where(is_g, logits, neg)
    gmax = jnp.max(gl, axis=-1, keepdims=True)
    gidx = jnp.min(jnp.where(gl == gmax, lane_f, big), axis=-1, keepdims=True)
    p_g = 1.0 / jnp.sum(jnp.exp(gl - gmax), axis=-1, keepdims=True)
    grp = gidx.astype(jnp.int32) - N_EXPERTS

    el = jnp.where((lane >> 3) == grp, logits, neg)
    m1 = jnp.max(el, axis=-1, keepdims=True)
    i1 = jnp.min(jnp.where(el == m1, lane_f, big), axis=-1, keepdims=True)
    el2 = jnp.where(lane_f == i1, neg, el)
    m2 = jnp.max(el2, axis=-1, keepdims=True)
    i2 = jnp.min(jnp.where(el2 == m2, lane_f, big), axis=-1, keepdims=True)
    e21 = jnp.exp(m2 - m1)
    den = 1.0 + e21
    gate = (p_g * (1.0 / den), p_g * (e21 / den))
    scatter_batch()

    sel = (lane_f == i1, lane_f == i2)
    onehot = jnp.where(sel[0] | sel[1], 1.0, 0.0)
    before = jnp.dot(tri_ref[...], onehot.astype(jnp.bfloat16),
                     preferred_element_type=jnp.float32) + running[0:1, :]
    rank = [jnp.sum(jnp.where(sel[k], before, 0.0), axis=-1, keepdims=True) for k in range(TOP_K)]
    running[0:1, :] = running[0:1, :] + jnp.sum(onehot, axis=0, keepdims=True)
    cnt_ref[...] = running[...]
    scatter_batch()
    assert issued[0] == n_rows

    eid = (i1.astype(jnp.int32), i2.astype(jnp.int32))
    dst = [(eid[k] * EXPERT_CAP + rank[k].astype(jnp.int32)) * ROW_SUB for k in range(TOP_K)]
    dst_tile = jnp.where(lane == 0, dst[0], jnp.where(lane == 1, dst[1], 0))
    dest_v[...] = jnp.transpose(dst_tile)[0:SUBLANES, :]
    idx_copy.start()

    data = _pack_bf16_pairs(hn2[:, :HALF], hn2[:, HALF:])
    tok = i * MIX_ROWS + lax.broadcasted_iota(jnp.int32, (MIX_ROWS, LANES), 0)

    round_copy.wait()
    for k in range(TOP_K):
        base = k * MIX_ROWS * ROW_SUB
        for c in range(DATA_SUB):
            rowbuf[pl.ds(base + c, MIX_ROWS, stride=ROW_SUB), :] = data[:, c * LANES:(c + 1) * LANES]
        ret = (tok * TOP_K + k) * DATA_SUB
        gbits = lax.bitcast_convert_type(jnp.broadcast_to(gate[k], (MIX_ROWS, LANES)), jnp.int32)
        meta = jnp.where(lane == 0, ret, jnp.where(lane == 1, gbits, 0))
        rowbuf[pl.ds(base + DATA_SUB, MIX_ROWS, stride=ROW_SUB), :] = lax.bitcast_convert_type(
            meta, jnp.uint32)

    @pl.when(i == MIX_STEPS - 1)
    def _():
        idx_copy.wait()
        scatter_rows(0, n_rows)
        round_copy.wait()


def _mixer(x, meta_t, g1, w_in, conv_w, conv_b, w_gate, b_gate, lam8, lru_gain, w_pool,
           pool_scale, w_out, g2, w_rt, b_rt, tri):
    full = lambda a: pl.BlockSpec(a.shape, lambda i: (0,) * a.ndim)
    consts = (meta_t, g1, w_in, conv_w, conv_b, w_gate, b_gate, lam8, lru_gain, w_pool,
              pool_scale, w_out, g2, w_rt, b_rt, tri)
    return pl.pallas_call(
        _mixer_kernel,
        grid=(MIX_STEPS,),
        in_specs=[pl.BlockSpec(memory_space=pl.ANY)] + [full(a) for a in consts],
        out_specs=[pl.BlockSpec((MIX_ROWS, D_MODEL), lambda i: (i, 0)),
                   pl.BlockSpec(memory_space=pl.ANY),
                   pl.BlockSpec((SUBLANES, LANES), lambda i: (0, 0))],
        out_shape=[jax.ShapeDtypeStruct((N_TOK, D_MODEL), jnp.float32),
                   jax.ShapeDtypeStruct((XS_ROWS * ROW_SUB, LANES), jnp.uint32),
                   jax.ShapeDtypeStruct((SUBLANES, LANES), jnp.float32)],
        scratch_shapes=[pltpu.VMEM((CONV_HALO + MIX_ROWS, LRU_WIDTH), jnp.float32),
                        pltpu.VMEM((POOL_HALO + MIX_ROWS, POOL_WIDTH), jnp.float32),
                        pltpu.VMEM((MIX_ROWS, LRU_WIDTH), jnp.float32),
                        pltpu.VMEM((MIX_ROWS, LRU_WIDTH), jnp.float32),
                        pltpu.VMEM((MIX_ROWS, LRU_WIDTH), jnp.float32),
                        pltpu.VMEM((SUBLANES, LRU_WIDTH), jnp.float32),
                        pltpu.VMEM((SUBLANES, LANES), jnp.float32),
                        pltpu.VMEM((2, MIX_T, BATCH, D_MODEL), jnp.float32),
                        pltpu.VMEM((TOP_K * MIX_ROWS * ROW_SUB, LANES), jnp.uint32),
                        pltpu.VMEM((SUBLANES, MIX_ROWS), jnp.int32),
                        pltpu.SMEM((SUBLANES, MIX_ROWS), jnp.int32),
                        pltpu.SemaphoreType.DMA((2,)),
                        pltpu.SemaphoreType.DMA,
                        pltpu.SemaphoreType.DMA],
        compiler_params=pltpu.CompilerParams(dimension_semantics=("arbitrary",),
                                             vmem_limit_bytes=VMEM_LIMIT),
        name="mixer",
    )(x, *consts)


def _experts_kernel(blk_ref, bexp_ref, nval_ref, ntot_ref, x_ref, wg_ref, wu_ref, wd_ref, y_hbm,
                    wgu_b, wd_b, obuf, addr_v, addr_s, row_sem, idx_sem):
    i = pl.program_id(0)
    slot = i % 2
    prev = 1 - slot
    n_used = ntot_ref[0]
    idx_copy = pltpu.make_async_copy(addr_v, addr_s.at[slot], idx_sem)
    prev_idx_copy = pltpu.make_async_copy(addr_v, addr_s.at[prev], idx_sem)

    def round_copy(s):
        return pltpu.make_async_copy(obuf.at[s], y_hbm.at[pl.ds(0, MOE_BM * DATA_SUB)], row_sem.at[s])

    def scatter_rows(first, count):
        for r in range(first, first + count):
            d = pl.multiple_of(addr_s[prev, 0, r], DATA_SUB)
            pltpu.make_async_copy(obuf.at[prev, pl.ds(r * DATA_SUB, DATA_SUB)],
                                  y_hbm.at[pl.ds(d, DATA_SUB)], row_sem.at[prev]).start(priority=r % 2)

    def dump_rows(s):
        return (Y_DUMP + s * MOE_BM + lax.broadcasted_iota(jnp.int32, (MOE_BM, LANES), 0)) * DATA_SUB

    @pl.when(i == 0)
    def _():
        obuf[1] = jnp.zeros((MOE_BM * DATA_SUB, LANES), jnp.uint32)
        addr_v[...] = jnp.transpose(dump_rows(1))[0:SUBLANES, :]
        prev_idx_copy.start()

    @pl.when((i < n_used) & ((i == 0) | (bexp_ref[i] != bexp_ref[jnp.maximum(i - 1, 0)])))
    def _():
        wgu_b[:, 0:EXPERT_FF] = wg_ref[0].astype(jnp.bfloat16)
        wgu_b[:, EXPERT_FF:] = wu_ref[0].astype(jnp.bfloat16)
        wd_b[...] = wd_ref[0].astype(jnp.bfloat16)

    @pl.when(i < n_used)
    def _():
        prev_idx_copy.wait()
        meta = lax.bitcast_convert_type(x_ref[pl.ds(DATA_SUB, MOE_BM, stride=ROW_SUB), :], jnp.int32)
        row = lax.broadcasted_iota(jnp.int32, (MOE_BM, LANES), 0)
        lane = lax.broadcasted_iota(jnp.int32, (MOE_BM, LANES), 1)
        ret = jnp.where(row < nval_ref[i], meta, dump_rows(slot))
        addr_v[...] = jnp.transpose(jnp.where(lane == 0, ret, 0))[0:SUBLANES, :]
        idx_copy.start()

        batch = MOE_BM // 4
        parts = [_unpack_bf16_pairs(x_ref[pl.ds(c, MOE_BM, stride=ROW_SUB), :]) for c in range(DATA_SUB)]
        x = jnp.concatenate([p[0] for p in parts] + [p[1] for p in parts], axis=-1)
        scatter_rows(0, batch)
        gu = _bdot(x, wgu_b[...])
        scatter_rows(batch, batch)
        g = gu[:, :EXPERT_FF]
        hid = (g * jax.nn.sigmoid(g)) * gu[:, EXPERT_FF:]
        scatter_rows(2 * batch, batch)
        gate = lax.bitcast_convert_type(meta[:, 1:2], jnp.float32)
        o = _bdot(hid, wd_b[...]) * gate
        scatter_rows(3 * batch, batch)
        packed = _pack_bf16_pairs(o[:, :HALF], o[:, HALF:])

        @pl.when(i >= 1)
        def _():
            round_copy(slot).wait()

        for c in range(DATA_SUB):
            obuf[slot, pl.ds(c, MOE_BM, stride=DATA_SUB), :] = packed[:, c * LANES:(c + 1) * LANES]

    @pl.when(i == n_used)
    def _():
        prev_idx_copy.wait()
        scatter_rows(0, MOE_BM)

    @pl.when(i == pl.num_programs(0) - 1)
    def _():
        round_copy(n_used % 2).wait()
        round_copy((n_used - 1) % 2).wait()


def _experts(blk_idx, blk_expert, blk_nvalid, n_used, xs, w_gate, w_up, w_down):
    wmap = lambda i, blk, bexp, nval, ntot: (bexp[i], 0, 0)
    return pl.pallas_call(
        _experts_kernel,
        grid_spec=pltpu.PrefetchScalarGridSpec(
            num_scalar_prefetch=4,
            grid=(N_BLOCKS + 1,),
            in_specs=[pl.BlockSpec((MOE_BM * ROW_SUB, LANES),
                                   lambda i, blk, bexp, nval, ntot: (blk[i], 0)),
                      pl.BlockSpec((1, D_MODEL, EXPERT_FF), wmap),
                      pl.BlockSpec((1, D_MODEL, EXPERT_FF), wmap),
                      pl.BlockSpec((1, EXPERT_FF, D_MODEL), wmap)],
            out_specs=pl.BlockSpec(memory_space=pl.ANY),
            scratch_shapes=[pltpu.VMEM((D_MODEL, 2 * EXPERT_FF), jnp.bfloat16),
                            pltpu.VMEM((EXPERT_FF, D_MODEL), jnp.bfloat16),
                            pltpu.VMEM((2, MOE_BM * DATA_SUB, LANES), jnp.uint32),
                            pltpu.VMEM((SUBLANES, MOE_BM), jnp.int32),
                            pltpu.SMEM((2, SUBLANES, MOE_BM), jnp.int32),
                            pltpu.SemaphoreType.DMA((2,)),
                            pltpu.SemaphoreType.DMA]),
        out_shape=jax.ShapeDtypeStruct((Y_ROWS * DATA_SUB, LANES), jnp.uint32),
        compiler_params=pltpu.CompilerParams(dimension_semantics=("arbitrary",),
                                             vmem_limit_bytes=VMEM_LIMIT),
        name="experts",
    )(blk_idx, blk_expert, blk_nvalid, n_used, xs, w_gate, w_up, w_down)


def _final_kernel(h1_ref, y_ref, gain_ref, out_hbm, obuf, sem):
    i = pl.program_id(0)
    slot = i % 2
    stride = TOP_K * DATA_SUB

    def out_copy(step, s, b):
        return pltpu.make_async_copy(obuf.at[s, :, b, :],
                                     out_hbm.at[b, pl.ds(step * FIN_T, FIN_T), :], sem.at[s])

    lo, hi = [], []
    for c in range(DATA_SUB):
        y0 = _unpack_bf16_pairs(y_ref[pl.ds(c, FIN_TM, stride=stride), :])
        y1 = _unpack_bf16_pairs(y_ref[pl.ds(DATA_SUB + c, FIN_TM, stride=stride), :])
        lo.append(y0[0] + y1[0])
        hi.append(y0[1] + y1[1])
    y = jnp.concatenate(lo + hi, axis=-1)
    res = _rms(h1_ref[...] + y, gain_ref[...])

    @pl.when(i >= 2)
    def _():
        for b in range(BATCH):
            out_copy(i - 2, slot, b).wait()

    obuf[slot] = res.reshape(FIN_T, BATCH, D_MODEL)
    for b in range(BATCH):
        out_copy(i, slot, b).start()

    @pl.when(i == pl.num_programs(0) - 1)
    def _():
        for b in range(BATCH):
            out_copy(i - 1, 1 - slot, b).wait()
        for b in range(BATCH):
            out_copy(i, slot, b).wait()


def _final(h1, y2, final_gain):
    rows = FIN_TM * TOP_K * DATA_SUB
    return pl.pallas_call(
        _final_kernel,
        grid=(N_TOK // FIN_TM,),
        in_specs=[pl.BlockSpec((FIN_TM, D_MODEL), lambda i: (i, 0)),
                  pl.BlockSpec((rows, LANES), lambda i: (i, 0)),
                  pl.BlockSpec((1, D_MODEL), lambda i: (0, 0))],
        out_specs=pl.BlockSpec(memory_space=pl.ANY),
        out_shape=jax.ShapeDtypeStruct((BATCH, SEQ, D_MODEL), jnp.float32),
        scratch_shapes=[pltpu.VMEM((2, FIN_T, BATCH, D_MODEL), jnp.float32),
                        pltpu.SemaphoreType.DMA((2,))],
        compiler_params=pltpu.CompilerParams(dimension_semantics=("arbitrary",),
                                             vmem_limit_bytes=VMEM_LIMIT),
        name="final",
    )(h1, y2, final_gain)


def _block_diag(blocks):
    n, r, c = blocks.shape
    on_diag = jnp.eye(n, dtype=bool)[:, None, :, None]
    return jnp.where(on_diag, blocks[:, :, None, :], 0).reshape(n * r, n * c)


def kernel(x, meta_tokens, norm1_gain, w_in, conv_w, conv_b, lru_wa, lru_ba, lru_wx, lru_bx,
           lru_lambda, lru_out_gain, pool_w, pool_scale, w_out, norm2_gain, w_group, b_group,
           w_router, b_router, w_gate, w_up, w_down, final_gain):
    assert x.shape == (BATCH, SEQ, D_MODEL) and norm1_gain.shape[0] == 1
    f32, bf16 = jnp.float32, jnp.bfloat16
    row = lambda v: v.reshape(1, -1).astype(f32)

    meta_t = jnp.repeat(meta_tokens.astype(f32), BATCH, axis=0)

    heads_per = LRU_HEADS // 2
    w_gate_blk = jnp.stack([
        jnp.concatenate([_block_diag(lru_wa[0, j * heads_per:(j + 1) * heads_per]),
                         _block_diag(lru_wx[0, j * heads_per:(j + 1) * heads_per])],
                        axis=1) for j in range(2)]).astype(bf16)
    b_gate_blk = jnp.stack([lru_ba[0], lru_bx[0]]).astype(f32)
    w_pool_blk = jnp.stack([_block_diag(pool_w[0, 2 * j:2 * j + 2])
                            for j in range(2)]).astype(bf16)
    lam8 = row(LRU_C * jax.nn.log_sigmoid(lru_lambda[0].astype(f32)))
    pad = LANES - N_EXPERTS - N_GROUPS
    w_rt = jnp.concatenate([w_router[0], w_group[0], jnp.zeros((D_MODEL, pad), f32)], axis=1).astype(bf16)
    b_rt = jnp.concatenate([b_router[0], b_group[0], jnp.zeros((pad,), f32)]).reshape(1, LANES).astype(f32)
    ridx = jnp.arange(MIX_ROWS)
    tri = (ridx[None, :] < ridx[:, None]).astype(bf16)

    h1, xs, cnt = _mixer(
        x, meta_t, row(norm1_gain[0]), w_in[0].astype(bf16), conv_w[0].astype(f32),
        row(conv_b[0]), w_gate_blk, b_gate_blk, lam8, row(lru_out_gain[0]), w_pool_blk,
        row(pool_scale[0]), w_out[0].astype(bf16), row(norm2_gain[0]), w_rt, b_rt, tri)

    counts = cnt[0, :N_EXPERTS].astype(jnp.int32)
    nblk = (counts + MOE_BM - 1) // MOE_BM
    blk_end = jnp.cumsum(nblk)
    step = jnp.arange(N_BLOCKS + 1, dtype=jnp.int32)
    used = step < blk_end[-1]
    step_c = jnp.minimum(step, blk_end[-1] - 1)
    e_of = jnp.sum(blk_end[None, :] <= step_c[:, None], axis=-1).astype(jnp.int32)
    onehot_e = e_of[:, None] == jnp.arange(N_EXPERTS, dtype=jnp.int32)[None, :]
    j_of = step_c - jnp.sum(jnp.where(onehot_e, (blk_end - nblk)[None, :], 0), axis=-1)
    cnt_of = jnp.sum(jnp.where(onehot_e, counts[None, :], 0), axis=-1)
    blk_idx = (e_of * CAP_BLOCKS + j_of).astype(jnp.int32)
    blk_nvalid = jnp.where(used, jnp.clip(cnt_of - j_of * MOE_BM, 0, MOE_BM), 0).astype(jnp.int32)

    y2 = _experts(blk_idx, e_of, blk_nvalid, blk_end[-1:].astype(jnp.int32), xs,
                  w_gate[0], w_up[0], w_down[0])
    return _final(h1, y2, row(final_gain))
```

```python
import functools

import jax
import jax.numpy as jnp
from jax import lax
from jax.experimental import pallas as pl
from jax.experimental.pallas import tpu as pltpu

D_MODEL = 1024
BATCH = 8
SEQ = 4096
N_META = 16
LRU_WIDTH = 512
LRU_HEADS = 8
LRU_HEAD_DIM = 64
CONV_WIDTH = 4
LRU_C = 8.0
POOL_WIDTH = 512
POOL_WINDOWS = (2, 4, 8, 16)
POOL_GROUP = 128
N_GROUPS = 4
EXPERTS_PER_GROUP = 8
N_EXPERTS = 32
TOP_K = 2
EXPERT_FF = 512
RMS_EPS = 1e-6
SQRT_FLOOR = 1e-30

N_TOK = BATCH * SEQ
SUBLANES = 8
LANES = 128
MIX_ROWS = 512
META_ROWS = N_META * BATCH
CONV_HALO = (CONV_WIDTH - 1) * BATCH
POOL_HALO = (max(POOL_WINDOWS) - 1) * BATCH
MIX_STEPS = N_TOK // MIX_ROWS
MIX_T = MIX_ROWS // BATCH
MOE_BM = 512
N_ASSIGN = N_TOK * TOP_K
N_BLOCKS = N_ASSIGN // MOE_BM + N_EXPERTS
EXPERT_CAP = N_TOK
CAP_BLOCKS = EXPERT_CAP // MOE_BM
ROW_SUB = 8
DATA_SUB = 4
HALF = D_MODEL // 2
XS_DUMP = N_EXPERTS * EXPERT_CAP
XS_ROWS = XS_DUMP + TOP_K * MIX_ROWS
SCATTER_BATCH = TOP_K * MIX_ROWS // 8
Y_DUMP = N_ASSIGN
Y_ROWS = Y_DUMP + 2 * MOE_BM
FIN_TM = 1024
FIN_T = FIN_TM // BATCH
VMEM_LIMIT = 56 * 1024 * 1024


def _rms(x, gain):
    return x * lax.rsqrt(jnp.mean(x * x, axis=-1, keepdims=True) + RMS_EPS) * gain


def _sigmoid(x):
    return 0.5 * jnp.tanh(0.5 * x) + 0.5


def _bdot(a, b):
    return jnp.dot(a.astype(jnp.bfloat16), b, preferred_element_type=jnp.float32)


def _pack_bf16_pairs(lo, hi):
    lo_b = lax.bitcast_convert_type(lo.astype(jnp.bfloat16).astype(jnp.float32), jnp.uint32)
    hi_b = lax.bitcast_convert_type(hi.astype(jnp.bfloat16).astype(jnp.float32), jnp.uint32)
    return (lo_b >> 16) | (hi_b & jnp.uint32(0xFFFF0000))


def _unpack_bf16_pairs(u):
    lo = lax.bitcast_convert_type(u << 16, jnp.float32)
    hi = lax.bitcast_convert_type(u & jnp.uint32(0xFFFF0000), jnp.float32)
    return lo, hi


def _mix_rows(x, nrows, with_output, g1_ref, w_in_ref, conv_w_ref, conv_b_ref, w_gate_ref,
              b_gate_ref, lam8_ref, lru_gain_ref, w_pool_ref, pool_scale_ref, w_out_ref,
              ux_buf, up_buf, a_buf, b_buf, hs_buf, hstate, between=lambda: None):
    hn = _rms(x, g1_ref[...])
    proj = _bdot(hn, w_in_ref[...])
    between()
    ux = proj[:, :LRU_WIDTH]
    up = proj[:, 2 * LRU_WIDTH:]

    ux_buf[CONV_HALO:CONV_HALO + nrows, :] = ux
    xc = conv_b_ref[...] + conv_w_ref[3:4, :] * ux
    for k in range(CONV_WIDTH - 1):
        xc = xc + conv_w_ref[k:k + 1, :] * ux_buf[k * BATCH:k * BATCH + nrows, :]
    ux_buf[0:CONV_HALO, :] = ux_buf[nrows:nrows + CONV_HALO, :]
    between()

    half = LRU_WIDTH // 2
    z = [_bdot(xc[:, j * half:(j + 1) * half], w_gate_ref[j]) for j in range(2)]
    za = jnp.concatenate([z[0][:, :half], z[1][:, :half]], axis=-1) + b_gate_ref[0:1, :]
    zx = jnp.concatenate([z[0][:, half:], z[1][:, half:]], axis=-1) + b_gate_ref[1:2, :]
    r = _sigmoid(za)
    gi = _sigmoid(zx)
    log_a = lam8_ref[...] * r
    a = jnp.exp(log_a)
    a_buf[0:nrows, :] = a
    v = 1.0 - a * a
    root = v * lax.rsqrt(jnp.maximum(v, SQRT_FLOOR))
    b_buf[0:nrows, :] = root * (gi * xc)
    between()

    def step(t, h):
        r0 = pl.multiple_of(t * SUBLANES, SUBLANES)
        h = a_buf[pl.ds(r0, SUBLANES), :] * h + b_buf[pl.ds(r0, SUBLANES), :]
        hs_buf[pl.ds(r0, SUBLANES), :] = h
        return h

    hstate[...] = lax.fori_loop(0, nrows // SUBLANES, step, hstate[...], unroll=True)

    up_buf[POOL_HALO:POOL_HALO + nrows, :] = up
    if not with_output:
        up_buf[0:POOL_HALO, :] = up_buf[nrows:nrows + POOL_HALO, :]
        return None

    ug = proj[:, LRU_WIDTH:2 * LRU_WIDTH]
    y_lru = _rms(hs_buf[0:nrows, :] * jax.nn.gelu(ug), lru_gain_ref[...])

    pm = []
    for g, w in enumerate(POOL_WINDOWS):
        lo, hi = g * POOL_GROUP, (g + 1) * POOL_GROUP
        s = up_buf[POOL_HALO - (w - 1) * BATCH:POOL_HALO + nrows, lo:hi]
        m = 1
        while m < w:
            s = s[m * BATCH:, :] + s[:-m * BATCH, :]
            m *= 2
        pm.append(s * (1.0 / w) - up[:, lo:hi])
    up_buf[0:POOL_HALO, :] = up_buf[nrows:nrows + POOL_HALO, :]
    yp = [_bdot(jnp.concatenate(pm[2 * j:2 * j + 2], axis=-1), w_pool_ref[j]) for j in range(2)]
    y_pool = _rms(jnp.concatenate(yp, axis=-1), pool_scale_ref[...])
    between()

    y = _bdot(jnp.concatenate([y_lru, y_pool], axis=-1), w_out_ref[...])
    between()
    return x + y


def _mixer_kernel(x_hbm, meta_ref, g1_ref, w_in_ref, conv_w_ref, conv_b_ref, w_gate_ref,
                  b_gate_ref, lam8_ref, lru_gain_ref, w_pool_ref, pool_scale_ref, w_out_ref,
                  g2_ref, w_rt_ref, b_rt_ref, tri_ref,
                  h1_ref, xs_hbm, cnt_ref,
                  ux_buf, up_buf, a_buf, b_buf, hs_buf, hstate, running,
                  xin, rowbuf, dest_v, dest_s, in_sem, row_sem, idx_sem):
    i = pl.program_id(0)
    slot = i % 2
    n_rows = TOP_K * MIX_ROWS
    mix = functools.partial(
        _mix_rows, g1_ref=g1_ref, w_in_ref=w_in_ref, conv_w_ref=conv_w_ref,
        conv_b_ref=conv_b_ref, w_gate_ref=w_gate_ref, b_gate_ref=b_gate_ref, lam8_ref=lam8_ref,
        lru_gain_ref=lru_gain_ref, w_pool_ref=w_pool_ref, pool_scale_ref=pool_scale_ref,
        w_out_ref=w_out_ref, ux_buf=ux_buf, up_buf=up_buf, a_buf=a_buf, b_buf=b_buf,
        hs_buf=hs_buf, hstate=hstate)
    idx_copy = pltpu.make_async_copy(dest_v, dest_s, idx_sem)
    round_copy = pltpu.make_async_copy(rowbuf, xs_hbm.at[pl.ds(0, n_rows * ROW_SUB)], row_sem)

    def x_copy(step, s, b):
        return pltpu.make_async_copy(x_hbm.at[b, pl.ds(step * MIX_T, MIX_T), :],
                                     xin.at[s, :, b, :], in_sem.at[s])

    def scatter_rows(first, count):
        for n in range(first, first + count):
            k, r = divmod(n, MIX_ROWS)
            d = pl.multiple_of(dest_s[k, r], ROW_SUB)
            pltpu.make_async_copy(rowbuf.at[pl.ds(n * ROW_SUB, ROW_SUB)],
                                  xs_hbm.at[pl.ds(d, ROW_SUB)], row_sem).start(priority=n % 2)

    issued = [0]

    def scatter_batch():
        scatter_rows(issued[0], SCATTER_BATCH)
        issued[0] += SCATTER_BATCH

    @pl.when(i == 0)
    def _():
        for b in range(BATCH):
            x_copy(0, 0, b).start()
        ux_buf[0:CONV_HALO, :] = jnp.zeros((CONV_HALO, LRU_WIDTH), jnp.float32)
        up_buf[0:POOL_HALO, :] = jnp.zeros((POOL_HALO, POOL_WIDTH), jnp.float32)
        hstate[...] = jnp.zeros_like(hstate)
        running[...] = jnp.zeros_like(running)
        mix(meta_ref[...], META_ROWS, False)
        rowbuf[...] = jnp.zeros_like(rowbuf)
        col = lax.broadcasted_iota(jnp.int32, (SUBLANES, MIX_ROWS), 1)
        sub = lax.broadcasted_iota(jnp.int32, (SUBLANES, MIX_ROWS), 0)
        dest_v[...] = (XS_DUMP + jnp.minimum(sub, TOP_K - 1) * MIX_ROWS + col) * ROW_SUB
        idx_copy.start()

    @pl.when(i + 1 < MIX_STEPS)
    def _():
        for b in range(BATCH):
            x_copy(i + 1, 1 - slot, b).start()

    for b in range(BATCH):
        x_copy(i, slot, b).wait()
    x = xin[slot].reshape(MIX_ROWS, D_MODEL)
    idx_copy.wait()
    h1 = mix(x, MIX_ROWS, True, between=scatter_batch)
    h1_ref[...] = h1

    hn2 = _rms(h1, g2_ref[...])
    logits = _bdot(hn2, w_rt_ref[...]) + b_rt_ref[...]
    scatter_batch()
    lane = lax.broadcasted_iota(jnp.int32, (MIX_ROWS, LANES), 1)
    lane_f = lane.astype(jnp.float32)
    neg = jnp.float32(-jnp.inf)
    big = jnp.float32(4 * LANES)

    is_g = (lane >= N_EXPERTS) & (lane < N_EXPERTS + N_GROUPS)
    gl = jnp.where(is_g, logits, neg)
    gmax = jnp.max(gl, axis=-1, keepdims=True)
    gidx = jnp.min(jnp.where(gl == gmax, lane_f, big), axis=-1, keepdims=True)
    p_g = 1.0 / jnp.sum(jnp.exp(gl - gmax), axis=-1, keepdims=True)
    grp = gidx.astype(jnp.int32) - N_EXPERTS

    el = jnp.where((lane >> 3) == grp, logits, neg)
    m1 = jnp.max(el, axis=-1, keepdims=True)
    i1 = jnp.min(jnp.where(el == m1, lane_f, big), axis=-1, keepdims=True)
    el2 = jnp.where(lane_f == i1, neg, el)
    m2 = jnp.max(el2, axis=-1, keepdims=True)
    i2 = jnp.min(jnp.where(el2 == m2, lane_f, big), axis=-1, keepdims=True)
    e21 = jnp.exp(m2 - m1)
    den = 1.0 + e21
    gate = (p_g * (1.0 / den), p_g * (e21 / den))
    scatter_batch()

    sel = (lane_f == i1, lane_f == i2)
    onehot = jnp.where(sel[0] | sel[1], 1.0, 0.0)
    before = jnp.dot(tri_ref[...], onehot.astype(jnp.bfloat16),
                     preferred_element_type=jnp.float32) + running[0:1, :]
    rank = [jnp.sum(jnp.where(sel[k], before, 0.0), axis=-1, keepdims=True) for k in range(TOP_K)]
    running[0:1, :] = running[0:1, :] + jnp.sum(onehot, axis=0, keepdims=True)
    cnt_ref[...] = running[...]
    scatter_batch()
    assert issued[0] == n_rows

    eid = (i1.astype(jnp.int32), i2.astype(jnp.int32))
    dst = [(eid[k] * EXPERT_CAP + rank[k].astype(jnp.int32)) * ROW_SUB for k in range(TOP_K)]
    dst_tile = jnp.where(lane == 0, dst[0], jnp.where(lane == 1, dst[1], 0))
    dest_v[...] = jnp.transpose(dst_tile)[0:SUBLANES, :]
    idx_copy.start()

    data = _pack_bf16_pairs(hn2[:, :HALF], hn2[:, HALF:])
    tok = i * MIX_ROWS + lax.broadcasted_iota(jnp.int32, (MIX_ROWS, LANES), 0)

    round_copy.wait()
    for k in range(TOP_K):
        base = k * MIX_ROWS * ROW_SUB
        for c in range(DATA_SUB):
            rowbuf[pl.ds(base + c, MIX_ROWS, stride=ROW_SUB), :] = data[:, c * LANES:(c + 1) * LANES]
        ret = (tok * TOP_K + k) * DATA_SUB
        gbits = lax.bitcast_convert_type(jnp.broadcast_to(gate[k], (MIX_ROWS, LANES)), jnp.int32)
        meta = jnp.where(lane == 0, ret, jnp.where(lane == 1, gbits, 0))
        rowbuf[pl.ds(base + DATA_SUB, MIX_ROWS, stride=ROW_SUB), :] = lax.bitcast_convert_type(
            meta, jnp.uint32)

    @pl.when(i == MIX_STEPS - 1)
    def _():
        idx_copy.wait()
        scatter_rows(0, n_rows)
        round_copy.wait()


def _mixer(x, meta_t, g1, w_in, conv_w, conv_b, w_gate, b_gate, lam8, lru_gain, w_pool,
           pool_scale, w_out, g2, w_rt, b_rt, tri):
    full = lambda a: pl.BlockSpec(a.shape, lambda i: (0,) * a.ndim)
    consts = (meta_t, g1, w_in, conv_w, conv_b, w_gate, b_gate, lam8, lru_gain, w_pool,
              pool_scale, w_out, g2, w_rt, b_rt, tri)
    return pl.pallas_call(
        _mixer_kernel,
        grid=(MIX_STEPS,),
        in_specs=[pl.BlockSpec(memory_space=pl.ANY)] + [full(a) for a in consts],
        out_specs=[pl.BlockSpec((MIX_ROWS, D_MODEL), lambda i: (i, 0)),
                   pl.BlockSpec(memory_space=pl.ANY),
                   pl.BlockSpec((SUBLANES, LANES), lambda i: (0, 0))],
        out_shape=[jax.ShapeDtypeStruct((N_TOK, D_MODEL), jnp.float32),
                   jax.ShapeDtypeStruct((XS_ROWS * ROW_SUB, LANES), jnp.uint32),
                   jax.ShapeDtypeStruct((SUBLANES, LANES), jnp.float32)],
        scratch_shapes=[pltpu.VMEM((CONV_HALO + MIX_ROWS, LRU_WIDTH), jnp.float32),
                        pltpu.VMEM((POOL_HALO + MIX_ROWS, POOL_WIDTH), jnp.float32),
                        pltpu.VMEM((MIX_ROWS, LRU_WIDTH), jnp.float32),
                        pltpu.VMEM((MIX_ROWS, LRU_WIDTH), jnp.float32),
                        pltpu.VMEM((MIX_ROWS, LRU_WIDTH), jnp.float32),
                        pltpu.VMEM((SUBLANES, LRU_WIDTH), jnp.float32),
                        pltpu.VMEM((SUBLANES, LANES), jnp.float32),
                        pltpu.VMEM((2, MIX_T, BATCH, D_MODEL), jnp.float32),
                        pltpu.VMEM((TOP_K * MIX_ROWS * ROW_SUB, LANES), jnp.uint32),
                        pltpu.VMEM((SUBLANES, MIX_ROWS), jnp.int32),
                        pltpu.SMEM((SUBLANES, MIX_ROWS), jnp.int32),
                        pltpu.SemaphoreType.DMA((2,)),
                        pltpu.SemaphoreType.DMA,
                        pltpu.SemaphoreType.DMA],
        compiler_params=pltpu.CompilerParams(dimension_semantics=("arbitrary",),
                                             vmem_limit_bytes=VMEM_LIMIT),
        name="mixer",
    )(x, *consts)


def _experts_kernel(blk_ref, bexp_ref, nval_ref, ntot_ref, x_ref, wg_ref, wu_ref, wd_ref, y_hbm,
                    wgu_b, wd_b, obuf, addr_v, addr_s, row_sem, idx_sem):
    i = pl.program_id(0)
    slot = i % 2
    prev = 1 - slot
    n_used = ntot_ref[0]
    idx_copy = pltpu.make_async_copy(addr_v, addr_s.at[slot], idx_sem)
    prev_idx_copy = pltpu.make_async_copy(addr_v, addr_s.at[prev], idx_sem)

    def round_copy(s):
        return pltpu.make_async_copy(obuf.at[s], y_hbm.at[pl.ds(0, MOE_BM * DATA_SUB)], row_sem.at[s])

    def scatter_rows(first, count):
        for r in range(first, first + count):
            d = pl.multiple_of(addr_s[prev, 0, r], DATA_SUB)
            pltpu.make_async_copy(obuf.at[prev, pl.ds(r * DATA_SUB, DATA_SUB)],
                                  y_hbm.at[pl.ds(d, DATA_SUB)], row_sem.at[prev]).start(priority=r % 2)

    def dump_rows(s):
        return (Y_DUMP + s * MOE_BM + lax.broadcasted_iota(jnp.int32, (MOE_BM, LANES), 0)) * DATA_SUB

    @pl.when(i == 0)
    def _():
        obuf[1] = jnp.zeros((MOE_BM * DATA_SUB, LANES), jnp.uint32)
        addr_v[...] = jnp.transpose(dump_rows(1))[0:SUBLANES, :]
        prev_idx_copy.start()

    @pl.when((i < n_used) & ((i == 0) | (bexp_ref[i] != bexp_ref[jnp.maximum(i - 1, 0)])))
    def _():
        wgu_b[:, 0:EXPERT_FF] = wg_ref[0].astype(jnp.bfloat16)
        wgu_b[:, EXPERT_FF:] = wu_ref[0].astype(jnp.bfloat16)
        wd_b[...] = wd_ref[0].astype(jnp.bfloat16)

    @pl.when(i < n_used)
    def _():
        prev_idx_copy.wait()
        meta = lax.bitcast_convert_type(x_ref[pl.ds(DATA_SUB, MOE_BM, stride=ROW_SUB), :], jnp.int32)
        row = lax.broadcasted_iota(jnp.int32, (MOE_BM, LANES), 0)
        lane = lax.broadcasted_iota(jnp.int32, (MOE_BM, LANES), 1)
        ret = jnp.where(row < nval_ref[i], meta, dump_rows(slot))
        addr_v[...] = jnp.transpose(jnp.where(lane == 0, ret, 0))[0:SUBLANES, :]
        idx_copy.start()

        batch = MOE_BM // 4
        parts = [_unpack_bf16_pairs(x_ref[pl.ds(c, MOE_BM, stride=ROW_SUB), :]) for c in range(DATA_SUB)]
        x = jnp.concatenate([p[0] for p in parts] + [p[1] for p in parts], axis=-1)
        scatter_rows(0, batch)
        gu = _bdot(x, wgu_b[...])
        scatter_rows(batch, batch)
        g = gu[:, :EXPERT_FF]
        hid = (g * _sigmoid(g)) * gu[:, EXPERT_FF:]
        scatter_rows(2 * batch, batch)
        gate = lax.bitcast_convert_type(meta[:, 1:2], jnp.float32)
        o = _bdot(hid, wd_b[...]) * gate
        scatter_rows(3 * batch, batch)
        packed = _pack_bf16_pairs(o[:, :HALF], o[:, HALF:])

        @pl.when(i >= 1)
        def _():
            round_copy(slot).wait()

        for c in range(DATA_SUB):
            obuf[slot, pl.ds(c, MOE_BM, stride=DATA_SUB), :] = packed[:, c * LANES:(c + 1) * LANES]

    @pl.when(i == n_used)
    def _():
        prev_idx_copy.wait()
        scatter_rows(0, MOE_BM)

    @pl.when(i == pl.num_programs(0) - 1)
    def _():
        round_copy(n_used % 2).wait()
        round_copy((n_used - 1) % 2).wait()


def _experts(blk_idx, blk_expert, blk_nvalid, n_used, xs, w_gate, w_up, w_down):
    wmap = lambda i, blk, bexp, nval, ntot: (bexp[i], 0, 0)
    return pl.pallas_call(
        _experts_kernel,
        grid_spec=pltpu.PrefetchScalarGridSpec(
            num_scalar_prefetch=4,
            grid=(N_BLOCKS + 1,),
            in_specs=[pl.BlockSpec((MOE_BM * ROW_SUB, LANES),
                                   lambda i, blk, bexp, nval, ntot: (blk[i], 0)),
                      pl.BlockSpec((1, D_MODEL, EXPERT_FF), wmap),
                      pl.BlockSpec((1, D_MODEL, EXPERT_FF), wmap),
                      pl.BlockSpec((1, EXPERT_FF, D_MODEL), wmap)],
            out_specs=pl.BlockSpec(memory_space=pl.ANY),
            scratch_shapes=[pltpu.VMEM((D_MODEL, 2 * EXPERT_FF), jnp.bfloat16),
                            pltpu.VMEM((EXPERT_FF, D_MODEL), jnp.bfloat16),
                            pltpu.VMEM((2, MOE_BM * DATA_SUB, LANES), jnp.uint32),
                            pltpu.VMEM((SUBLANES, MOE_BM), jnp.int32),
                            pltpu.SMEM((2, SUBLANES, MOE_BM), jnp.int32),
                            pltpu.SemaphoreType.DMA((2,)),
                            pltpu.SemaphoreType.DMA]),
        out_shape=jax.ShapeDtypeStruct((Y_ROWS * DATA_SUB, LANES), jnp.uint32),
        compiler_params=pltpu.CompilerParams(dimension_semantics=("arbitrary",),
                                             vmem_limit_bytes=VMEM_LIMIT),
        name="experts",
    )(blk_idx, blk_expert, blk_nvalid, n_used, xs, w_gate, w_up, w_down)


def _final_kernel(h1_ref, y_ref, gain_ref, out_hbm, obuf, sem):
    i = pl.program_id(0)
    slot = i % 2
    stride = TOP_K * DATA_SUB

    def out_copy(step, s, b):
        return pltpu.make_async_copy(obuf.at[s, :, b, :],
                                     out_hbm.at[b, pl.ds(step * FIN_T, FIN_T), :], sem.at[s])

    lo, hi = [], []
    for c in range(DATA_SUB):
        y0 = _unpack_bf16_pairs(y_ref[pl.ds(c, FIN_TM, stride=stride), :])
        y1 = _unpack_bf16_pairs(y_ref[pl.ds(DATA_SUB + c, FIN_TM, stride=stride), :])
        lo.append(y0[0] + y1[0])
        hi.append(y0[1] + y1[1])
    y = jnp.concatenate(lo + hi, axis=-1)
    res = _rms(h1_ref[...] + y, gain_ref[...])

    @pl.when(i >= 2)
    def _():
        for b in range(BATCH):
            out_copy(i - 2, slot, b).wait()

    obuf[slot] = res.reshape(FIN_T, BATCH, D_MODEL)
    for b in range(BATCH):
        out_copy(i, slot, b).start()

    @pl.when(i == pl.num_programs(0) - 1)
    def _():
        for b in range(BATCH):
            out_copy(i - 1, 1 - slot, b).wait()
        for b in range(BATCH):
            out_copy(i, slot, b).wait()


def _final(h1, y2, final_gain):
    rows = FIN_TM * TOP_K * DATA_SUB
    return pl.pallas_call(
        _final_kernel,
        grid=(N_TOK // FIN_TM,),
        in_specs=[pl.BlockSpec((FIN_TM, D_MODEL), lambda i: (i, 0)),
                  pl.BlockSpec((rows, LANES), lambda i: (i, 0)),
                  pl.BlockSpec((1, D_MODEL), lambda i: (0, 0))],
        out_specs=pl.BlockSpec(memory_space=pl.ANY),
        out_shape=jax.ShapeDtypeStruct((BATCH, SEQ, D_MODEL), jnp.float32),
        scratch_shapes=[pltpu.VMEM((2, FIN_T, BATCH, D_MODEL), jnp.float32),
                        pltpu.SemaphoreType.DMA((2,))],
        compiler_params=pltpu.CompilerParams(dimension_semantics=("arbitrary",),
                                             vmem_limit_bytes=VMEM_LIMIT),
        name="final",
    )(h1, y2, final_gain)


def _block_diag(blocks):
    n, r, c = blocks.shape
    on_diag = jnp.eye(n, dtype=bool)[:, None, :, None]
    return jnp.where(on_diag, blocks[:, :, None, :], 0).reshape(n * r, n * c)


def kernel(x, meta_tokens, norm1_gain, w_in, conv_w, conv_b, lru_wa, lru_ba, lru_wx, lru_bx,
           lru_lambda, lru_out_gain, pool_w, pool_scale, w_out, norm2_gain, w_group, b_group,
           w_router, b_router, w_gate, w_up, w_down, final_gain):
    assert x.shape == (BATCH, SEQ, D_MODEL) and norm1_gain.shape[0] == 1
    f32, bf16 = jnp.float32, jnp.bfloat16
    row = lambda v: v.reshape(1, -1).astype(f32)

    meta_t = jnp.repeat(meta_tokens.astype(f32), BATCH, axis=0)

    heads_per = LRU_HEADS // 2
    w_gate_blk = jnp.stack([
        jnp.concatenate([_block_diag(lru_wa[0, j * heads_per:(j + 1) * heads_per]),
                         _block_diag(lru_wx[0, j * heads_per:(j + 1) * heads_per])],
                        axis=1) for j in range(2)]).astype(bf16)
    b_gate_blk = jnp.stack([lru_ba[0], lru_bx[0]]).astype(f32)
    w_pool_blk = jnp.stack([_block_diag(pool_w[0, 2 * j:2 * j + 2])
                            for j in range(2)]).astype(bf16)
    lam8 = row(LRU_C * jax.nn.log_sigmoid(lru_lambda[0].astype(f32)))
    pad = LANES - N_EXPERTS - N_GROUPS
    w_rt = jnp.concatenate([w_router[0], w_group[0], jnp.zeros((D_MODEL, pad), f32)], axis=1).astype(bf16)
    b_rt = jnp.concatenate([b_router[0], b_group[0], jnp.zeros((pad,), f32)]).reshape(1, LANES).astype(f32)
    ridx = jnp.arange(MIX_ROWS)
    tri = (ridx[None, :] < ridx[:, None]).astype(bf16)

    h1, xs, cnt = _mixer(
        x, meta_t, row(norm1_gain[0]), w_in[0].astype(bf16), conv_w[0].astype(f32),
        row(conv_b[0]), w_gate_blk, b_gate_blk, lam8, row(lru_out_gain[0]), w_pool_blk,
        row(pool_scale[0]), w_out[0].astype(bf16), row(norm2_gain[0]), w_rt, b_rt, tri)

    counts = cnt[0, :N_EXPERTS].astype(jnp.int32)
    nblk = (counts + MOE_BM - 1) // MOE_BM
    blk_end = jnp.cumsum(nblk)
    step = jnp.arange(N_BLOCKS + 1, dtype=jnp.int32)
    used = step < blk_end[-1]
    step_c = jnp.minimum(step, blk_end[-1] - 1)
    e_of = jnp.sum(blk_end[None, :] <= step_c[:, None], axis=-1).astype(jnp.int32)
    onehot_e = e_of[:, None] == jnp.arange(N_EXPERTS, dtype=jnp.int32)[None, :]
    j_of = step_c - jnp.sum(jnp.where(onehot_e, (blk_end - nblk)[None, :], 0), axis=-1)
    cnt_of = jnp.sum(jnp.where(onehot_e, counts[None, :], 0), axis=-1)
    blk_idx = (e_of * CAP_BLOCKS + j_of).astype(jnp.int32)
    blk_nvalid = jnp.where(used, jnp.clip(cnt_of - j_of * MOE_BM, 0, MOE_BM), 0).astype(jnp.int32)

    y2 = _experts(blk_idx, e_of, blk_nvalid, blk_end[-1:].astype(jnp.int32), xs,
                  w_gate[0], w_up[0], w_down[0])
    return _final(h1, y2, row(final_gain))
```

```python
import functools

import jax
import jax.numpy as jnp
from jax import lax
from jax.experimental import pallas as pl
from jax.experimental.pallas import tpu as pltpu

D_MODEL = 1024
BATCH = 8
SEQ = 4096
N_META = 16
LRU_WIDTH = 512
LRU_HEADS = 8
LRU_HEAD_DIM = 64
CONV_WIDTH = 4
LRU_C = 8.0
POOL_WIDTH = 512
POOL_WINDOWS = (2, 4, 8, 16)
POOL_GROUP = 128
N_GROUPS = 4
EXPERTS_PER_GROUP = 8
N_EXPERTS = 32
TOP_K = 2
EXPERT_FF = 512
RMS_EPS = 1e-6
SQRT_FLOOR = 1e-37

N_TOK = BATCH * SEQ
SUBLANES = 8
LANES = 128
MIX_ROWS = 512
META_ROWS = N_META * BATCH
CONV_HALO = (CONV_WIDTH - 1) * BATCH
POOL_HALO = (max(POOL_WINDOWS) - 1) * BATCH
MIX_STEPS = N_TOK // MIX_ROWS
MIX_T = MIX_ROWS // BATCH
MOE_BM = 512
N_ASSIGN = N_TOK * TOP_K
N_BLOCKS = N_ASSIGN // MOE_BM + N_EXPERTS
EXPERT_CAP = N_TOK
CAP_BLOCKS = EXPERT_CAP // MOE_BM
ROW_SUB = 8
DATA_SUB = 4
HALF = D_MODEL // 2
XS_DUMP = N_EXPERTS * EXPERT_CAP
XS_ROWS = XS_DUMP + TOP_K * MIX_ROWS
SCATTER_BATCH = TOP_K * MIX_ROWS // 8
Y_DUMP = N_ASSIGN
Y_ROWS = Y_DUMP + 2 * MOE_BM
FIN_TM = 1024
FIN_T = FIN_TM // BATCH
VMEM_LIMIT = 56 * 1024 * 1024


def _rms(x, gain):
    return x * lax.rsqrt(jnp.mean(x * x, axis=-1, keepdims=True) + RMS_EPS) * gain


def _sigmoid(x):
    return 0.5 * jnp.tanh(0.5 * x) + 0.5


def _bdot(a, b):
    return jnp.dot(a.astype(jnp.bfloat16), b, preferred_element_type=jnp.float32)


def _pack_bf16_pairs(lo, hi):
    lo_b = lax.bitcast_convert_type(lo.astype(jnp.bfloat16).astype(jnp.float32), jnp.uint32)
    hi_b = lax.bitcast_convert_type(hi.astype(jnp.bfloat16).astype(jnp.float32), jnp.uint32)
    return (lo_b >> 16) | (hi_b & jnp.uint32(0xFFFF0000))


def _unpack_bf16_pairs(u):
    lo = lax.bitcast_convert_type(u << 16, jnp.float32)
    hi = lax.bitcast_convert_type(u & jnp.uint32(0xFFFF0000), jnp.float32)
    return lo, hi


def _mix_rows(x, nrows, with_output, g1_ref, w_in_ref, conv_w_ref, conv_b_ref, w_gate_ref,
              b_gate_ref, lam8_ref, lru_gain_ref, w_pool_ref, pool_scale_ref, w_out_ref,
              ux_buf, up_buf, a_buf, b_buf, hs_buf, hstate, between=lambda: None):
    hn = _rms(x, g1_ref[...])
    proj = _bdot(hn, w_in_ref[...])
    between()
    ux = proj[:, :LRU_WIDTH]
    up = proj[:, 2 * LRU_WIDTH:]

    ux_buf[CONV_HALO:CONV_HALO + nrows, :] = ux
    xc = conv_b_ref[...] + conv_w_ref[3:4, :] * ux
    for k in range(CONV_WIDTH - 1):
        xc = xc + conv_w_ref[k:k + 1, :] * ux_buf[k * BATCH:k * BATCH + nrows, :]
    ux_buf[0:CONV_HALO, :] = ux_buf[nrows:nrows + CONV_HALO, :]
    between()

    half = LRU_WIDTH // 2
    z = [_bdot(xc[:, j * half:(j + 1) * half], w_gate_ref[j]) for j in range(2)]
    za = jnp.concatenate([z[0][:, :half], z[1][:, :half]], axis=-1) + b_gate_ref[0:1, :]
    zx = jnp.concatenate([z[0][:, half:], z[1][:, half:]], axis=-1) + b_gate_ref[1:2, :]
    r = _sigmoid(za)
    gi = _sigmoid(zx)
    log_a = lam8_ref[...] * r
    a = jnp.exp(log_a)
    a_buf[0:nrows, :] = a
    v = jnp.tanh(-log_a) * (1.0 + a * a)
    root = v * lax.rsqrt(jnp.maximum(v, SQRT_FLOOR))
    b_buf[0:nrows, :] = root * (gi * xc)
    between()

    def step(t, h):
        r0 = pl.multiple_of(t * SUBLANES, SUBLANES)
        h = a_buf[pl.ds(r0, SUBLANES), :] * h + b_buf[pl.ds(r0, SUBLANES), :]
        hs_buf[pl.ds(r0, SUBLANES), :] = h
        return h

    hstate[...] = lax.fori_loop(0, nrows // SUBLANES, step, hstate[...], unroll=True)

    up_buf[POOL_HALO:POOL_HALO + nrows, :] = up
    if not with_output:
        up_buf[0:POOL_HALO, :] = up_buf[nrows:nrows + POOL_HALO, :]
        return None

    ug = proj[:, LRU_WIDTH:2 * LRU_WIDTH]
    y_lru = _rms(hs_buf[0:nrows, :] * jax.nn.gelu(ug), lru_gain_ref[...])

    pm = []
    for g, w in enumerate(POOL_WINDOWS):
        lo, hi = g * POOL_GROUP, (g + 1) * POOL_GROUP
        s = up_buf[POOL_HALO - (w - 1) * BATCH:POOL_HALO + nrows, lo:hi]
        m = 1
        while m < w:
            s = s[m * BATCH:, :] + s[:-m * BATCH, :]
            m *= 2
        pm.append(s * (1.0 / w) - up[:, lo:hi])
    up_buf[0:POOL_HALO, :] = up_buf[nrows:nrows + POOL_HALO, :]
    yp = [_bdot(jnp.concatenate(pm[2 * j:2 * j + 2], axis=-1), w_pool_ref[j]) for j in range(2)]
    y_pool = _rms(jnp.concatenate(yp, axis=-1), pool_scale_ref[...])
    between()

    y = _bdot(jnp.concatenate([y_lru, y_pool], axis=-1), w_out_ref[...])
    between()
    return x + y


def _mixer_kernel(x_hbm, meta_ref, g1_ref, w_in_ref, conv_w_ref, conv_b_ref, w_gate_ref,
                  b_gate_ref, lam8_ref, lru_gain_ref, w_pool_ref, pool_scale_ref, w_out_ref,
                  g2_ref, w_rt_ref, b_rt_ref, tri_ref,
                  h1_ref, xs_hbm, cnt_ref,
                  ux_buf, up_buf, a_buf, b_buf, hs_buf, hstate, running,
                  xin, rowbuf, dest_v, dest_s, in_sem, row_sem, idx_sem):
    i = pl.program_id(0)
    slot = i % 2
    n_rows = TOP_K * MIX_ROWS
    mix = functools.partial(
        _mix_rows, g1_ref=g1_ref, w_in_ref=w_in_ref, conv_w_ref=conv_w_ref,
        conv_b_ref=conv_b_ref, w_gate_ref=w_gate_ref, b_gate_ref=b_gate_ref, lam8_ref=lam8_ref,
        lru_gain_ref=lru_gain_ref, w_pool_ref=w_pool_ref, pool_scale_ref=pool_scale_ref,
        w_out_ref=w_out_ref, ux_buf=ux_buf, up_buf=up_buf, a_buf=a_buf, b_buf=b_buf,
        hs_buf=hs_buf, hstate=hstate)
    idx_copy = pltpu.make_async_copy(dest_v, dest_s, idx_sem)
    round_copy = pltpu.make_async_copy(rowbuf, xs_hbm.at[pl.ds(0, n_rows * ROW_SUB)], row_sem)

    def x_copy(step, s, b):
        return pltpu.make_async_copy(x_hbm.at[b, pl.ds(step * MIX_T, MIX_T), :],
                                     xin.at[s, :, b, :], in_sem.at[s])

    def scatter_rows(first, count):
        for n in range(first, first + count):
            k, r = divmod(n, MIX_ROWS)
            d = pl.multiple_of(dest_s[k, r], ROW_SUB)
            pltpu.make_async_copy(rowbuf.at[pl.ds(n * ROW_SUB, ROW_SUB)],
                                  xs_hbm.at[pl.ds(d, ROW_SUB)], row_sem).start(priority=n % 2)

    issued = [0]

    def scatter_batch():
        scatter_rows(issued[0], SCATTER_BATCH)
        issued[0] += SCATTER_BATCH

    @pl.when(i == 0)
    def _():
        for b in range(BATCH):
            x_copy(0, 0, b).start()
        ux_buf[0:CONV_HALO, :] = jnp.zeros((CONV_HALO, LRU_WIDTH), jnp.float32)
        up_buf[0:POOL_HALO, :] = jnp.zeros((POOL_HALO, POOL_WIDTH), jnp.float32)
        hstate[...] = jnp.zeros_like(hstate)
        running[...] = jnp.zeros_like(running)
        mix(meta_ref[...], META_ROWS, False)
        rowbuf[...] = jnp.zeros_like(rowbuf)
        col = lax.broadcasted_iota(jnp.int32, (SUBLANES, MIX_ROWS), 1)
        sub = lax.broadcasted_iota(jnp.int32, (SUBLANES, MIX_ROWS), 0)
        dest_v[...] = (XS_DUMP + jnp.minimum(sub, TOP_K - 1) * MIX_ROWS + col) * ROW_SUB
        idx_copy.start()

    @pl.when(i + 1 < MIX_STEPS)
    def _():
        for b in range(BATCH):
            x_copy(i + 1, 1 - slot, b).start()

    for b in range(BATCH):
        x_copy(i, slot, b).wait()
    x = xin[slot].reshape(MIX_ROWS, D_MODEL)
    idx_copy.wait()
    h1 = mix(x, MIX_ROWS, True, between=scatter_batch)
    h1_ref[...] = h1

    hn2 = _rms(h1, g2_ref[...])
    logits = _bdot(hn2, w_rt_ref[...]) + b_rt_ref[...]
    scatter_batch()
    lane = lax.broadcasted_iota(jnp.int32, (MIX_ROWS, LANES), 1)
    lane_f = lane.astype(jnp.float32)
    neg = jnp.float32(-jnp.inf)
    big = jnp.float32(4 * LANES)

    is_g = (lane >= N_EXPERTS) & (lane < N_EXPERTS + N_GROUPS)
    gl = jnp.where(is_g, logits, neg)
    gmax = jnp.max(gl, axis=-1, keepdims=True)
    gidx = jnp.min(jnp.where(gl == gmax, lane_f, big), axis=-1, keepdims=True)
    p_g = 1.0 / jnp.sum(jnp.exp(gl - gmax), axis=-1, keepdims=True)
    grp = gidx.astype(jnp.int32) - N_EXPERTS

    el = jnp.where((lane >> 3) == grp, logits, neg)
    m1 = jnp.max(el, axis=-1, keepdims=True)
    i1 = jnp.min(jnp.where(el == m1, lane_f, big), axis=-1, keepdims=True)
    el2 = jnp.where(lane_f == i1, neg, el)
    m2 = jnp.max(el2, axis=-1, keepdims=True)
    i2 = jnp.min(jnp.where(el2 == m2, lane_f, big), axis=-1, keepdims=True)
    e21 = jnp.exp(m2 - m1)
    den = 1.0 + e21
    gate = (p_g * (1.0 / den), p_g * (e21 / den))
    scatter_batch()

    sel = (lane_f == i1, lane_f == i2)
    onehot = jnp.where(sel[0] | sel[1], 1.0, 0.0)
    before = jnp.dot(tri_ref[...], onehot.astype(jnp.bfloat16),
                     preferred_element_type=jnp.float32) + running[0:1, :]
    rank = [jnp.sum(jnp.where(sel[k], before, 0.0), axis=-1, keepdims=True) for k in range(TOP_K)]
    running[0:1, :] = running[0:1, :] + jnp.sum(onehot, axis=0, keepdims=True)
    cnt_ref[...] = running[...]
    scatter_batch()
    assert issued[0] == n_rows

    eid = (i1.astype(jnp.int32), i2.astype(jnp.int32))
    dst = [(eid[k] * EXPERT_CAP + rank[k].astype(jnp.int32)) * ROW_SUB for k in range(TOP_K)]
    dst_tile = jnp.where(lane == 0, dst[0], jnp.where(lane == 1, dst[1], 0))
    dest_v[...] = jnp.transpose(dst_tile)[0:SUBLANES, :]
    idx_copy.start()

    data = _pack_bf16_pairs(hn2[:, :HALF], hn2[:, HALF:])
    tok = i * MIX_ROWS + lax.broadcasted_iota(jnp.int32, (MIX_ROWS, LANES), 0)

    round_copy.wait()
    for k in range(TOP_K):
        base = k * MIX_ROWS * ROW_SUB
        for c in range(DATA_SUB):
            rowbuf[pl.ds(base + c, MIX_ROWS, stride=ROW_SUB), :] = data[:, c * LANES:(c + 1) * LANES]
        ret = (tok * TOP_K + k) * DATA_SUB
        gbits = lax.bitcast_convert_type(jnp.broadcast_to(gate[k], (MIX_ROWS, LANES)), jnp.int32)
        meta = jnp.where(lane == 0, ret, jnp.where(lane == 1, gbits, 0))
        rowbuf[pl.ds(base + DATA_SUB, MIX_ROWS, stride=ROW_SUB), :] = lax.bitcast_convert_type(
            meta, jnp.uint32)

    @pl.when(i == MIX_STEPS - 1)
    def _():
        idx_copy.wait()
        scatter_rows(0, n_rows)
        round_copy.wait()


def _mixer(x, meta_t, g1, w_in, conv_w, conv_b, w_gate, b_gate, lam8, lru_gain, w_pool,
           pool_scale, w_out, g2, w_rt, b_rt, tri):
    full = lambda a: pl.BlockSpec(a.shape, lambda i: (0,) * a.ndim)
    consts = (meta_t, g1, w_in, conv_w, conv_b, w_gate, b_gate, lam8, lru_gain, w_pool,
              pool_scale, w_out, g2, w_rt, b_rt, tri)
    return pl.pallas_call(
        _mixer_kernel,
        grid=(MIX_STEPS,),
        in_specs=[pl.BlockSpec(memory_space=pl.ANY)] + [full(a) for a in consts],
        out_specs=[pl.BlockSpec((MIX_ROWS, D_MODEL), lambda i: (i, 0)),
                   pl.BlockSpec(memory_space=pl.ANY),
                   pl.BlockSpec((SUBLANES, LANES), lambda i: (0, 0))],
        out_shape=[jax.ShapeDtypeStruct((N_TOK, D_MODEL), jnp.float32),
                   jax.ShapeDtypeStruct((XS_ROWS * ROW_SUB, LANES), jnp.uint32),
                   jax.ShapeDtypeStruct((SUBLANES, LANES), jnp.float32)],
        scratch_shapes=[pltpu.VMEM((CONV_HALO + MIX_ROWS, LRU_WIDTH), jnp.float32),
                        pltpu.VMEM((POOL_HALO + MIX_ROWS, POOL_WIDTH), jnp.float32),
                        pltpu.VMEM((MIX_ROWS, LRU_WIDTH), jnp.float32),
                        pltpu.VMEM((MIX_ROWS, LRU_WIDTH), jnp.float32),
                        pltpu.VMEM((MIX_ROWS, LRU_WIDTH), jnp.float32),
                        pltpu.VMEM((SUBLANES, LRU_WIDTH), jnp.float32),
                        pltpu.VMEM((SUBLANES, LANES), jnp.float32),
                        pltpu.VMEM((2, MIX_T, BATCH, D_MODEL), jnp.float32),
                        pltpu.VMEM((TOP_K * MIX_ROWS * ROW_SUB, LANES), jnp.uint32),
                        pltpu.VMEM((SUBLANES, MIX_ROWS), jnp.int32),
                        pltpu.SMEM((SUBLANES, MIX_ROWS), jnp.int32),
                        pltpu.SemaphoreType.DMA((2,)),
                        pltpu.SemaphoreType.DMA,
                        pltpu.SemaphoreType.DMA],
        compiler_params=pltpu.CompilerParams(dimension_semantics=("arbitrary",),
                                             vmem_limit_bytes=VMEM_LIMIT),
        name="mixer",
    )(x, *consts)


def _experts_kernel(blk_ref, bexp_ref, nval_ref, ntot_ref, x_ref, wg_ref, wu_ref, wd_ref, y_hbm,
                    wgu_b, wd_b, obuf, addr_v, addr_s, row_sem, idx_sem):
    i = pl.program_id(0)
    slot = i % 2
    prev = 1 - slot
    n_used = ntot_ref[0]
    idx_copy = pltpu.make_async_copy(addr_v, addr_s.at[slot], idx_sem)
    prev_idx_copy = pltpu.make_async_copy(addr_v, addr_s.at[prev], idx_sem)

    def round_copy(s):
        return pltpu.make_async_copy(obuf.at[s], y_hbm.at[pl.ds(0, MOE_BM * DATA_SUB)], row_sem.at[s])

    def scatter_rows(first, count):
        for r in range(first, first + count):
            d = pl.multiple_of(addr_s[prev, 0, r], DATA_SUB)
            pltpu.make_async_copy(obuf.at[prev, pl.ds(r * DATA_SUB, DATA_SUB)],
                                  y_hbm.at[pl.ds(d, DATA_SUB)], row_sem.at[prev]).start(priority=r % 2)

    def dump_rows(s):
        return (Y_DUMP + s * MOE_BM + lax.broadcasted_iota(jnp.int32, (MOE_BM, LANES), 0)) * DATA_SUB

    @pl.when(i == 0)
    def _():
        obuf[1] = jnp.zeros((MOE_BM * DATA_SUB, LANES), jnp.uint32)
        addr_v[...] = jnp.transpose(dump_rows(1))[0:SUBLANES, :]
        prev_idx_copy.start()

    @pl.when((i < n_used) & ((i == 0) | (bexp_ref[i] != bexp_ref[jnp.maximum(i - 1, 0)])))
    def _():
        wgu_b[:, 0:EXPERT_FF] = wg_ref[0].astype(jnp.bfloat16)
        wgu_b[:, EXPERT_FF:] = wu_ref[0].astype(jnp.bfloat16)
        wd_b[...] = wd_ref[0].astype(jnp.bfloat16)

    @pl.when(i < n_used)
    def _():
        prev_idx_copy.wait()
        meta = lax.bitcast_convert_type(x_ref[pl.ds(DATA_SUB, MOE_BM, stride=ROW_SUB), :], jnp.int32)
        row = lax.broadcasted_iota(jnp.int32, (MOE_BM, LANES), 0)
        lane = lax.broadcasted_iota(jnp.int32, (MOE_BM, LANES), 1)
        ret = jnp.where(row < nval_ref[i], meta, dump_rows(slot))
        addr_v[...] = jnp.transpose(jnp.where(lane == 0, ret, 0))[0:SUBLANES, :]
        idx_copy.start()

        batch = MOE_BM // 4
        parts = [_unpack_bf16_pairs(x_ref[pl.ds(c, MOE_BM, stride=ROW_SUB), :]) for c in range(DATA_SUB)]
        x = jnp.concatenate([p[0] for p in parts] + [p[1] for p in parts], axis=-1)
        scatter_rows(0, batch)
        gu = _bdot(x, wgu_b[...])
        scatter_rows(batch, batch)
        g = gu[:, :EXPERT_FF]
        hid = (g * _sigmoid(g)) * gu[:, EXPERT_FF:]
        scatter_rows(2 * batch, batch)
        gate = lax.bitcast_convert_type(meta[:, 1:2], jnp.float32)
        o = _bdot(hid, wd_b[...]) * gate
        scatter_rows(3 * batch, batch)
        packed = _pack_bf16_pairs(o[:, :HALF], o[:, HALF:])

        @pl.when(i >= 1)
        def _():
            round_copy(slot).wait()

        for c in range(DATA_SUB):
            obuf[slot, pl.ds(c, MOE_BM, stride=DATA_SUB), :] = packed[:, c * LANES:(c + 1) * LANES]

    @pl.when(i == n_used)
    def _():
        prev_idx_copy.wait()
        scatter_rows(0, MOE_BM)

    @pl.when(i == pl.num_programs(0) - 1)
    def _():
        round_copy(n_used % 2).wait()
        round_copy((n_used - 1) % 2).wait()


def _experts(blk_idx, blk_expert, blk_nvalid, n_used, xs, w_gate, w_up, w_down):
    wmap = lambda i, blk, bexp, nval, ntot: (bexp[i], 0, 0)
    return pl.pallas_call(
        _experts_kernel,
        grid_spec=pltpu.PrefetchScalarGridSpec(
            num_scalar_prefetch=4,
            grid=(N_BLOCKS + 1,),
            in_specs=[pl.BlockSpec((MOE_BM * ROW_SUB, LANES),
                                   lambda i, blk, bexp, nval, ntot: (blk[i], 0)),
                      pl.BlockSpec((1, D_MODEL, EXPERT_FF), wmap),
                      pl.BlockSpec((1, D_MODEL, EXPERT_FF), wmap),
                      pl.BlockSpec((1, EXPERT_FF, D_MODEL), wmap)],
            out_specs=pl.BlockSpec(memory_space=pl.ANY),
            scratch_shapes=[pltpu.VMEM((D_MODEL, 2 * EXPERT_FF), jnp.bfloat16),
                            pltpu.VMEM((EXPERT_FF, D_MODEL), jnp.bfloat16),
                            pltpu.VMEM((2, MOE_BM * DATA_SUB, LANES), jnp.uint32),
                            pltpu.VMEM((SUBLANES, MOE_BM), jnp.int32),
                            pltpu.SMEM((2, SUBLANES, MOE_BM), jnp.int32),
                            pltpu.SemaphoreType.DMA((2,)),
                            pltpu.SemaphoreType.DMA]),
        out_shape=jax.ShapeDtypeStruct((Y_ROWS * DATA_SUB, LANES), jnp.uint32),
        compiler_params=pltpu.CompilerParams(dimension_semantics=("arbitrary",),
                                             vmem_limit_bytes=VMEM_LIMIT),
        name="experts",
    )(blk_idx, blk_expert, blk_nvalid, n_used, xs, w_gate, w_up, w_down)


def _final_kernel(h1_ref, y_ref, gain_ref, out_hbm, obuf, sem):
    i = pl.program_id(0)
    slot = i % 2
    stride = TOP_K * DATA_SUB

    def out_copy(step, s, b):
        return pltpu.make_async_copy(obuf.at[s, :, b, :],
                                     out_hbm.at[b, pl.ds(step * FIN_T, FIN_T), :], sem.at[s])

    lo, hi = [], []
    for c in range(DATA_SUB):
        y0 = _unpack_bf16_pairs(y_ref[pl.ds(c, FIN_TM, stride=stride), :])
        y1 = _unpack_bf16_pairs(y_ref[pl.ds(DATA_SUB + c, FIN_TM, stride=stride), :])
        lo.append(y0[0] + y1[0])
        hi.append(y0[1] + y1[1])
    y = jnp.concatenate(lo + hi, axis=-1)
    res = _rms(h1_ref[...] + y, gain_ref[...])

    @pl.when(i >= 2)
    def _():
        for b in range(BATCH):
            out_copy(i - 2, slot, b).wait()

    obuf[slot] = res.reshape(FIN_T, BATCH, D_MODEL)
    for b in range(BATCH):
        out_copy(i, slot, b).start()

    @pl.when(i == pl.num_programs(0) - 1)
    def _():
        for b in range(BATCH):
            out_copy(i - 1, 1 - slot, b).wait()
        for b in range(BATCH):
            out_copy(i, slot, b).wait()


def _final(h1, y2, final_gain):
    rows = FIN_TM * TOP_K * DATA_SUB
    return pl.pallas_call(
        _final_kernel,
        grid=(N_TOK // FIN_TM,),
        in_specs=[pl.BlockSpec((FIN_TM, D_MODEL), lambda i: (i, 0)),
                  pl.BlockSpec((rows, LANES), lambda i: (i, 0)),
                  pl.BlockSpec((1, D_MODEL), lambda i: (0, 0))],
        out_specs=pl.BlockSpec(memory_space=pl.ANY),
        out_shape=jax.ShapeDtypeStruct((BATCH, SEQ, D_MODEL), jnp.float32),
        scratch_shapes=[pltpu.VMEM((2, FIN_T, BATCH, D_MODEL), jnp.float32),
                        pltpu.SemaphoreType.DMA((2,))],
        compiler_params=pltpu.CompilerParams(dimension_semantics=("arbitrary",),
                                             vmem_limit_bytes=VMEM_LIMIT),
        name="final",
    )(h1, y2, final_gain)


def _block_diag(blocks):
    n, r, c = blocks.shape
    on_diag = jnp.eye(n, dtype=bool)[:, None, :, None]
    return jnp.where(on_diag, blocks[:, :, None, :], 0).reshape(n * r, n * c)


def kernel(x, meta_tokens, norm1_gain, w_in, conv_w, conv_b, lru_wa, lru_ba, lru_wx, lru_bx,
           lru_lambda, lru_out_gain, pool_w, pool_scale, w_out, norm2_gain, w_group, b_group,
           w_router, b_router, w_gate, w_up, w_down, final_gain):
    assert x.shape == (BATCH, SEQ, D_MODEL) and norm1_gain.shape[0] == 1
    f32, bf16 = jnp.float32, jnp.bfloat16
    row = lambda v: v.reshape(1, -1).astype(f32)

    meta_t = jnp.repeat(meta_tokens.astype(f32), BATCH, axis=0)

    heads_per = LRU_HEADS // 2
    w_gate_blk = jnp.stack([
        jnp.concatenate([_block_diag(lru_wa[0, j * heads_per:(j + 1) * heads_per]),
                         _block_diag(lru_wx[0, j * heads_per:(j + 1) * heads_per])],
                        axis=1) for j in range(2)]).astype(bf16)
    b_gate_blk = jnp.stack([lru_ba[0], lru_bx[0]]).astype(f32)
    w_pool_blk = jnp.stack([_block_diag(pool_w[0, 2 * j:2 * j + 2])
                            for j in range(2)]).astype(bf16)
    lam8 = row(LRU_C * jax.nn.log_sigmoid(lru_lambda[0].astype(f32)))
    pad = LANES - N_EXPERTS - N_GROUPS
    w_rt = jnp.concatenate([w_router[0], w_group[0], jnp.zeros((D_MODEL, pad), f32)], axis=1).astype(bf16)
    b_rt = jnp.concatenate([b_router[0], b_group[0], jnp.zeros((pad,), f32)]).reshape(1, LANES).astype(f32)
    ridx = jnp.arange(MIX_ROWS)
    tri = (ridx[None, :] < ridx[:, None]).astype(bf16)

    h1, xs, cnt = _mixer(
        x, meta_t, row(norm1_gain[0]), w_in[0].astype(bf16), conv_w[0].astype(f32),
        row(conv_b[0]), w_gate_blk, b_gate_blk, lam8, row(lru_out_gain[0]), w_pool_blk,
        row(pool_scale[0]), w_out[0].astype(bf16), row(norm2_gain[0]), w_rt, b_rt, tri)

    counts = cnt[0, :N_EXPERTS].astype(jnp.int32)
    nblk = (counts + MOE_BM - 1) // MOE_BM
    blk_end = jnp.cumsum(nblk)
    step = jnp.arange(N_BLOCKS + 1, dtype=jnp.int32)
    used = step < blk_end[-1]
    step_c = jnp.minimum(step, blk_end[-1] - 1)
    e_of = jnp.sum(blk_end[None, :] <= step_c[:, None], axis=-1).astype(jnp.int32)
    onehot_e = e_of[:, None] == jnp.arange(N_EXPERTS, dtype=jnp.int32)[None, :]
    j_of = step_c - jnp.sum(jnp.where(onehot_e, (blk_end - nblk)[None, :], 0), axis=-1)
    cnt_of = jnp.sum(jnp.where(onehot_e, counts[None, :], 0), axis=-1)
    blk_idx = (e_of * CAP_BLOCKS + j_of).astype(jnp.int32)
    blk_nvalid = jnp.where(used, jnp.clip(cnt_of - j_of * MOE_BM, 0, MOE_BM), 0).astype(jnp.int32)

    y2 = _experts(blk_idx, e_of, blk_nvalid, blk_end[-1:].astype(jnp.int32), xs,
                  w_gate[0], w_up[0], w_down[0])
    return _final(h1, y2, row(final_gain))
```

```python
import functools

import jax
import jax.numpy as jnp
from jax import lax
from jax.experimental import pallas as pl
from jax.experimental.pallas import tpu as pltpu

D_MODEL = 1024
BATCH = 8
SEQ = 4096
N_META = 16
LRU_WIDTH = 512
LRU_HEADS = 8
LRU_HEAD_DIM = 64
CONV_WIDTH = 4
LRU_C = 8.0
POOL_WIDTH = 512
POOL_WINDOWS = (2, 4, 8, 16)
POOL_GROUP = 128
N_GROUPS = 4
EXPERTS_PER_GROUP = 8
N_EXPERTS = 32
TOP_K = 2
EXPERT_FF = 512
RMS_EPS = 1e-6
SQRT_FLOOR = 1e-37

N_TOK = BATCH * SEQ
SUBLANES = 8
LANES = 128
MIX_ROWS = 512
META_ROWS = N_META * BATCH
CONV_HALO = (CONV_WIDTH - 1) * BATCH
POOL_HALO = (max(POOL_WINDOWS) - 1) * BATCH
MIX_STEPS = N_TOK // MIX_ROWS
MIX_T = MIX_ROWS // BATCH
MOE_BM = 512
N_ASSIGN = N_TOK * TOP_K
N_BLOCKS = N_ASSIGN // MOE_BM + N_EXPERTS
EXPERT_CAP = N_TOK
CAP_BLOCKS = EXPERT_CAP // MOE_BM
ROW_SUB = 8
DATA_SUB = 4
HALF = D_MODEL // 2
XS_DUMP = N_EXPERTS * EXPERT_CAP
XS_ROWS = XS_DUMP + TOP_K * MIX_ROWS
SCATTER_BATCH = TOP_K * MIX_ROWS // 8
Y_DUMP = N_ASSIGN
Y_ROWS = Y_DUMP + 2 * MOE_BM
FIN_TM = 1024
FIN_T = FIN_TM // BATCH
VMEM_LIMIT = 56 * 1024 * 1024


def _rms(x, gain):
    return x * lax.rsqrt(jnp.mean(x * x, axis=-1, keepdims=True) + RMS_EPS) * gain


def _sigmoid(x):
    return 0.5 * jnp.tanh(0.5 * x) + 0.5


def _bdot(a, b):
    return jnp.dot(a.astype(jnp.bfloat16), b, preferred_element_type=jnp.float32)


def _pack_bf16_pairs(lo, hi):
    lo_b = lax.bitcast_convert_type(lo.astype(jnp.bfloat16).astype(jnp.float32), jnp.uint32)
    hi_b = lax.bitcast_convert_type(hi.astype(jnp.bfloat16).astype(jnp.float32), jnp.uint32)
    return (lo_b >> 16) | (hi_b & jnp.uint32(0xFFFF0000))


def _unpack_bf16_pairs(u):
    lo = lax.bitcast_convert_type(u << 16, jnp.float32)
    hi = lax.bitcast_convert_type(u & jnp.uint32(0xFFFF0000), jnp.float32)
    return lo, hi


def _mix_rows(x, nrows, with_output, g1_ref, w_in_ref, conv_w_ref, conv_b_ref, w_gate_ref,
              b_gate_ref, lam8_ref, lru_gain_ref, w_pool_ref, pool_scale_ref, w_out_ref,
              ux_buf, up_buf, a_buf, b_buf, hs_buf, hstate, between=lambda: None):
    hn = _rms(x, g1_ref[...])
    proj = _bdot(hn, w_in_ref[...])
    between()
    ux = proj[:, :LRU_WIDTH]
    up = proj[:, 2 * LRU_WIDTH:]

    ux_buf[CONV_HALO:CONV_HALO + nrows, :] = ux
    xc = conv_b_ref[...] + conv_w_ref[3:4, :] * ux
    for k in range(CONV_WIDTH - 1):
        xc = xc + conv_w_ref[k:k + 1, :] * ux_buf[k * BATCH:k * BATCH + nrows, :]
    ux_buf[0:CONV_HALO, :] = ux_buf[nrows:nrows + CONV_HALO, :]
    between()

    half = LRU_WIDTH // 2
    z = [_bdot(xc[:, j * half:(j + 1) * half], w_gate_ref[j]) for j in range(2)]
    za = jnp.concatenate([z[0][:, :half], z[1][:, :half]], axis=-1) + b_gate_ref[0:1, :]
    zx = jnp.concatenate([z[0][:, half:], z[1][:, half:]], axis=-1) + b_gate_ref[1:2, :]
    r = _sigmoid(za)
    gi = _sigmoid(zx)
    log_a = lam8_ref[...] * r
    a = jnp.exp(log_a)
    a_buf[0:nrows, :] = a
    v = jnp.tanh(-log_a) * (1.0 + a * a)
    root = v * lax.rsqrt(jnp.maximum(v, SQRT_FLOOR))
    b_buf[0:nrows, :] = root * (gi * xc)
    between()

    def step(t, h):
        r0 = pl.multiple_of(t * SUBLANES, SUBLANES)
        h = a_buf[pl.ds(r0, SUBLANES), :] * h + b_buf[pl.ds(r0, SUBLANES), :]
        hs_buf[pl.ds(r0, SUBLANES), :] = h
        return h

    hstate[...] = lax.fori_loop(0, nrows // SUBLANES, step, hstate[...], unroll=True)

    up_buf[POOL_HALO:POOL_HALO + nrows, :] = up
    if not with_output:
        up_buf[0:POOL_HALO, :] = up_buf[nrows:nrows + POOL_HALO, :]
        return None

    ug = proj[:, LRU_WIDTH:2 * LRU_WIDTH]
    y_lru = _rms(hs_buf[0:nrows, :] * jax.nn.gelu(ug), lru_gain_ref[...])

    pm = []
    for g, w in enumerate(POOL_WINDOWS):
        lo, hi = g * POOL_GROUP, (g + 1) * POOL_GROUP
        s = up_buf[POOL_HALO - (w - 1) * BATCH:POOL_HALO + nrows, lo:hi]
        m = 1
        while m < w:
            s = s[m * BATCH:, :] + s[:-m * BATCH, :]
            m *= 2
        pm.append(s * (1.0 / w) - up[:, lo:hi])
    up_buf[0:POOL_HALO, :] = up_buf[nrows:nrows + POOL_HALO, :]
    yp = [_bdot(jnp.concatenate(pm[2 * j:2 * j + 2], axis=-1), w_pool_ref[j]) for j in range(2)]
    y_pool = _rms(jnp.concatenate(yp, axis=-1), pool_scale_ref[...])
    between()

    y = _bdot(jnp.concatenate([y_lru, y_pool], axis=-1), w_out_ref[...])
    between()
    return x + y


def _mixer_kernel(x_hbm, meta_ref, g1_ref, w_in_ref, conv_w_ref, conv_b_ref, w_gate_ref,
                  b_gate_ref, lam8_ref, lru_gain_ref, w_pool_ref, pool_scale_ref, w_out_ref,
                  g2_ref, w_rt_ref, b_rt_ref, tri_ref,
                  h1_ref, xs_hbm, cnt_ref,
                  ux_buf, up_buf, a_buf, b_buf, hs_buf, hstate, running,
                  xin, rowbuf, dest_v, dest_s, in_sem, row_sem, idx_sem):
    i = pl.program_id(0)
    slot = i % 2
    n_rows = TOP_K * MIX_ROWS
    mix = functools.partial(
        _mix_rows, g1_ref=g1_ref, w_in_ref=w_in_ref, conv_w_ref=conv_w_ref,
        conv_b_ref=conv_b_ref, w_gate_ref=w_gate_ref, b_gate_ref=b_gate_ref, lam8_ref=lam8_ref,
        lru_gain_ref=lru_gain_ref, w_pool_ref=w_pool_ref, pool_scale_ref=pool_scale_ref,
        w_out_ref=w_out_ref, ux_buf=ux_buf, up_buf=up_buf, a_buf=a_buf, b_buf=b_buf,
        hs_buf=hs_buf, hstate=hstate)
    idx_copy = pltpu.make_async_copy(dest_v, dest_s, idx_sem)
    round_copy = pltpu.make_async_copy(rowbuf, xs_hbm.at[pl.ds(0, n_rows * ROW_SUB)], row_sem)

    def x_copy(step, s, b):
        return pltpu.make_async_copy(x_hbm.at[b, pl.ds(step * MIX_T, MIX_T), :],
                                     xin.at[s, :, b, :], in_sem.at[s])

    def scatter_rows(first, count):
        for n in range(first, first + count):
            k, r = divmod(n, MIX_ROWS)
            d = pl.multiple_of(dest_s[k, r], ROW_SUB)
            pltpu.make_async_copy(rowbuf.at[pl.ds(n * ROW_SUB, ROW_SUB)],
                                  xs_hbm.at[pl.ds(d, ROW_SUB)], row_sem).start(priority=n % 2)

    issued = [0]

    def scatter_batch():
        scatter_rows(issued[0], SCATTER_BATCH)
        issued[0] += SCATTER_BATCH

    @pl.when(i == 0)
    def _():
        for b in range(BATCH):
            x_copy(0, 0, b).start()
        ux_buf[0:CONV_HALO, :] = jnp.zeros((CONV_HALO, LRU_WIDTH), jnp.float32)
        up_buf[0:POOL_HALO, :] = jnp.zeros((POOL_HALO, POOL_WIDTH), jnp.float32)
        hstate[...] = jnp.zeros_like(hstate)
        running[...] = jnp.zeros_like(running)
        mix(meta_ref[...], META_ROWS, False)
        rowbuf[...] = jnp.zeros_like(rowbuf)
        col = lax.broadcasted_iota(jnp.int32, (SUBLANES, MIX_ROWS), 1)
        sub = lax.broadcasted_iota(jnp.int32, (SUBLANES, MIX_ROWS), 0)
        dest_v[...] = (XS_DUMP + jnp.minimum(sub, TOP_K - 1) * MIX_ROWS + col) * ROW_SUB
        idx_copy.start()

    @pl.when(i + 1 < MIX_STEPS)
    def _():
        for b in range(BATCH):
            x_copy(i + 1, 1 - slot, b).start()

    for b in range(BATCH):
        x_copy(i, slot, b).wait()
    x = xin[slot].reshape(MIX_ROWS, D_MODEL)
    idx_copy.wait()
    h1 = mix(x, MIX_ROWS, True, between=scatter_batch)
    h1_ref[...] = h1

    hn2 = _rms(h1, g2_ref[...])
    logits = _bdot(hn2, w_rt_ref[...]) + b_rt_ref[...]
    scatter_batch()
    lane = lax.broadcasted_iota(jnp.int32, (MIX_ROWS, LANES), 1)
    lane_f = lane.astype(jnp.float32)
    neg = jnp.float32(-jnp.inf)
    big = jnp.float32(4 * LANES)

    is_g = (lane >= N_EXPERTS) & (lane < N_EXPERTS + N_GROUPS)
    gl = jnp.where(is_g, logits, neg)
    gmax = jnp.max(gl, axis=-1, keepdims=True)
    gidx = jnp.min(jnp.where(gl == gmax, lane_f, big), axis=-1, keepdims=True)
    p_g = 1.0 / jnp.sum(jnp.exp(gl - gmax), axis=-1, keepdims=True)
    grp = gidx.astype(jnp.int32) - N_EXPERTS

    el = jnp.where((lane >> 3) == grp, logits, neg)
    m1 = jnp.max(el, axis=-1, keepdims=True)
    i1 = jnp.min(jnp.where(el == m1, lane_f, big), axis=-1, keepdims=True)
    el2 = jnp.where(lane_f == i1, neg, el)
    m2 = jnp.max(el2, axis=-1, keepdims=True)
    i2 = jnp.min(jnp.where(el2 == m2, lane_f, big), axis=-1, keepdims=True)
    e21 = jnp.exp(m2 - m1)
    den = 1.0 + e21
    gate = (p_g * (1.0 / den), p_g * (e21 / den))
    scatter_batch()

    sel = (lane_f == i1, lane_f == i2)
    onehot = jnp.where(sel[0] | sel[1], 1.0, 0.0)
    before = jnp.dot(tri_ref[...], onehot.astype(jnp.bfloat16),
                     preferred_element_type=jnp.float32) + running[0:1, :]
    rank = [jnp.sum(jnp.where(sel[k], before, 0.0), axis=-1, keepdims=True) for k in range(TOP_K)]
    running[0:1, :] = running[0:1, :] + jnp.sum(onehot, axis=0, keepdims=True)
    cnt_ref[...] = running[...]
    scatter_batch()
    assert issued[0] == n_rows

    eid = (i1.astype(jnp.int32), i2.astype(jnp.int32))
    dst = [(eid[k] * EXPERT_CAP + rank[k].astype(jnp.int32)) * ROW_SUB for k in range(TOP_K)]
    dst_tile = jnp.where(lane == 0, dst[0], jnp.where(lane == 1, dst[1], 0))
    dest_v[...] = jnp.transpose(dst_tile)[0:SUBLANES, :]
    idx_copy.start()

    data = _pack_bf16_pairs(hn2[:, :HALF], hn2[:, HALF:])
    tok = i * MIX_ROWS + lax.broadcasted_iota(jnp.int32, (MIX_ROWS, LANES), 0)

    round_copy.wait()
    for k in range(TOP_K):
        base = k * MIX_ROWS * ROW_SUB
        for c in range(DATA_SUB):
            rowbuf[pl.ds(base + c, MIX_ROWS, stride=ROW_SUB), :] = data[:, c * LANES:(c + 1) * LANES]
        ret = (tok * TOP_K + k) * DATA_SUB
        gbits = lax.bitcast_convert_type(jnp.broadcast_to(gate[k], (MIX_ROWS, LANES)), jnp.int32)
        meta = jnp.where(lane == 0, ret, jnp.where(lane == 1, gbits, 0))
        rowbuf[pl.ds(base + DATA_SUB, MIX_ROWS, stride=ROW_SUB), :] = lax.bitcast_convert_type(
            meta, jnp.uint32)

    @pl.when(i == MIX_STEPS - 1)
    def _():
        idx_copy.wait()
        scatter_rows(0, n_rows)
        round_copy.wait()


def _mixer(x, meta_t, g1, w_in, conv_w, conv_b, w_gate, b_gate, lam8, lru_gain, w_pool,
           pool_scale, w_out, g2, w_rt, b_rt, tri):
    full = lambda a: pl.BlockSpec(a.shape, lambda i: (0,) * a.ndim)
    consts = (meta_t, g1, w_in, conv_w, conv_b, w_gate, b_gate, lam8, lru_gain, w_pool,
              pool_scale, w_out, g2, w_rt, b_rt, tri)
    return pl.pallas_call(
        _mixer_kernel,
        grid=(MIX_STEPS,),
        in_specs=[pl.BlockSpec(memory_space=pl.ANY)] + [full(a) for a in consts],
        out_specs=[pl.BlockSpec((MIX_ROWS, D_MODEL), lambda i: (i, 0)),
                   pl.BlockSpec(memory_space=pl.ANY),
                   pl.BlockSpec((SUBLANES, LANES), lambda i: (0, 0))],
        out_shape=[jax.ShapeDtypeStruct((N_TOK, D_MODEL), jnp.float32),
                   jax.ShapeDtypeStruct((XS_ROWS * ROW_SUB, LANES), jnp.uint32),
                   jax.ShapeDtypeStruct((SUBLANES, LANES), jnp.float32)],
        scratch_shapes=[pltpu.VMEM((CONV_HALO + MIX_ROWS, LRU_WIDTH), jnp.float32),
                        pltpu.VMEM((POOL_HALO + MIX_ROWS, POOL_WIDTH), jnp.float32),
                        pltpu.VMEM((MIX_ROWS, LRU_WIDTH), jnp.float32),
                        pltpu.VMEM((MIX_ROWS, LRU_WIDTH), jnp.float32),
                        pltpu.VMEM((MIX_ROWS, LRU_WIDTH), jnp.float32),
                        pltpu.VMEM((SUBLANES, LRU_WIDTH), jnp.float32),
                        pltpu.VMEM((SUBLANES, LANES), jnp.float32),
                        pltpu.VMEM((2, MIX_T, BATCH, D_MODEL), jnp.float32),
                        pltpu.VMEM((TOP_K * MIX_ROWS * ROW_SUB, LANES), jnp.uint32),
                        pltpu.VMEM((SUBLANES, MIX_ROWS), jnp.int32),
                        pltpu.SMEM((SUBLANES, MIX_ROWS), jnp.int32),
                        pltpu.SemaphoreType.DMA((2,)),
                        pltpu.SemaphoreType.DMA,
                        pltpu.SemaphoreType.DMA],
        compiler_params=pltpu.CompilerParams(dimension_semantics=("arbitrary",),
                                             vmem_limit_bytes=VMEM_LIMIT),
        name="mixer",
    )(x, *consts)


def _experts_kernel(blk_ref, bexp_ref, nval_ref, ntot_ref, x_ref, wg_ref, wu_ref, wd_ref, y_hbm,
                    wgu_b, wd_b, obuf, addr_v, addr_s, row_sem, idx_sem):
    i = pl.program_id(0)
    slot = i % 2
    prev = 1 - slot
    n_used = ntot_ref[0]
    idx_copy = pltpu.make_async_copy(addr_v, addr_s.at[slot], idx_sem)
    prev_idx_copy = pltpu.make_async_copy(addr_v, addr_s.at[prev], idx_sem)

    def round_copy(s):
        return pltpu.make_async_copy(obuf.at[s], y_hbm.at[pl.ds(0, MOE_BM * DATA_SUB)], row_sem.at[s])

    def scatter_rows(first, count):
        for r in range(first, first + count):
            d = pl.multiple_of(addr_s[prev, 0, r], DATA_SUB)
            pltpu.make_async_copy(obuf.at[prev, pl.ds(r * DATA_SUB, DATA_SUB)],
                                  y_hbm.at[pl.ds(d, DATA_SUB)], row_sem.at[prev]).start(priority=r % 2)

    def dump_rows(s):
        return (Y_DUMP + s * MOE_BM + lax.broadcasted_iota(jnp.int32, (MOE_BM, LANES), 0)) * DATA_SUB

    @pl.when(i == 0)
    def _():
        obuf[...] = jnp.zeros_like(obuf)
        addr_v[...] = jnp.transpose(dump_rows(1))[0:SUBLANES, :]
        prev_idx_copy.start()

    @pl.when((i < n_used) & ((i == 0) | (bexp_ref[i] != bexp_ref[jnp.maximum(i - 1, 0)])))
    def _():
        wgu_b[:, 0:EXPERT_FF] = wg_ref[0].astype(jnp.bfloat16)
        wgu_b[:, EXPERT_FF:] = wu_ref[0].astype(jnp.bfloat16)
        wd_b[...] = wd_ref[0].astype(jnp.bfloat16)

    @pl.when(i < n_used)
    def _():
        prev_idx_copy.wait()
        meta = lax.bitcast_convert_type(x_ref[pl.ds(DATA_SUB, MOE_BM, stride=ROW_SUB), :], jnp.int32)
        row = lax.broadcasted_iota(jnp.int32, (MOE_BM, LANES), 0)
        lane = lax.broadcasted_iota(jnp.int32, (MOE_BM, LANES), 1)
        ret = jnp.where(row < nval_ref[i], meta, dump_rows(slot))
        addr_v[...] = jnp.transpose(jnp.where(lane == 0, ret, 0))[0:SUBLANES, :]
        idx_copy.start()

        def compute(nrows):
            batch = MOE_BM // 4
            parts = [_unpack_bf16_pairs(x_ref[pl.ds(c, nrows, stride=ROW_SUB), :]) for c in range(DATA_SUB)]
            x = jnp.concatenate([p[0] for p in parts] + [p[1] for p in parts], axis=-1)
            scatter_rows(0, batch)
            gu = _bdot(x, wgu_b[...])
            scatter_rows(batch, batch)
            g = gu[:, :EXPERT_FF]
            hid = (g * _sigmoid(g)) * gu[:, EXPERT_FF:]
            scatter_rows(2 * batch, batch)
            gate = lax.bitcast_convert_type(meta[0:nrows, 1:2], jnp.float32)
            o = _bdot(hid, wd_b[...]) * gate
            scatter_rows(3 * batch, batch)
            packed = _pack_bf16_pairs(o[:, :HALF], o[:, HALF:])

            @pl.when(i >= 1)
            def _():
                round_copy(slot).wait()

            for c in range(DATA_SUB):
                obuf[slot, pl.ds(c, nrows, stride=DATA_SUB), :] = packed[:, c * LANES:(c + 1) * LANES]

        half_full = nval_ref[i] <= MOE_BM // 2

        @pl.when(half_full)
        def _():
            compute(MOE_BM // 2)

        @pl.when(jnp.logical_not(half_full))
        def _():
            compute(MOE_BM)

    @pl.when(i == n_used)
    def _():
        prev_idx_copy.wait()
        scatter_rows(0, MOE_BM)

    @pl.when(i == pl.num_programs(0) - 1)
    def _():
        round_copy(n_used % 2).wait()
        round_copy((n_used - 1) % 2).wait()


def _experts(blk_idx, blk_expert, blk_nvalid, n_used, xs, w_gate, w_up, w_down):
    wmap = lambda i, blk, bexp, nval, ntot: (bexp[i], 0, 0)
    return pl.pallas_call(
        _experts_kernel,
        grid_spec=pltpu.PrefetchScalarGridSpec(
            num_scalar_prefetch=4,
            grid=(N_BLOCKS + 1,),
            in_specs=[pl.BlockSpec((MOE_BM * ROW_SUB, LANES),
                                   lambda i, blk, bexp, nval, ntot: (blk[i], 0)),
                      pl.BlockSpec((1, D_MODEL, EXPERT_FF), wmap),
                      pl.BlockSpec((1, D_MODEL, EXPERT_FF), wmap),
                      pl.BlockSpec((1, EXPERT_FF, D_MODEL), wmap)],
            out_specs=pl.BlockSpec(memory_space=pl.ANY),
            scratch_shapes=[pltpu.VMEM((D_MODEL, 2 * EXPERT_FF), jnp.bfloat16),
                            pltpu.VMEM((EXPERT_FF, D_MODEL), jnp.bfloat16),
                            pltpu.VMEM((2, MOE_BM * DATA_SUB, LANES), jnp.uint32),
                            pltpu.VMEM((SUBLANES, MOE_BM), jnp.int32),
                            pltpu.SMEM((2, SUBLANES, MOE_BM), jnp.int32),
                            pltpu.SemaphoreType.DMA((2,)),
                            pltpu.SemaphoreType.DMA]),
        out_shape=jax.ShapeDtypeStruct((Y_ROWS * DATA_SUB, LANES), jnp.uint32),
        compiler_params=pltpu.CompilerParams(dimension_semantics=("arbitrary",),
                                             vmem_limit_bytes=VMEM_LIMIT),
        name="experts",
    )(blk_idx, blk_expert, blk_nvalid, n_used, xs, w_gate, w_up, w_down)


def _final_kernel(h1_ref, y_ref, gain_ref, out_hbm, obuf, sem):
    i = pl.program_id(0)
    slot = i % 2
    stride = TOP_K * DATA_SUB

    def out_copy(step, s, b):
        return pltpu.make_async_copy(obuf.at[s, :, b, :],
                                     out_hbm.at[b, pl.ds(step * FIN_T, FIN_T), :], sem.at[s])

    lo, hi = [], []
    for c in range(DATA_SUB):
        y0 = _unpack_bf16_pairs(y_ref[pl.ds(c, FIN_TM, stride=stride), :])
        y1 = _unpack_bf16_pairs(y_ref[pl.ds(DATA_SUB + c, FIN_TM, stride=stride), :])
        lo.append(y0[0] + y1[0])
        hi.append(y0[1] + y1[1])
    y = jnp.concatenate(lo + hi, axis=-1)
    res = _rms(h1_ref[...] + y, gain_ref[...])

    @pl.when(i >= 2)
    def _():
        for b in range(BATCH):
            out_copy(i - 2, slot, b).wait()

    obuf[slot] = res.reshape(FIN_T, BATCH, D_MODEL)
    for b in range(BATCH):
        out_copy(i, slot, b).start()

    @pl.when(i == pl.num_programs(0) - 1)
    def _():
        for b in range(BATCH):
            out_copy(i - 1, 1 - slot, b).wait()
        for b in range(BATCH):
            out_copy(i, slot, b).wait()


def _final(h1, y2, final_gain):
    rows = FIN_TM * TOP_K * DATA_SUB
    return pl.pallas_call(
        _final_kernel,
        grid=(N_TOK // FIN_TM,),
        in_specs=[pl.BlockSpec((FIN_TM, D_MODEL), lambda i: (i, 0)),
                  pl.BlockSpec((rows, LANES), lambda i: (i, 0)),
                  pl.BlockSpec((1, D_MODEL), lambda i: (0, 0))],
        out_specs=pl.BlockSpec(memory_space=pl.ANY),
        out_shape=jax.ShapeDtypeStruct((BATCH, SEQ, D_MODEL), jnp.float32),
        scratch_shapes=[pltpu.VMEM((2, FIN_T, BATCH, D_MODEL), jnp.float32),
                        pltpu.SemaphoreType.DMA((2,))],
        compiler_params=pltpu.CompilerParams(dimension_semantics=("arbitrary",),
                                             vmem_limit_bytes=VMEM_LIMIT),
        name="final",
    )(h1, y2, final_gain)


def _block_diag(blocks):
    n, r, c = blocks.shape
    on_diag = jnp.eye(n, dtype=bool)[:, None, :, None]
    return jnp.where(on_diag, blocks[:, :, None, :], 0).reshape(n * r, n * c)


def kernel(x, meta_tokens, norm1_gain, w_in, conv_w, conv_b, lru_wa, lru_ba, lru_wx, lru_bx,
           lru_lambda, lru_out_gain, pool_w, pool_scale, w_out, norm2_gain, w_group, b_group,
           w_router, b_router, w_gate, w_up, w_down, final_gain):
    assert x.shape == (BATCH, SEQ, D_MODEL) and norm1_gain.shape[0] == 1
    f32, bf16 = jnp.float32, jnp.bfloat16
    row = lambda v: v.reshape(1, -1).astype(f32)

    meta_t = jnp.repeat(meta_tokens.astype(f32), BATCH, axis=0)

    heads_per = LRU_HEADS // 2
    w_gate_blk = jnp.stack([
        jnp.concatenate([_block_diag(lru_wa[0, j * heads_per:(j + 1) * heads_per]),
                         _block_diag(lru_wx[0, j * heads_per:(j + 1) * heads_per])],
                        axis=1) for j in range(2)]).astype(bf16)
    b_gate_blk = jnp.stack([lru_ba[0], lru_bx[0]]).astype(f32)
    w_pool_blk = jnp.stack([_block_diag(pool_w[0, 2 * j:2 * j + 2])
                            for j in range(2)]).astype(bf16)
    lam8 = row(LRU_C * jax.nn.log_sigmoid(lru_lambda[0].astype(f32)))
    pad = LANES - N_EXPERTS - N_GROUPS
    w_rt = jnp.concatenate([w_router[0], w_group[0], jnp.zeros((D_MODEL, pad), f32)], axis=1).astype(bf16)
    b_rt = jnp.concatenate([b_router[0], b_group[0], jnp.zeros((pad,), f32)]).reshape(1, LANES).astype(f32)
    ridx = jnp.arange(MIX_ROWS)
    tri = (ridx[None, :] < ridx[:, None]).astype(bf16)

    h1, xs, cnt = _mixer(
        x, meta_t, row(norm1_gain[0]), w_in[0].astype(bf16), conv_w[0].astype(f32),
        row(conv_b[0]), w_gate_blk, b_gate_blk, lam8, row(lru_out_gain[0]), w_pool_blk,
        row(pool_scale[0]), w_out[0].astype(bf16), row(norm2_gain[0]), w_rt, b_rt, tri)

    counts = cnt[0, :N_EXPERTS].astype(jnp.int32)
    nblk = (counts + MOE_BM - 1) // MOE_BM
    blk_end = jnp.cumsum(nblk)
    step = jnp.arange(N_BLOCKS + 1, dtype=jnp.int32)
    used = step < blk_end[-1]
    step_c = jnp.minimum(step, blk_end[-1] - 1)
    e_of = jnp.sum(blk_end[None, :] <= step_c[:, None], axis=-1).astype(jnp.int32)
    onehot_e = e_of[:, None] == jnp.arange(N_EXPERTS, dtype=jnp.int32)[None, :]
    j_of = step_c - jnp.sum(jnp.where(onehot_e, (blk_end - nblk)[None, :], 0), axis=-1)
    cnt_of = jnp.sum(jnp.where(onehot_e, counts[None, :], 0), axis=-1)
    blk_idx = (e_of * CAP_BLOCKS + j_of).astype(jnp.int32)
    blk_nvalid = jnp.where(used, jnp.clip(cnt_of - j_of * MOE_BM, 0, MOE_BM), 0).astype(jnp.int32)

    y2 = _experts(blk_idx, e_of, blk_nvalid, blk_end[-1:].astype(jnp.int32), xs,
                  w_gate[0], w_up[0], w_down[0])
    return _final(h1, y2, row(final_gain))
```

```python
import functools

import jax
import jax.numpy as jnp
from jax import lax
from jax.experimental import pallas as pl
from jax.experimental.pallas import tpu as pltpu

D_MODEL = 1024
BATCH = 8
SEQ = 4096
N_META = 16
LRU_WIDTH = 512
LRU_HEADS = 8
LRU_HEAD_DIM = 64
CONV_WIDTH = 4
LRU_C = 8.0
POOL_WIDTH = 512
POOL_WINDOWS = (2, 4, 8, 16)
POOL_GROUP = 128
N_GROUPS = 4
EXPERTS_PER_GROUP = 8
N_EXPERTS = 32
TOP_K = 2
EXPERT_FF = 512
RMS_EPS = 1e-6
SQRT_FLOOR = 1e-37

N_TOK = BATCH * SEQ
SUBLANES = 8
LANES = 128
MIX_ROWS = 512
META_ROWS = N_META * BATCH
CONV_HALO = (CONV_WIDTH - 1) * BATCH
POOL_HALO = (max(POOL_WINDOWS) - 1) * BATCH
MIX_STEPS = N_TOK // MIX_ROWS
MIX_T = MIX_ROWS // BATCH
MOE_BM = 512
N_ASSIGN = N_TOK * TOP_K
N_BLOCKS = N_ASSIGN // MOE_BM + N_EXPERTS
EXPERT_CAP = N_TOK
CAP_BLOCKS = EXPERT_CAP // MOE_BM
ROW_SUB = 8
DATA_SUB = 4
HALF = D_MODEL // 2
XS_DUMP = N_EXPERTS * EXPERT_CAP
XS_ROWS = XS_DUMP + TOP_K * MIX_ROWS
SCATTER_BATCH = TOP_K * MIX_ROWS // 8
Y_DUMP = N_ASSIGN
Y_ROWS = Y_DUMP + 2 * MOE_BM
FIN_TM = 1024
FIN_T = FIN_TM // BATCH
VMEM_LIMIT = 56 * 1024 * 1024


def _rms(x, gain):
    return x * lax.rsqrt(jnp.mean(x * x, axis=-1, keepdims=True) + RMS_EPS) * gain


def _sigmoid(x):
    return 0.5 * jnp.tanh(0.5 * x) + 0.5


def _bdot(a, b):
    return jnp.dot(a.astype(jnp.bfloat16), b, preferred_element_type=jnp.float32)


def _pack_bf16_pairs(lo, hi):
    lo_b = lax.bitcast_convert_type(lo.astype(jnp.bfloat16).astype(jnp.float32), jnp.uint32)
    hi_b = lax.bitcast_convert_type(hi.astype(jnp.bfloat16).astype(jnp.float32), jnp.uint32)
    return (lo_b >> 16) | (hi_b & jnp.uint32(0xFFFF0000))


def _unpack_bf16_pairs(u):
    lo = lax.bitcast_convert_type(u << 16, jnp.float32)
    hi = lax.bitcast_convert_type(u & jnp.uint32(0xFFFF0000), jnp.float32)
    return lo, hi


def _mix_rows(x, nrows, with_output, g1_ref, w_in_ref, conv_w_ref, conv_b_ref, w_gate_ref,
              b_gate_ref, lam8_ref, lru_gain_ref, w_pool_ref, pool_scale_ref, w_out_ref,
              ux_buf, up_buf, a_buf, b_buf, hs_buf, hstate, between=lambda: None):
    hn = _rms(x, g1_ref[...])
    proj = _bdot(hn, w_in_ref[...])
    between()
    ux = proj[:, :LRU_WIDTH]
    up = proj[:, 2 * LRU_WIDTH:]

    ux_buf[CONV_HALO:CONV_HALO + nrows, :] = ux
    xc = conv_b_ref[...] + conv_w_ref[3:4, :] * ux
    for k in range(CONV_WIDTH - 1):
        xc = xc + conv_w_ref[k:k + 1, :] * ux_buf[k * BATCH:k * BATCH + nrows, :]
    ux_buf[0:CONV_HALO, :] = ux_buf[nrows:nrows + CONV_HALO, :]
    between()

    half = LRU_WIDTH // 2
    z = [_bdot(xc[:, j * half:(j + 1) * half], w_gate_ref[j]) for j in range(2)]
    za = jnp.concatenate([z[0][:, :half], z[1][:, :half]], axis=-1) + b_gate_ref[0:1, :]
    zx = jnp.concatenate([z[0][:, half:], z[1][:, half:]], axis=-1) + b_gate_ref[1:2, :]
    r = _sigmoid(za)
    gi = _sigmoid(zx)
    log_a = lam8_ref[...] * r
    a = jnp.exp(log_a)
    a_buf[0:nrows, :] = a
    v = jnp.tanh(-log_a) * (1.0 + a * a)
    root = v * lax.rsqrt(jnp.maximum(v, SQRT_FLOOR))
    b_buf[0:nrows, :] = root * (gi * xc)
    between()

    def step(t, h):
        r0 = pl.multiple_of(t * SUBLANES, SUBLANES)
        h = a_buf[pl.ds(r0, SUBLANES), :] * h + b_buf[pl.ds(r0, SUBLANES), :]
        hs_buf[pl.ds(r0, SUBLANES), :] = h
        return h

    hstate[...] = lax.fori_loop(0, nrows // SUBLANES, step, hstate[...], unroll=True)

    up_buf[POOL_HALO:POOL_HALO + nrows, :] = up
    if not with_output:
        up_buf[0:POOL_HALO, :] = up_buf[nrows:nrows + POOL_HALO, :]
        return None

    ug = proj[:, LRU_WIDTH:2 * LRU_WIDTH]
    y_lru = _rms(hs_buf[0:nrows, :] * jax.nn.gelu(ug), lru_gain_ref[...])

    pm = []
    for g, w in enumerate(POOL_WINDOWS):
        lo, hi = g * POOL_GROUP, (g + 1) * POOL_GROUP
        s = up_buf[POOL_HALO - (w - 1) * BATCH:POOL_HALO + nrows, lo:hi]
        m = 1
        while m < w:
            s = s[m * BATCH:, :] + s[:-m * BATCH, :]
            m *= 2
        pm.append(s * (1.0 / w) - up[:, lo:hi])
    up_buf[0:POOL_HALO, :] = up_buf[nrows:nrows + POOL_HALO, :]
    yp = [_bdot(jnp.concatenate(pm[2 * j:2 * j + 2], axis=-1), w_pool_ref[j]) for j in range(2)]
    y_pool = _rms(jnp.concatenate(yp, axis=-1), pool_scale_ref[...])
    between()

    y = _bdot(jnp.concatenate([y_lru, y_pool], axis=-1), w_out_ref[...])
    between()
    return x + y


def _mixer_kernel(x_hbm, meta_ref, g1_ref, w_in_ref, conv_w_ref, conv_b_ref, w_gate_ref,
                  b_gate_ref, lam8_ref, lru_gain_ref, w_pool_ref, pool_scale_ref, w_out_ref,
                  g2_ref, w_rt_ref, b_rt_ref, tri_ref,
                  h1_ref, xs_hbm, cnt_ref,
                  ux_buf, up_buf, a_buf, b_buf, hs_buf, hstate, running,
                  xin, rowbuf, dest_v, dest_s, in_sem, row_sem, idx_sem):
    i = pl.program_id(0)
    slot = i % 2
    n_rows = TOP_K * MIX_ROWS
    mix = functools.partial(
        _mix_rows, g1_ref=g1_ref, w_in_ref=w_in_ref, conv_w_ref=conv_w_ref,
        conv_b_ref=conv_b_ref, w_gate_ref=w_gate_ref, b_gate_ref=b_gate_ref, lam8_ref=lam8_ref,
        lru_gain_ref=lru_gain_ref, w_pool_ref=w_pool_ref, pool_scale_ref=pool_scale_ref,
        w_out_ref=w_out_ref, ux_buf=ux_buf, up_buf=up_buf, a_buf=a_buf, b_buf=b_buf,
        hs_buf=hs_buf, hstate=hstate)
    idx_copy = pltpu.make_async_copy(dest_v, dest_s, idx_sem)
    round_copy = pltpu.make_async_copy(rowbuf, xs_hbm.at[pl.ds(0, n_rows * ROW_SUB)], row_sem)

    def x_copy(step, s, b):
        return pltpu.make_async_copy(x_hbm.at[b, pl.ds(step * MIX_T, MIX_T), :],
                                     xin.at[s, :, b, :], in_sem.at[s])

    def scatter_rows(first, count):
        for n in range(first, first + count):
            k, r = divmod(n, MIX_ROWS)
            d = pl.multiple_of(dest_s[k, r], ROW_SUB)
            pltpu.make_async_copy(rowbuf.at[pl.ds(n * ROW_SUB, ROW_SUB)],
                                  xs_hbm.at[pl.ds(d, ROW_SUB)], row_sem).start(priority=n % 2)

    issued = [0]

    def scatter_batch():
        scatter_rows(issued[0], SCATTER_BATCH)
        issued[0] += SCATTER_BATCH

    @pl.when(i == 0)
    def _():
        for b in range(BATCH):
            x_copy(0, 0, b).start()
        ux_buf[0:CONV_HALO, :] = jnp.zeros((CONV_HALO, LRU_WIDTH), jnp.float32)
        up_buf[0:POOL_HALO, :] = jnp.zeros((POOL_HALO, POOL_WIDTH), jnp.float32)
        hstate[...] = jnp.zeros_like(hstate)
        running[...] = jnp.zeros_like(running)
        mix(meta_ref[...], META_ROWS, False)
        rowbuf[...] = jnp.zeros_like(rowbuf)
        col = lax.broadcasted_iota(jnp.int32, (SUBLANES, MIX_ROWS), 1)
        sub = lax.broadcasted_iota(jnp.int32, (SUBLANES, MIX_ROWS), 0)
        dest_v[...] = (XS_DUMP + jnp.minimum(sub, TOP_K - 1) * MIX_ROWS + col) * ROW_SUB
        idx_copy.start()

    @pl.when(i + 1 < MIX_STEPS)
    def _():
        for b in range(BATCH):
            x_copy(i + 1, 1 - slot, b).start()

    for b in range(BATCH):
        x_copy(i, slot, b).wait()
    idx_copy.wait()

    @pl.when(i >= 0)
    def _():
        x = xin[slot].reshape(MIX_ROWS, D_MODEL)
        h1_ref[...] = mix(x, MIX_ROWS, True, between=scatter_batch)

    h1 = h1_ref[...]
    hn2 = _rms(h1, g2_ref[...])
    logits = _bdot(hn2, w_rt_ref[...]) + b_rt_ref[...]
    scatter_batch()
    lane = lax.broadcasted_iota(jnp.int32, (MIX_ROWS, LANES), 1)
    lane_f = lane.astype(jnp.float32)
    neg = jnp.float32(-jnp.inf)
    big = jnp.float32(4 * LANES)

    is_g = (lane >= N_EXPERTS) & (lane < N_EXPERTS + N_GROUPS)
    gl = jnp.where(is_g, logits, neg)
    gmax = jnp.max(gl, axis=-1, keepdims=True)
    gidx = jnp.min(jnp.where(gl == gmax, lane_f, big), axis=-1, keepdims=True)
    p_g = 1.0 / jnp.sum(jnp.exp(gl - gmax), axis=-1, keepdims=True)
    grp = gidx.astype(jnp.int32) - N_EXPERTS

    el = jnp.where((lane >> 3) == grp, logits, neg)
    m1 = jnp.max(el, axis=-1, keepdims=True)
    i1 = jnp.min(jnp.where(el == m1, lane_f, big), axis=-1, keepdims=True)
    el2 = jnp.where(lane_f == i1, neg, el)
    m2 = jnp.max(el2, axis=-1, keepdims=True)
    i2 = jnp.min(jnp.where(el2 == m2, lane_f, big), axis=-1, keepdims=True)
    e21 = jnp.exp(m2 - m1)
    den = 1.0 + e21
    gate = (p_g * (1.0 / den), p_g * (e21 / den))
    scatter_batch()

    sel = (lane_f == i1, lane_f == i2)
    onehot = jnp.where(sel[0] | sel[1], 1.0, 0.0)
    before = jnp.dot(tri_ref[...], onehot.astype(jnp.bfloat16),
                     preferred_element_type=jnp.float32) + running[0:1, :]
    rank = [jnp.sum(jnp.where(sel[k], before, 0.0), axis=-1, keepdims=True) for k in range(TOP_K)]
    running[0:1, :] = running[0:1, :] + jnp.sum(onehot, axis=0, keepdims=True)
    cnt_ref[...] = running[...]
    scatter_batch()
    assert issued[0] == n_rows

    eid = (i1.astype(jnp.int32), i2.astype(jnp.int32))
    dst = [(eid[k] * EXPERT_CAP + rank[k].astype(jnp.int32)) * ROW_SUB for k in range(TOP_K)]
    dst_tile = jnp.where(lane == 0, dst[0], jnp.where(lane == 1, dst[1], 0))
    dest_v[...] = jnp.transpose(dst_tile)[0:SUBLANES, :]
    idx_copy.start()

    data = _pack_bf16_pairs(hn2[:, :HALF], hn2[:, HALF:])
    tok = i * MIX_ROWS + lax.broadcasted_iota(jnp.int32, (MIX_ROWS, LANES), 0)

    round_copy.wait()
    for k in range(TOP_K):
        base = k * MIX_ROWS * ROW_SUB
        for c in range(DATA_SUB):
            rowbuf[pl.ds(base + c, MIX_ROWS, stride=ROW_SUB), :] = data[:, c * LANES:(c + 1) * LANES]
        ret = (tok * TOP_K + k) * DATA_SUB
        gbits = lax.bitcast_convert_type(jnp.broadcast_to(gate[k], (MIX_ROWS, LANES)), jnp.int32)
        meta = jnp.where(lane == 0, ret, jnp.where(lane == 1, gbits, 0))
        rowbuf[pl.ds(base + DATA_SUB, MIX_ROWS, stride=ROW_SUB), :] = lax.bitcast_convert_type(
            meta, jnp.uint32)

    @pl.when(i == MIX_STEPS - 1)
    def _():
        idx_copy.wait()
        scatter_rows(0, n_rows)
        round_copy.wait()


def _mixer(x, meta_t, g1, w_in, conv_w, conv_b, w_gate, b_gate, lam8, lru_gain, w_pool,
           pool_scale, w_out, g2, w_rt, b_rt, tri):
    full = lambda a: pl.BlockSpec(a.shape, lambda i: (0,) * a.ndim)
    consts = (meta_t, g1, w_in, conv_w, conv_b, w_gate, b_gate, lam8, lru_gain, w_pool,
              pool_scale, w_out, g2, w_rt, b_rt, tri)
    return pl.pallas_call(
        _mixer_kernel,
        grid=(MIX_STEPS,),
        in_specs=[pl.BlockSpec(memory_space=pl.ANY)] + [full(a) for a in consts],
        out_specs=[pl.BlockSpec((MIX_ROWS, D_MODEL), lambda i: (i, 0)),
                   pl.BlockSpec(memory_space=pl.ANY),
                   pl.BlockSpec((SUBLANES, LANES), lambda i: (0, 0))],
        out_shape=[jax.ShapeDtypeStruct((N_TOK, D_MODEL), jnp.float32),
                   jax.ShapeDtypeStruct((XS_ROWS * ROW_SUB, LANES), jnp.uint32),
                   jax.ShapeDtypeStruct((SUBLANES, LANES), jnp.float32)],
        scratch_shapes=[pltpu.VMEM((CONV_HALO + MIX_ROWS, LRU_WIDTH), jnp.float32),
                        pltpu.VMEM((POOL_HALO + MIX_ROWS, POOL_WIDTH), jnp.float32),
                        pltpu.VMEM((MIX_ROWS, LRU_WIDTH), jnp.float32),
                        pltpu.VMEM((MIX_ROWS, LRU_WIDTH), jnp.float32),
                        pltpu.VMEM((MIX_ROWS, LRU_WIDTH), jnp.float32),
                        pltpu.VMEM((SUBLANES, LRU_WIDTH), jnp.float32),
                        pltpu.VMEM((SUBLANES, LANES), jnp.float32),
                        pltpu.VMEM((2, MIX_T, BATCH, D_MODEL), jnp.float32),
                        pltpu.VMEM((TOP_K * MIX_ROWS * ROW_SUB, LANES), jnp.uint32),
                        pltpu.VMEM((SUBLANES, MIX_ROWS), jnp.int32),
                        pltpu.SMEM((SUBLANES, MIX_ROWS), jnp.int32),
                        pltpu.SemaphoreType.DMA((2,)),
                        pltpu.SemaphoreType.DMA,
                        pltpu.SemaphoreType.DMA],
        compiler_params=pltpu.CompilerParams(dimension_semantics=("arbitrary",),
                                             vmem_limit_bytes=VMEM_LIMIT),
        name="mixer",
    )(x, *consts)


def _experts_kernel(blk_ref, bexp_ref, nval_ref, ntot_ref, x_ref, wg_ref, wu_ref, wd_ref, y_hbm,
                    wgu_b, wd_b, obuf, addr_v, addr_s, row_sem, idx_sem):
    i = pl.program_id(0)
    slot = i % 2
    prev = 1 - slot
    n_used = ntot_ref[0]
    idx_copy = pltpu.make_async_copy(addr_v, addr_s.at[slot], idx_sem)
    prev_idx_copy = pltpu.make_async_copy(addr_v, addr_s.at[prev], idx_sem)

    def round_copy(s):
        return pltpu.make_async_copy(obuf.at[s], y_hbm.at[pl.ds(0, MOE_BM * DATA_SUB)], row_sem.at[s])

    def scatter_rows(first, count):
        for r in range(first, first + count):
            d = pl.multiple_of(addr_s[prev, 0, r], DATA_SUB)
            pltpu.make_async_copy(obuf.at[prev, pl.ds(r * DATA_SUB, DATA_SUB)],
                                  y_hbm.at[pl.ds(d, DATA_SUB)], row_sem.at[prev]).start(priority=r % 2)

    def dump_rows(s):
        return (Y_DUMP + s * MOE_BM + lax.broadcasted_iota(jnp.int32, (MOE_BM, LANES), 0)) * DATA_SUB

    @pl.when(i == 0)
    def _():
        obuf[...] = jnp.zeros_like(obuf)
        addr_v[...] = jnp.transpose(dump_rows(1))[0:SUBLANES, :]
        prev_idx_copy.start()

    @pl.when((i < n_used) & ((i == 0) | (bexp_ref[i] != bexp_ref[jnp.maximum(i - 1, 0)])))
    def _():
        wgu_b[:, 0:EXPERT_FF] = wg_ref[0].astype(jnp.bfloat16)
        wgu_b[:, EXPERT_FF:] = wu_ref[0].astype(jnp.bfloat16)
        wd_b[...] = wd_ref[0].astype(jnp.bfloat16)

    @pl.when(i < n_used)
    def _():
        prev_idx_copy.wait()
        meta = lax.bitcast_convert_type(x_ref[pl.ds(DATA_SUB, MOE_BM, stride=ROW_SUB), :], jnp.int32)
        row = lax.broadcasted_iota(jnp.int32, (MOE_BM, LANES), 0)
        lane = lax.broadcasted_iota(jnp.int32, (MOE_BM, LANES), 1)
        ret = jnp.where(row < nval_ref[i], meta, dump_rows(slot))
        addr_v[...] = jnp.transpose(jnp.where(lane == 0, ret, 0))[0:SUBLANES, :]
        idx_copy.start()

        def compute(nrows):
            batch = MOE_BM // 4
            parts = [_unpack_bf16_pairs(x_ref[pl.ds(c, nrows, stride=ROW_SUB), :]) for c in range(DATA_SUB)]
            x = jnp.concatenate([p[0] for p in parts] + [p[1] for p in parts], axis=-1)
            scatter_rows(0, batch)
            gu = _bdot(x, wgu_b[...])
            scatter_rows(batch, batch)
            g = gu[:, :EXPERT_FF]
            hid = (g * _sigmoid(g)) * gu[:, EXPERT_FF:]
            scatter_rows(2 * batch, batch)
            gate = lax.bitcast_convert_type(meta[0:nrows, 1:2], jnp.float32)
            o = _bdot(hid, wd_b[...]) * gate
            scatter_rows(3 * batch, batch)
            packed = _pack_bf16_pairs(o[:, :HALF], o[:, HALF:])

            @pl.when(i >= 1)
            def _():
                round_copy(slot).wait()

            for c in range(DATA_SUB):
                obuf[slot, pl.ds(c, nrows, stride=DATA_SUB), :] = packed[:, c * LANES:(c + 1) * LANES]

        half_full = nval_ref[i] <= MOE_BM // 2

        @pl.when(half_full)
        def _():
            compute(MOE_BM // 2)

        @pl.when(jnp.logical_not(half_full))
        def _():
            compute(MOE_BM)

    @pl.when(i == n_used)
    def _():
        prev_idx_copy.wait()
        scatter_rows(0, MOE_BM)

    @pl.when(i == pl.num_programs(0) - 1)
    def _():
        round_copy(n_used % 2).wait()
        round_copy((n_used - 1) % 2).wait()


def _experts(blk_idx, blk_expert, blk_nvalid, n_used, xs, w_gate, w_up, w_down):
    wmap = lambda i, blk, bexp, nval, ntot: (bexp[i], 0, 0)
    return pl.pallas_call(
        _experts_kernel,
        grid_spec=pltpu.PrefetchScalarGridSpec(
            num_scalar_prefetch=4,
            grid=(N_BLOCKS + 1,),
            in_specs=[pl.BlockSpec((MOE_BM * ROW_SUB, LANES),
                                   lambda i, blk, bexp, nval, ntot: (blk[i], 0)),
                      pl.BlockSpec((1, D_MODEL, EXPERT_FF), wmap),
                      pl.BlockSpec((1, D_MODEL, EXPERT_FF), wmap),
                      pl.BlockSpec((1, EXPERT_FF, D_MODEL), wmap)],
            out_specs=pl.BlockSpec(memory_space=pl.ANY),
            scratch_shapes=[pltpu.VMEM((D_MODEL, 2 * EXPERT_FF), jnp.bfloat16),
                            pltpu.VMEM((EXPERT_FF, D_MODEL), jnp.bfloat16),
                            pltpu.VMEM((2, MOE_BM * DATA_SUB, LANES), jnp.uint32),
                            pltpu.VMEM((SUBLANES, MOE_BM), jnp.int32),
                            pltpu.SMEM((2, SUBLANES, MOE_BM), jnp.int32),
                            pltpu.SemaphoreType.DMA((2,)),
                            pltpu.SemaphoreType.DMA]),
        out_shape=jax.ShapeDtypeStruct((Y_ROWS * DATA_SUB, LANES), jnp.uint32),
        compiler_params=pltpu.CompilerParams(dimension_semantics=("arbitrary",),
                                             vmem_limit_bytes=VMEM_LIMIT),
        name="experts",
    )(blk_idx, blk_expert, blk_nvalid, n_used, xs, w_gate, w_up, w_down)


def _final_kernel(h1_ref, y_ref, gain_ref, out_hbm, obuf, sem):
    i = pl.program_id(0)
    slot = i % 2
    stride = TOP_K * DATA_SUB

    def out_copy(step, s, b):
        return pltpu.make_async_copy(obuf.at[s, :, b, :],
                                     out_hbm.at[b, pl.ds(step * FIN_T, FIN_T), :], sem.at[s])

    lo, hi = [], []
    for c in range(DATA_SUB):
        y0 = _unpack_bf16_pairs(y_ref[pl.ds(c, FIN_TM, stride=stride), :])
        y1 = _unpack_bf16_pairs(y_ref[pl.ds(DATA_SUB + c, FIN_TM, stride=stride), :])
        lo.append(y0[0] + y1[0])
        hi.append(y0[1] + y1[1])
    y = jnp.concatenate(lo + hi, axis=-1)
    res = _rms(h1_ref[...] + y, gain_ref[...])

    @pl.when(i >= 2)
    def _():
        for b in range(BATCH):
            out_copy(i - 2, slot, b).wait()

    obuf[slot] = res.reshape(FIN_T, BATCH, D_MODEL)
    for b in range(BATCH):
        out_copy(i, slot, b).start()

    @pl.when(i == pl.num_programs(0) - 1)
    def _():
        for b in range(BATCH):
            out_copy(i - 1, 1 - slot, b).wait()
        for b in range(BATCH):
            out_copy(i, slot, b).wait()


def _final(h1, y2, final_gain):
    rows = FIN_TM * TOP_K * DATA_SUB
    return pl.pallas_call(
        _final_kernel,
        grid=(N_TOK // FIN_TM,),
        in_specs=[pl.BlockSpec((FIN_TM, D_MODEL), lambda i: (i, 0)),
                  pl.BlockSpec((rows, LANES), lambda i: (i, 0)),
                  pl.BlockSpec((1, D_MODEL), lambda i: (0, 0))],
        out_specs=pl.BlockSpec(memory_space=pl.ANY),
        out_shape=jax.ShapeDtypeStruct((BATCH, SEQ, D_MODEL), jnp.float32),
        scratch_shapes=[pltpu.VMEM((2, FIN_T, BATCH, D_MODEL), jnp.float32),
                        pltpu.SemaphoreType.DMA((2,))],
        compiler_params=pltpu.CompilerParams(dimension_semantics=("arbitrary",),
                                             vmem_limit_bytes=VMEM_LIMIT),
        name="final",
    )(h1, y2, final_gain)


def _block_diag(blocks):
    n, r, c = blocks.shape
    on_diag = jnp.eye(n, dtype=bool)[:, None, :, None]
    return jnp.where(on_diag, blocks[:, :, None, :], 0).reshape(n * r, n * c)


def kernel(x, meta_tokens, norm1_gain, w_in, conv_w, conv_b, lru_wa, lru_ba, lru_wx, lru_bx,
           lru_lambda, lru_out_gain, pool_w, pool_scale, w_out, norm2_gain, w_group, b_group,
           w_router, b_router, w_gate, w_up, w_down, final_gain):
    assert x.shape == (BATCH, SEQ, D_MODEL) and norm1_gain.shape[0] == 1
    f32, bf16 = jnp.float32, jnp.bfloat16
    row = lambda v: v.reshape(1, -1).astype(f32)

    meta_t = jnp.repeat(meta_tokens.astype(f32), BATCH, axis=0)

    heads_per = LRU_HEADS // 2
    w_gate_blk = jnp.stack([
        jnp.concatenate([_block_diag(lru_wa[0, j * heads_per:(j + 1) * heads_per]),
                         _block_diag(lru_wx[0, j * heads_per:(j + 1) * heads_per])],
                        axis=1) for j in range(2)]).astype(bf16)
    b_gate_blk = jnp.stack([lru_ba[0], lru_bx[0]]).astype(f32)
    w_pool_blk = jnp.stack([_block_diag(pool_w[0, 2 * j:2 * j + 2])
                            for j in range(2)]).astype(bf16)
    lam8 = row(LRU_C * jax.nn.log_sigmoid(lru_lambda[0].astype(f32)))
    pad = LANES - N_EXPERTS - N_GROUPS
    w_rt = jnp.concatenate([w_router[0], w_group[0], jnp.zeros((D_MODEL, pad), f32)], axis=1).astype(bf16)
    b_rt = jnp.concatenate([b_router[0], b_group[0], jnp.zeros((pad,), f32)]).reshape(1, LANES).astype(f32)
    ridx = jnp.arange(MIX_ROWS)
    tri = (ridx[None, :] < ridx[:, None]).astype(bf16)

    h1, xs, cnt = _mixer(
        x, meta_t, row(norm1_gain[0]), w_in[0].astype(bf16), conv_w[0].astype(f32),
        row(conv_b[0]), w_gate_blk, b_gate_blk, lam8, row(lru_out_gain[0]), w_pool_blk,
        row(pool_scale[0]), w_out[0].astype(bf16), row(norm2_gain[0]), w_rt, b_rt, tri)

    counts = cnt[0, :N_EXPERTS].astype(jnp.int32)
    nblk = (counts + MOE_BM - 1) // MOE_BM
    blk_end = jnp.cumsum(nblk)
    step = jnp.arange(N_BLOCKS + 1, dtype=jnp.int32)
    used = step < blk_end[-1]
    step_c = jnp.minimum(step, blk_end[-1] - 1)
    e_of = jnp.sum(blk_end[None, :] <= step_c[:, None], axis=-1).astype(jnp.int32)
    onehot_e = e_of[:, None] == jnp.arange(N_EXPERTS, dtype=jnp.int32)[None, :]
    j_of = step_c - jnp.sum(jnp.where(onehot_e, (blk_end - nblk)[None, :], 0), axis=-1)
    cnt_of = jnp.sum(jnp.where(onehot_e, counts[None, :], 0), axis=-1)
    blk_idx = (e_of * CAP_BLOCKS + j_of).astype(jnp.int32)
    blk_nvalid = jnp.where(used, jnp.clip(cnt_of - j_of * MOE_BM, 0, MOE_BM), 0).astype(jnp.int32)

    y2 = _experts(blk_idx, e_of, blk_nvalid, blk_end[-1:].astype(jnp.int32), xs,
                  w_gate[0], w_up[0], w_down[0])
    return _final(h1, y2, row(final_gain))
```

```python
import functools

import jax
import jax.numpy as jnp
from jax import lax
from jax.experimental import pallas as pl
from jax.experimental.pallas import tpu as pltpu

D_MODEL = 1024
BATCH = 8
SEQ = 4096
N_META = 16
LRU_WIDTH = 512
LRU_HEADS = 8
LRU_HEAD_DIM = 64
CONV_WIDTH = 4
LRU_C = 8.0
POOL_WIDTH = 512
POOL_WINDOWS = (2, 4, 8, 16)
POOL_GROUP = 128
N_GROUPS = 4
EXPERTS_PER_GROUP = 8
N_EXPERTS = 32
TOP_K = 2
EXPERT_FF = 512
RMS_EPS = 1e-6
SQRT_FLOOR = 1e-37

N_TOK = BATCH * SEQ
SUBLANES = 8
LANES = 128
MIX_ROWS = 512
META_ROWS = N_META * BATCH
CONV_HALO = (CONV_WIDTH - 1) * BATCH
POOL_HALO = (max(POOL_WINDOWS) - 1) * BATCH
MIX_STEPS = N_TOK // MIX_ROWS
MIX_T = MIX_ROWS // BATCH
MOE_BM = 512
N_ASSIGN = N_TOK * TOP_K
N_BLOCKS = N_ASSIGN // MOE_BM + N_EXPERTS
EXPERT_CAP = N_TOK
CAP_BLOCKS = EXPERT_CAP // MOE_BM
ROW_SUB = 8
DATA_SUB = 4
HALF = D_MODEL // 2
XS_DUMP = N_EXPERTS * EXPERT_CAP
XS_ROWS = XS_DUMP + TOP_K * MIX_ROWS
SCATTER_BATCH = TOP_K * MIX_ROWS // 8
Y_DUMP = N_ASSIGN
Y_ROWS = Y_DUMP + 2 * MOE_BM
FIN_TM = 1024
FIN_T = FIN_TM // BATCH
RT_ROWS = 40
VMEM_LIMIT = 56 * 1024 * 1024


def _rms(x, gain):
    return x * lax.rsqrt(jnp.mean(x * x, axis=-1, keepdims=True) + RMS_EPS) * gain


def _sigmoid(x):
    return 0.5 * jnp.tanh(0.5 * x) + 0.5


def _bdot(a, b):
    return jnp.dot(a.astype(jnp.bfloat16), b, preferred_element_type=jnp.float32)


def _pack_bf16_pairs(lo, hi):
    lo_b = lax.bitcast_convert_type(lo.astype(jnp.bfloat16).astype(jnp.float32), jnp.uint32)
    hi_b = lax.bitcast_convert_type(hi.astype(jnp.bfloat16).astype(jnp.float32), jnp.uint32)
    return (lo_b >> 16) | (hi_b & jnp.uint32(0xFFFF0000))


def _unpack_bf16_pairs(u):
    lo = lax.bitcast_convert_type(u << 16, jnp.float32)
    hi = lax.bitcast_convert_type(u & jnp.uint32(0xFFFF0000), jnp.float32)
    return lo, hi


def _mix_rows(x, nrows, with_output, g1_ref, w_in_ref, conv_w_ref, conv_b_ref, w_gate_ref,
              b_gate_ref, lam8_ref, lru_gain_ref, w_pool_ref, pool_scale_ref, w_out_ref,
              ux_buf, up_buf, a_buf, b_buf, hs_buf, hstate, between=lambda: None):
    hn = _rms(x, g1_ref[...])
    proj = _bdot(hn, w_in_ref[...])
    between()
    ux = proj[:, :LRU_WIDTH]
    up = proj[:, 2 * LRU_WIDTH:]

    ux_buf[CONV_HALO:CONV_HALO + nrows, :] = ux
    xc = conv_b_ref[...] + conv_w_ref[3:4, :] * ux
    for k in range(CONV_WIDTH - 1):
        xc = xc + conv_w_ref[k:k + 1, :] * ux_buf[k * BATCH:k * BATCH + nrows, :]
    ux_buf[0:CONV_HALO, :] = ux_buf[nrows:nrows + CONV_HALO, :]
    between()

    half = LRU_WIDTH // 2
    z = [_bdot(xc[:, j * half:(j + 1) * half], w_gate_ref[j]) for j in range(2)]
    za = jnp.concatenate([z[0][:, :half], z[1][:, :half]], axis=-1) + b_gate_ref[0:1, :]
    zx = jnp.concatenate([z[0][:, half:], z[1][:, half:]], axis=-1) + b_gate_ref[1:2, :]
    r = _sigmoid(za)
    gi = _sigmoid(zx)
    log_a = lam8_ref[...] * r
    a = jnp.exp(log_a)
    a_buf[0:nrows, :] = a
    v = jnp.tanh(-log_a) * (1.0 + a * a)
    root = v * lax.rsqrt(jnp.maximum(v, SQRT_FLOOR))
    b_buf[0:nrows, :] = root * (gi * xc)
    between()

    def step(t, h):
        r0 = pl.multiple_of(t * SUBLANES, SUBLANES)
        h = a_buf[pl.ds(r0, SUBLANES), :] * h + b_buf[pl.ds(r0, SUBLANES), :]
        hs_buf[pl.ds(r0, SUBLANES), :] = h
        return h

    hstate[...] = lax.fori_loop(0, nrows // SUBLANES, step, hstate[...], unroll=True)

    up_buf[POOL_HALO:POOL_HALO + nrows, :] = up
    if not with_output:
        up_buf[0:POOL_HALO, :] = up_buf[nrows:nrows + POOL_HALO, :]
        return None

    ug = proj[:, LRU_WIDTH:2 * LRU_WIDTH]
    y_lru = _rms(hs_buf[0:nrows, :] * jax.nn.gelu(ug), lru_gain_ref[...])

    pm = []
    for g, w in enumerate(POOL_WINDOWS):
        lo, hi = g * POOL_GROUP, (g + 1) * POOL_GROUP
        s = up_buf[POOL_HALO - (w - 1) * BATCH:POOL_HALO + nrows, lo:hi]
        m = 1
        while m < w:
            s = s[m * BATCH:, :] + s[:-m * BATCH, :]
            m *= 2
        pm.append(s * (1.0 / w) - up[:, lo:hi])
    up_buf[0:POOL_HALO, :] = up_buf[nrows:nrows + POOL_HALO, :]
    yp = [_bdot(jnp.concatenate(pm[2 * j:2 * j + 2], axis=-1), w_pool_ref[j]) for j in range(2)]
    y_pool = _rms(jnp.concatenate(yp, axis=-1), pool_scale_ref[...])
    between()

    y = _bdot(jnp.concatenate([y_lru, y_pool], axis=-1), w_out_ref[...])
    between()
    return x + y


def _mixer_kernel(x_hbm, meta_ref, g1_ref, w_in_ref, conv_w_ref, conv_b_ref, w_gate_ref,
                  b_gate_ref, lam8_ref, lru_gain_ref, w_pool_ref, pool_scale_ref, w_out_ref,
                  g2_ref, w_rt_ref, b_rt_ref, tri_ref,
                  h1_ref, xs_hbm, cnt_ref,
                  ux_buf, up_buf, a_buf, b_buf, hs_buf, hstate, running,
                  xin, rowbuf, dest_v, dest_s, in_sem, row_sem, idx_sem):
    i = pl.program_id(0)
    slot = i % 2
    n_rows = TOP_K * MIX_ROWS
    mix = functools.partial(
        _mix_rows, g1_ref=g1_ref, w_in_ref=w_in_ref, conv_w_ref=conv_w_ref,
        conv_b_ref=conv_b_ref, w_gate_ref=w_gate_ref, b_gate_ref=b_gate_ref, lam8_ref=lam8_ref,
        lru_gain_ref=lru_gain_ref, w_pool_ref=w_pool_ref, pool_scale_ref=pool_scale_ref,
        w_out_ref=w_out_ref, ux_buf=ux_buf, up_buf=up_buf, a_buf=a_buf, b_buf=b_buf,
        hs_buf=hs_buf, hstate=hstate)
    idx_copy = pltpu.make_async_copy(dest_v, dest_s, idx_sem)
    round_copy = pltpu.make_async_copy(rowbuf, xs_hbm.at[pl.ds(0, n_rows * ROW_SUB)], row_sem)

    def x_copy(step, s, b):
        return pltpu.make_async_copy(x_hbm.at[b, pl.ds(step * MIX_T, MIX_T), :],
                                     xin.at[s, :, b, :], in_sem.at[s])

    def scatter_rows(first, count):
        for n in range(first, first + count):
            k, r = divmod(n, MIX_ROWS)
            d = pl.multiple_of(dest_s[k, r], ROW_SUB)
            pltpu.make_async_copy(rowbuf.at[pl.ds(n * ROW_SUB, ROW_SUB)],
                                  xs_hbm.at[pl.ds(d, ROW_SUB)], row_sem).start(priority=n % 2)

    issued = [0]

    def scatter_batch():
        scatter_rows(issued[0], SCATTER_BATCH)
        issued[0] += SCATTER_BATCH

    @pl.when(i == 0)
    def _():
        for b in range(BATCH):
            x_copy(0, 0, b).start()
        ux_buf[0:CONV_HALO, :] = jnp.zeros((CONV_HALO, LRU_WIDTH), jnp.float32)
        up_buf[0:POOL_HALO, :] = jnp.zeros((POOL_HALO, POOL_WIDTH), jnp.float32)
        hstate[...] = jnp.zeros_like(hstate)
        running[...] = jnp.zeros_like(running)
        mix(meta_ref[...], META_ROWS, False)
        rowbuf[...] = jnp.zeros_like(rowbuf)
        col = lax.broadcasted_iota(jnp.int32, (SUBLANES, MIX_ROWS), 1)
        sub = lax.broadcasted_iota(jnp.int32, (SUBLANES, MIX_ROWS), 0)
        dest_v[...] = (XS_DUMP + jnp.minimum(sub, TOP_K - 1) * MIX_ROWS + col) * ROW_SUB
        idx_copy.start()

    @pl.when(i + 1 < MIX_STEPS)
    def _():
        for b in range(BATCH):
            x_copy(i + 1, 1 - slot, b).start()

    for b in range(BATCH):
        x_copy(i, slot, b).wait()
    idx_copy.wait()

    @pl.when(i >= 0)
    def _():
        x = xin[slot].reshape(MIX_ROWS, D_MODEL)
        h1_ref[...] = mix(x, MIX_ROWS, True, between=scatter_batch)

    h1 = h1_ref[...]
    hn2 = _rms(h1, g2_ref[...])
    logits = lax.dot_general(w_rt_ref[...], hn2.astype(jnp.bfloat16), (((1,), (1,)), ((), ())),
                             preferred_element_type=jnp.float32) + b_rt_ref[:, 0:1]
    scatter_batch()
    neg = jnp.float32(-jnp.inf)
    big = jnp.float32(4 * LANES)

    gl = logits[N_EXPERTS:N_EXPERTS + N_GROUPS, :]
    grow = lax.broadcasted_iota(jnp.int32, (N_GROUPS, MIX_ROWS), 0).astype(jnp.float32)
    gmax = jnp.max(gl, axis=0, keepdims=True)
    grp = jnp.min(jnp.where(gl == gmax, grow, big), axis=0, keepdims=True).astype(jnp.int32)
    p_g = 1.0 / jnp.sum(jnp.exp(gl - gmax), axis=0, keepdims=True)

    erow = lax.broadcasted_iota(jnp.int32, (N_EXPERTS, MIX_ROWS), 0)
    erow_f = erow.astype(jnp.float32)
    el = jnp.where((erow >> 3) == grp, logits[0:N_EXPERTS, :], neg)
    m1 = jnp.max(el, axis=0, keepdims=True)
    i1 = jnp.min(jnp.where(el == m1, erow_f, big), axis=0, keepdims=True)
    el2 = jnp.where(erow_f == i1, neg, el)
    m2 = jnp.max(el2, axis=0, keepdims=True)
    i2 = jnp.min(jnp.where(el2 == m2, erow_f, big), axis=0, keepdims=True)
    e21 = jnp.exp(m2 - m1)
    den = 1.0 + e21
    gate_rows = (p_g * (1.0 / den), p_g * (e21 / den))
    scatter_batch()

    sel = (erow_f == i1, erow_f == i2)
    onehot = jnp.where(sel[0] | sel[1], 1.0, 0.0)
    before = jnp.dot(onehot.astype(jnp.bfloat16), tri_ref[...],
                     preferred_element_type=jnp.float32) + running[:, 0:1]
    rank = [jnp.sum(jnp.where(sel[k], before, 0.0), axis=0, keepdims=True) for k in range(TOP_K)]
    running[...] = running[...] + jnp.sum(onehot, axis=1, keepdims=True)
    cnt_ref[...] = running[...]
    scatter_batch()
    assert issued[0] == n_rows

    for k, e_k in enumerate((i1, i2)):
        dest_v[k:k + 1, :] = (e_k.astype(jnp.int32) * EXPERT_CAP + rank[k].astype(jnp.int32)) * ROW_SUB
    idx_copy.start()

    gate_tile = jnp.concatenate(list(gate_rows) + [jnp.zeros((LANES - TOP_K, MIX_ROWS), jnp.float32)], axis=0)
    gate_cols = jnp.transpose(gate_tile)
    gate = (gate_cols[:, 0:1], gate_cols[:, 1:2])
    lane = lax.broadcasted_iota(jnp.int32, (MIX_ROWS, LANES), 1)

    data = _pack_bf16_pairs(hn2[:, :HALF], hn2[:, HALF:])
    tok = i * MIX_ROWS + lax.broadcasted_iota(jnp.int32, (MIX_ROWS, LANES), 0)

    round_copy.wait()
    for k in range(TOP_K):
        base = k * MIX_ROWS * ROW_SUB
        for c in range(DATA_SUB):
            rowbuf[pl.ds(base + c, MIX_ROWS, stride=ROW_SUB), :] = data[:, c * LANES:(c + 1) * LANES]
        ret = (tok * TOP_K + k) * DATA_SUB
        gbits = lax.bitcast_convert_type(jnp.broadcast_to(gate[k], (MIX_ROWS, LANES)), jnp.int32)
        meta = jnp.where(lane == 0, ret, jnp.where(lane == 1, gbits, 0))
        rowbuf[pl.ds(base + DATA_SUB, MIX_ROWS, stride=ROW_SUB), :] = lax.bitcast_convert_type(
            meta, jnp.uint32)

    @pl.when(i == MIX_STEPS - 1)
    def _():
        idx_copy.wait()
        scatter_rows(0, n_rows)
        round_copy.wait()


def _mixer(x, meta_t, g1, w_in, conv_w, conv_b, w_gate, b_gate, lam8, lru_gain, w_pool,
           pool_scale, w_out, g2, w_rt, b_rt, tri):
    full = lambda a: pl.BlockSpec(a.shape, lambda i: (0,) * a.ndim)
    consts = (meta_t, g1, w_in, conv_w, conv_b, w_gate, b_gate, lam8, lru_gain, w_pool,
              pool_scale, w_out, g2, w_rt, b_rt, tri)
    return pl.pallas_call(
        _mixer_kernel,
        grid=(MIX_STEPS,),
        in_specs=[pl.BlockSpec(memory_space=pl.ANY)] + [full(a) for a in consts],
        out_specs=[pl.BlockSpec((MIX_ROWS, D_MODEL), lambda i: (i, 0)),
                   pl.BlockSpec(memory_space=pl.ANY),
                   pl.BlockSpec((N_EXPERTS, LANES), lambda i: (0, 0))],
        out_shape=[jax.ShapeDtypeStruct((N_TOK, D_MODEL), jnp.float32),
                   jax.ShapeDtypeStruct((XS_ROWS * ROW_SUB, LANES), jnp.uint32),
                   jax.ShapeDtypeStruct((N_EXPERTS, LANES), jnp.float32)],
        scratch_shapes=[pltpu.VMEM((CONV_HALO + MIX_ROWS, LRU_WIDTH), jnp.float32),
                        pltpu.VMEM((POOL_HALO + MIX_ROWS, POOL_WIDTH), jnp.float32),
                        pltpu.VMEM((MIX_ROWS, LRU_WIDTH), jnp.float32),
                        pltpu.VMEM((MIX_ROWS, LRU_WIDTH), jnp.float32),
                        pltpu.VMEM((MIX_ROWS, LRU_WIDTH), jnp.float32),
                        pltpu.VMEM((SUBLANES, LRU_WIDTH), jnp.float32),
                        pltpu.VMEM((N_EXPERTS, LANES), jnp.float32),
                        pltpu.VMEM((2, MIX_T, BATCH, D_MODEL), jnp.float32),
                        pltpu.VMEM((TOP_K * MIX_ROWS * ROW_SUB, LANES), jnp.uint32),
                        pltpu.VMEM((SUBLANES, MIX_ROWS), jnp.int32),
                        pltpu.SMEM((SUBLANES, MIX_ROWS), jnp.int32),
                        pltpu.SemaphoreType.DMA((2,)),
                        pltpu.SemaphoreType.DMA,
                        pltpu.SemaphoreType.DMA],
        compiler_params=pltpu.CompilerParams(dimension_semantics=("arbitrary",),
                                             vmem_limit_bytes=VMEM_LIMIT),
        name="mixer",
    )(x, *consts)


def _experts_kernel(blk_ref, bexp_ref, nval_ref, ntot_ref, x_ref, wg_ref, wu_ref, wd_ref, y_hbm,
                    wgu_b, wd_b, obuf, addr_v, addr_s, row_sem, idx_sem):
    i = pl.program_id(0)
    slot = i % 2
    prev = 1 - slot
    n_used = ntot_ref[0]
    idx_copy = pltpu.make_async_copy(addr_v, addr_s.at[slot], idx_sem)
    prev_idx_copy = pltpu.make_async_copy(addr_v, addr_s.at[prev], idx_sem)

    def round_copy(s):
        return pltpu.make_async_copy(obuf.at[s], y_hbm.at[pl.ds(0, MOE_BM * DATA_SUB)], row_sem.at[s])

    def scatter_rows(first, count):
        for r in range(first, first + count):
            d = pl.multiple_of(addr_s[prev, 0, r], DATA_SUB)
            pltpu.make_async_copy(obuf.at[prev, pl.ds(r * DATA_SUB, DATA_SUB)],
                                  y_hbm.at[pl.ds(d, DATA_SUB)], row_sem.at[prev]).start(priority=r % 2)

    def dump_rows(s):
        return (Y_DUMP + s * MOE_BM + lax.broadcasted_iota(jnp.int32, (MOE_BM, LANES), 0)) * DATA_SUB

    @pl.when(i == 0)
    def _():
        obuf[...] = jnp.zeros_like(obuf)
        addr_v[...] = jnp.transpose(dump_rows(1))[0:SUBLANES, :]
        prev_idx_copy.start()

    @pl.when((i < n_used) & ((i == 0) | (bexp_ref[i] != bexp_ref[jnp.maximum(i - 1, 0)])))
    def _():
        wgu_b[:, 0:EXPERT_FF] = wg_ref[0].astype(jnp.bfloat16)
        wgu_b[:, EXPERT_FF:] = wu_ref[0].astype(jnp.bfloat16)
        wd_b[...] = wd_ref[0].astype(jnp.bfloat16)

    @pl.when(i < n_used)
    def _():
        prev_idx_copy.wait()
        meta = lax.bitcast_convert_type(x_ref[pl.ds(DATA_SUB, MOE_BM, stride=ROW_SUB), :], jnp.int32)
        row = lax.broadcasted_iota(jnp.int32, (MOE_BM, LANES), 0)
        lane = lax.broadcasted_iota(jnp.int32, (MOE_BM, LANES), 1)
        ret = jnp.where(row < nval_ref[i], meta, dump_rows(slot))
        addr_v[...] = jnp.transpose(jnp.where(lane == 0, ret, 0))[0:SUBLANES, :]
        idx_copy.start()

        def compute(nrows):
            batch = MOE_BM // 4
            parts = [_unpack_bf16_pairs(x_ref[pl.ds(c, nrows, stride=ROW_SUB), :]) for c in range(DATA_SUB)]
            x = jnp.concatenate([p[0] for p in parts] + [p[1] for p in parts], axis=-1)
            scatter_rows(0, batch)
            gu = _bdot(x, wgu_b[...])
            scatter_rows(batch, batch)
            g = gu[:, :EXPERT_FF]
            hid = (g * _sigmoid(g)) * gu[:, EXPERT_FF:]
            scatter_rows(2 * batch, batch)
            gate = lax.bitcast_convert_type(meta[0:nrows, 1:2], jnp.float32)
            o = _bdot(hid, wd_b[...]) * gate
            scatter_rows(3 * batch, batch)
            packed = _pack_bf16_pairs(o[:, :HALF], o[:, HALF:])

            @pl.when(i >= 1)
            def _():
                round_copy(slot).wait()

            for c in range(DATA_SUB):
                obuf[slot, pl.ds(c, nrows, stride=DATA_SUB), :] = packed[:, c * LANES:(c + 1) * LANES]

        half_full = nval_ref[i] <= MOE_BM // 2

        @pl.when(half_full)
        def _():
            compute(MOE_BM // 2)

        @pl.when(jnp.logical_not(half_full))
        def _():
            compute(MOE_BM)

    @pl.when(i == n_used)
    def _():
        prev_idx_copy.wait()
        scatter_rows(0, MOE_BM)

    @pl.when(i == pl.num_programs(0) - 1)
    def _():
        round_copy(n_used % 2).wait()
        round_copy((n_used - 1) % 2).wait()


def _experts(blk_idx, blk_expert, blk_nvalid, n_used, xs, w_gate, w_up, w_down):
    wmap = lambda i, blk, bexp, nval, ntot: (bexp[i], 0, 0)
    return pl.pallas_call(
        _experts_kernel,
        grid_spec=pltpu.PrefetchScalarGridSpec(
            num_scalar_prefetch=4,
            grid=(N_BLOCKS + 1,),
            in_specs=[pl.BlockSpec((MOE_BM * ROW_SUB, LANES),
                                   lambda i, blk, bexp, nval, ntot: (blk[i], 0)),
                      pl.BlockSpec((1, D_MODEL, EXPERT_FF), wmap),
                      pl.BlockSpec((1, D_MODEL, EXPERT_FF), wmap),
                      pl.BlockSpec((1, EXPERT_FF, D_MODEL), wmap)],
            out_specs=pl.BlockSpec(memory_space=pl.ANY),
            scratch_shapes=[pltpu.VMEM((D_MODEL, 2 * EXPERT_FF), jnp.bfloat16),
                            pltpu.VMEM((EXPERT_FF, D_MODEL), jnp.bfloat16),
                            pltpu.VMEM((2, MOE_BM * DATA_SUB, LANES), jnp.uint32),
                            pltpu.VMEM((SUBLANES, MOE_BM), jnp.int32),
                            pltpu.SMEM((2, SUBLANES, MOE_BM), jnp.int32),
                            pltpu.SemaphoreType.DMA((2,)),
                            pltpu.SemaphoreType.DMA]),
        out_shape=jax.ShapeDtypeStruct((Y_ROWS * DATA_SUB, LANES), jnp.uint32),
        compiler_params=pltpu.CompilerParams(dimension_semantics=("arbitrary",),
                                             vmem_limit_bytes=VMEM_LIMIT),
        name="experts",
    )(blk_idx, blk_expert, blk_nvalid, n_used, xs, w_gate, w_up, w_down)


def _final_kernel(h1_ref, y_ref, gain_ref, out_hbm, obuf, sem):
    i = pl.program_id(0)
    slot = i % 2
    stride = TOP_K * DATA_SUB

    def out_copy(step, s, b):
        return pltpu.make_async_copy(obuf.at[s, :, b, :],
                                     out_hbm.at[b, pl.ds(step * FIN_T, FIN_T), :], sem.at[s])

    lo, hi = [], []
    for c in range(DATA_SUB):
        y0 = _unpack_bf16_pairs(y_ref[pl.ds(c, FIN_TM, stride=stride), :])
        y1 = _unpack_bf16_pairs(y_ref[pl.ds(DATA_SUB + c, FIN_TM, stride=stride), :])
        lo.append(y0[0] + y1[0])
        hi.append(y0[1] + y1[1])
    y = jnp.concatenate(lo + hi, axis=-1)
    res = _rms(h1_ref[...] + y, gain_ref[...])

    @pl.when(i >= 2)
    def _():
        for b in range(BATCH):
            out_copy(i - 2, slot, b).wait()

    obuf[slot] = res.reshape(FIN_T, BATCH, D_MODEL)
    for b in range(BATCH):
        out_copy(i, slot, b).start()

    @pl.when(i == pl.num_programs(0) - 1)
    def _():
        for b in range(BATCH):
            out_copy(i - 1, 1 - slot, b).wait()
        for b in range(BATCH):
            out_copy(i, slot, b).wait()


def _final(h1, y2, final_gain):
    rows = FIN_TM * TOP_K * DATA_SUB
    return pl.pallas_call(
        _final_kernel,
        grid=(N_TOK // FIN_TM,),
        in_specs=[pl.BlockSpec((FIN_TM, D_MODEL), lambda i: (i, 0)),
                  pl.BlockSpec((rows, LANES), lambda i: (i, 0)),
                  pl.BlockSpec((1, D_MODEL), lambda i: (0, 0))],
        out_specs=pl.BlockSpec(memory_space=pl.ANY),
        out_shape=jax.ShapeDtypeStruct((BATCH, SEQ, D_MODEL), jnp.float32),
        scratch_shapes=[pltpu.VMEM((2, FIN_T, BATCH, D_MODEL), jnp.float32),
                        pltpu.SemaphoreType.DMA((2,))],
        compiler_params=pltpu.CompilerParams(dimension_semantics=("arbitrary",),
                                             vmem_limit_bytes=VMEM_LIMIT),
        name="final",
    )(h1, y2, final_gain)


def _block_diag(blocks):
    n, r, c = blocks.shape
    on_diag = jnp.eye(n, dtype=bool)[:, None, :, None]
    return jnp.where(on_diag, blocks[:, :, None, :], 0).reshape(n * r, n * c)


def kernel(x, meta_tokens, norm1_gain, w_in, conv_w, conv_b, lru_wa, lru_ba, lru_wx, lru_bx,
           lru_lambda, lru_out_gain, pool_w, pool_scale, w_out, norm2_gain, w_group, b_group,
           w_router, b_router, w_gate, w_up, w_down, final_gain):
    assert x.shape == (BATCH, SEQ, D_MODEL) and norm1_gain.shape[0] == 1
    f32, bf16 = jnp.float32, jnp.bfloat16
    row = lambda v: v.reshape(1, -1).astype(f32)

    meta_t = jnp.repeat(meta_tokens.astype(f32), BATCH, axis=0)

    heads_per = LRU_HEADS // 2
    w_gate_blk = jnp.stack([
        jnp.concatenate([_block_diag(lru_wa[0, j * heads_per:(j + 1) * heads_per]),
                         _block_diag(lru_wx[0, j * heads_per:(j + 1) * heads_per])],
                        axis=1) for j in range(2)]).astype(bf16)
    b_gate_blk = jnp.stack([lru_ba[0], lru_bx[0]]).astype(f32)
    w_pool_blk = jnp.stack([_block_diag(pool_w[0, 2 * j:2 * j + 2])
                            for j in range(2)]).astype(bf16)
    lam8 = row(LRU_C * jax.nn.log_sigmoid(lru_lambda[0].astype(f32)))
    pad = RT_ROWS - N_EXPERTS - N_GROUPS
    w_rt = jnp.concatenate([w_router[0], w_group[0], jnp.zeros((D_MODEL, pad), f32)], axis=1).T.astype(bf16)
    b_rt = jnp.broadcast_to(jnp.concatenate([b_router[0], b_group[0], jnp.zeros((pad,), f32)])[:, None],
                            (RT_ROWS, LANES)).astype(f32)
    ridx = jnp.arange(MIX_ROWS)
    tri = (ridx[:, None] < ridx[None, :]).astype(bf16)

    h1, xs, cnt = _mixer(
        x, meta_t, row(norm1_gain[0]), w_in[0].astype(bf16), conv_w[0].astype(f32),
        row(conv_b[0]), w_gate_blk, b_gate_blk, lam8, row(lru_out_gain[0]), w_pool_blk,
        row(pool_scale[0]), w_out[0].astype(bf16), row(norm2_gain[0]), w_rt, b_rt, tri)

    counts = cnt[:, 0].astype(jnp.int32)
    nblk = (counts + MOE_BM - 1) // MOE_BM
    blk_end = jnp.cumsum(nblk)
    step = jnp.arange(N_BLOCKS + 1, dtype=jnp.int32)
    used = step < blk_end[-1]
    step_c = jnp.minimum(step, blk_end[-1] - 1)
    e_of = jnp.sum(blk_end[None, :] <= step_c[:, None], axis=-1).astype(jnp.int32)
    onehot_e = e_of[:, None] == jnp.arange(N_EXPERTS, dtype=jnp.int32)[None, :]
    j_of = step_c - jnp.sum(jnp.where(onehot_e, (blk_end - nblk)[None, :], 0), axis=-1)
    cnt_of = jnp.sum(jnp.where(onehot_e, counts[None, :], 0), axis=-1)
    blk_idx = (e_of * CAP_BLOCKS + j_of).astype(jnp.int32)
    blk_nvalid = jnp.where(used, jnp.clip(cnt_of - j_of * MOE_BM, 0, MOE_BM), 0).astype(jnp.int32)

    y2 = _experts(blk_idx, e_of, blk_nvalid, blk_end[-1:].astype(jnp.int32), xs,
                  w_gate[0], w_up[0], w_down[0])
    return _final(h1, y2, row(final_gain))
```

```python
import functools

import jax
import jax.numpy as jnp
from jax import lax
from jax.experimental import pallas as pl
from jax.experimental.pallas import tpu as pltpu

D_MODEL = 1024
BATCH = 8
SEQ = 4096
N_META = 16
LRU_WIDTH = 512
LRU_HEADS = 8
LRU_HEAD_DIM = 64
CONV_WIDTH = 4
LRU_C = 8.0
POOL_WIDTH = 512
POOL_WINDOWS = (2, 4, 8, 16)
POOL_GROUP = 128
N_GROUPS = 4
EXPERTS_PER_GROUP = 8
N_EXPERTS = 32
TOP_K = 2
EXPERT_FF = 512
RMS_EPS = 1e-6
SQRT_FLOOR = 1e-37

N_TOK = BATCH * SEQ
SUBLANES = 8
LANES = 128
MIX_ROWS = 512
META_ROWS = N_META * BATCH
CONV_HALO = (CONV_WIDTH - 1) * BATCH
POOL_HALO = (max(POOL_WINDOWS) - 1) * BATCH
MIX_STEPS = N_TOK // MIX_ROWS
MIX_T = MIX_ROWS // BATCH
MOE_BM = 512
N_ASSIGN = N_TOK * TOP_K
N_BLOCKS = N_ASSIGN // MOE_BM + N_EXPERTS
EXPERT_CAP = N_TOK
CAP_BLOCKS = EXPERT_CAP // MOE_BM
ROW_SUB = 8
DATA_SUB = 4
HALF = D_MODEL // 2
XS_DUMP = N_EXPERTS * EXPERT_CAP
XS_ROWS = XS_DUMP + TOP_K * MIX_ROWS
SCATTER_BATCH = TOP_K * MIX_ROWS // 8
Y_DUMP = N_ASSIGN
Y_ROWS = Y_DUMP + 2 * MOE_BM
FIN_TM = 1024
FIN_T = FIN_TM // BATCH
RT_ROWS = 40
VMEM_LIMIT = 56 * 1024 * 1024


def _rms(x, gain):
    return x * lax.rsqrt(jnp.mean(x * x, axis=-1, keepdims=True) + RMS_EPS) * gain


def _sigmoid(x):
    return 0.5 * jnp.tanh(0.5 * x) + 0.5


def _bdot(a, b):
    return jnp.dot(a.astype(jnp.bfloat16), b, preferred_element_type=jnp.float32)


def _pack_bf16_pairs(lo, hi):
    lo_b = lax.bitcast_convert_type(lo.astype(jnp.bfloat16).astype(jnp.float32), jnp.uint32)
    hi_b = lax.bitcast_convert_type(hi.astype(jnp.bfloat16).astype(jnp.float32), jnp.uint32)
    return (lo_b >> 16) | (hi_b & jnp.uint32(0xFFFF0000))


def _unpack_bf16_pairs(u):
    lo = lax.bitcast_convert_type(u << 16, jnp.float32)
    hi = lax.bitcast_convert_type(u & jnp.uint32(0xFFFF0000), jnp.float32)
    return lo, hi


def _mix_rows(x, nrows, with_output, g1_ref, w_in_ref, conv_w_ref, conv_b_ref, w_gate_ref,
              b_gate_ref, lam8_ref, lru_gain_ref, w_pool_ref, pool_scale_ref, w_out_ref,
              ux_buf, up_buf, a_buf, b_buf, hs_buf, hstate, between=lambda: None):
    hn = _rms(x, g1_ref[...])
    proj = _bdot(hn, w_in_ref[...])
    between()
    ux = proj[:, :LRU_WIDTH]
    up = proj[:, 2 * LRU_WIDTH:]

    ux_buf[CONV_HALO:CONV_HALO + nrows, :] = ux
    xc = conv_b_ref[...] + conv_w_ref[3:4, :] * ux
    for k in range(CONV_WIDTH - 1):
        xc = xc + conv_w_ref[k:k + 1, :] * ux_buf[k * BATCH:k * BATCH + nrows, :]
    ux_buf[0:CONV_HALO, :] = ux_buf[nrows:nrows + CONV_HALO, :]
    between()

    half = LRU_WIDTH // 2
    z = [_bdot(xc[:, j * half:(j + 1) * half], w_gate_ref[j]) for j in range(2)]
    za = jnp.concatenate([z[0][:, :half], z[1][:, :half]], axis=-1) + b_gate_ref[0:1, :]
    zx = jnp.concatenate([z[0][:, half:], z[1][:, half:]], axis=-1) + b_gate_ref[1:2, :]
    r = _sigmoid(za)
    gi = _sigmoid(zx)
    log_a = lam8_ref[...] * r
    a = jnp.exp(log_a)
    a_buf[0:nrows, :] = a
    v = jnp.tanh(-log_a) * (1.0 + a * a)
    root = v * lax.rsqrt(jnp.maximum(v, SQRT_FLOOR))
    b_buf[0:nrows, :] = root * (gi * xc)
    between()

    def step(t, h):
        r0 = pl.multiple_of(t * SUBLANES, SUBLANES)
        h = a_buf[pl.ds(r0, SUBLANES), :] * h + b_buf[pl.ds(r0, SUBLANES), :]
        hs_buf[pl.ds(r0, SUBLANES), :] = h
        return h

    hstate[...] = lax.fori_loop(0, nrows // SUBLANES, step, hstate[...], unroll=True)

    up_buf[POOL_HALO:POOL_HALO + nrows, :] = up
    if not with_output:
        up_buf[0:POOL_HALO, :] = up_buf[nrows:nrows + POOL_HALO, :]
        return None

    ug = proj[:, LRU_WIDTH:2 * LRU_WIDTH]
    y_lru = _rms(hs_buf[0:nrows, :] * jax.nn.gelu(ug), lru_gain_ref[...])

    pm = []
    for g, w in enumerate(POOL_WINDOWS):
        lo, hi = g * POOL_GROUP, (g + 1) * POOL_GROUP
        s = up_buf[POOL_HALO - (w - 1) * BATCH:POOL_HALO + nrows, lo:hi]
        m = 1
        while m < w:
            s = s[m * BATCH:, :] + s[:-m * BATCH, :]
            m *= 2
        pm.append(s * (1.0 / w) - up[:, lo:hi])
    up_buf[0:POOL_HALO, :] = up_buf[nrows:nrows + POOL_HALO, :]
    yp = [_bdot(jnp.concatenate(pm[2 * j:2 * j + 2], axis=-1), w_pool_ref[j]) for j in range(2)]
    y_pool = _rms(jnp.concatenate(yp, axis=-1), pool_scale_ref[...])
    between()

    y = _bdot(jnp.concatenate([y_lru, y_pool], axis=-1), w_out_ref[...])
    between()
    return x + y


def _mixer_kernel(x_hbm, meta_ref, g1_ref, w_in_ref, conv_w_ref, conv_b_ref, w_gate_ref,
                  b_gate_ref, lam8_ref, lru_gain_ref, w_pool_ref, pool_scale_ref, w_out_ref,
                  g2_ref, w_rt_ref, b_rt_ref, tri_ref,
                  h1_ref, xs_hbm, cnt_ref,
                  ux_buf, up_buf, a_buf, b_buf, hs_buf, hstate, running,
                  xin, rowbuf, dest_v, dest_s, in_sem, row_sem, idx_sem):
    i = pl.program_id(0)
    slot = i % 2
    n_rows = TOP_K * MIX_ROWS
    mix = functools.partial(
        _mix_rows, g1_ref=g1_ref, w_in_ref=w_in_ref, conv_w_ref=conv_w_ref,
        conv_b_ref=conv_b_ref, w_gate_ref=w_gate_ref, b_gate_ref=b_gate_ref, lam8_ref=lam8_ref,
        lru_gain_ref=lru_gain_ref, w_pool_ref=w_pool_ref, pool_scale_ref=pool_scale_ref,
        w_out_ref=w_out_ref, ux_buf=ux_buf, up_buf=up_buf, a_buf=a_buf, b_buf=b_buf,
        hs_buf=hs_buf, hstate=hstate)
    idx_copy = pltpu.make_async_copy(dest_v, dest_s, idx_sem)
    round_copy = pltpu.make_async_copy(rowbuf, xs_hbm.at[pl.ds(0, n_rows * ROW_SUB)], row_sem)

    def x_copy(step, s, b):
        return pltpu.make_async_copy(x_hbm.at[b, pl.ds(step * MIX_T, MIX_T), :],
                                     xin.at[s, :, b, :], in_sem.at[s])

    def scatter_rows(first, count):
        for n in range(first, first + count):
            k, r = divmod(n, MIX_ROWS)
            d = pl.multiple_of(dest_s[k, r], ROW_SUB)
            pltpu.make_async_copy(rowbuf.at[pl.ds(n * ROW_SUB, ROW_SUB)],
                                  xs_hbm.at[pl.ds(d, ROW_SUB)], row_sem).start(priority=n % 2)

    issued = [0]

    def scatter_batch():
        scatter_rows(issued[0], SCATTER_BATCH)
        issued[0] += SCATTER_BATCH

    @pl.when(i == 0)
    def _():
        for b in range(BATCH):
            x_copy(0, 0, b).start()
        ux_buf[0:CONV_HALO, :] = jnp.zeros((CONV_HALO, LRU_WIDTH), jnp.float32)
        up_buf[0:POOL_HALO, :] = jnp.zeros((POOL_HALO, POOL_WIDTH), jnp.float32)
        hstate[...] = jnp.zeros_like(hstate)
        running[...] = jnp.zeros_like(running)
        mix(meta_ref[...], META_ROWS, False)
        rowbuf[...] = jnp.zeros_like(rowbuf)
        col = lax.broadcasted_iota(jnp.int32, (SUBLANES, MIX_ROWS), 1)
        sub = lax.broadcasted_iota(jnp.int32, (SUBLANES, MIX_ROWS), 0)
        dest_v[...] = (XS_DUMP + jnp.minimum(sub, TOP_K - 1) * MIX_ROWS + col) * ROW_SUB
        idx_copy.start()

    @pl.when(i + 1 < MIX_STEPS)
    def _():
        for b in range(BATCH):
            x_copy(i + 1, 1 - slot, b).start()

    for b in range(BATCH):
        x_copy(i, slot, b).wait()
    idx_copy.wait()

    @pl.when(i >= 0)
    def _():
        x = xin[slot].reshape(MIX_ROWS, D_MODEL)
        h1_ref[...] = mix(x, MIX_ROWS, True, between=scatter_batch)

    h1 = h1_ref[...]
    hn2 = _rms(h1, g2_ref[...])
    logits = jnp.transpose(_bdot(hn2, w_rt_ref[...]) + b_rt_ref[...])[0:RT_ROWS, :]
    scatter_batch()
    neg = jnp.float32(-jnp.inf)
    big = jnp.float32(4 * LANES)

    gl = logits[N_EXPERTS:N_EXPERTS + N_GROUPS, :]
    grow = lax.broadcasted_iota(jnp.int32, (N_GROUPS, MIX_ROWS), 0).astype(jnp.float32)
    gmax = jnp.max(gl, axis=0, keepdims=True)
    grp = jnp.min(jnp.where(gl == gmax, grow, big), axis=0, keepdims=True).astype(jnp.int32)
    p_g = 1.0 / jnp.sum(jnp.exp(gl - gmax), axis=0, keepdims=True)

    erow = lax.broadcasted_iota(jnp.int32, (N_EXPERTS, MIX_ROWS), 0)
    erow_f = erow.astype(jnp.float32)
    el = jnp.where((erow >> 3) == grp, logits[0:N_EXPERTS, :], neg)
    m1 = jnp.max(el, axis=0, keepdims=True)
    i1 = jnp.min(jnp.where(el == m1, erow_f, big), axis=0, keepdims=True)
    el2 = jnp.where(erow_f == i1, neg, el)
    m2 = jnp.max(el2, axis=0, keepdims=True)
    i2 = jnp.min(jnp.where(el2 == m2, erow_f, big), axis=0, keepdims=True)
    e21 = jnp.exp(m2 - m1)
    den = 1.0 + e21
    gate_rows = (p_g * (1.0 / den), p_g * (e21 / den))
    scatter_batch()

    sel = (erow_f == i1, erow_f == i2)
    onehot = jnp.where(sel[0] | sel[1], 1.0, 0.0)
    before = jnp.dot(onehot.astype(jnp.bfloat16), tri_ref[...],
                     preferred_element_type=jnp.float32) + running[:, 0:1]
    rank = [jnp.sum(jnp.where(sel[k], before, 0.0), axis=0, keepdims=True) for k in range(TOP_K)]
    running[...] = running[...] + jnp.sum(onehot, axis=1, keepdims=True)
    cnt_ref[...] = running[...]
    scatter_batch()
    assert issued[0] == n_rows

    for k, e_k in enumerate((i1, i2)):
        dest_v[k:k + 1, :] = (e_k.astype(jnp.int32) * EXPERT_CAP + rank[k].astype(jnp.int32)) * ROW_SUB
    idx_copy.start()

    gate_tile = jnp.concatenate(list(gate_rows) + [jnp.zeros((LANES - TOP_K, MIX_ROWS), jnp.float32)], axis=0)
    gate_cols = jnp.transpose(gate_tile)
    gate = (gate_cols[:, 0:1], gate_cols[:, 1:2])
    lane = lax.broadcasted_iota(jnp.int32, (MIX_ROWS, LANES), 1)

    data = _pack_bf16_pairs(hn2[:, :HALF], hn2[:, HALF:])
    tok = i * MIX_ROWS + lax.broadcasted_iota(jnp.int32, (MIX_ROWS, LANES), 0)

    round_copy.wait()
    for k in range(TOP_K):
        base = k * MIX_ROWS * ROW_SUB
        for c in range(DATA_SUB):
            rowbuf[pl.ds(base + c, MIX_ROWS, stride=ROW_SUB), :] = data[:, c * LANES:(c + 1) * LANES]
        ret = (tok * TOP_K + k) * DATA_SUB
        gbits = lax.bitcast_convert_type(jnp.broadcast_to(gate[k], (MIX_ROWS, LANES)), jnp.int32)
        meta = jnp.where(lane == 0, ret, jnp.where(lane == 1, gbits, 0))
        rowbuf[pl.ds(base + DATA_SUB, MIX_ROWS, stride=ROW_SUB), :] = lax.bitcast_convert_type(
            meta, jnp.uint32)

    @pl.when(i == MIX_STEPS - 1)
    def _():
        idx_copy.wait()
        scatter_rows(0, n_rows)
        round_copy.wait()


def _mixer(x, meta_t, g1, w_in, conv_w, conv_b, w_gate, b_gate, lam8, lru_gain, w_pool,
           pool_scale, w_out, g2, w_rt, b_rt, tri):
    full = lambda a: pl.BlockSpec(a.shape, lambda i: (0,) * a.ndim)
    consts = (meta_t, g1, w_in, conv_w, conv_b, w_gate, b_gate, lam8, lru_gain, w_pool,
              pool_scale, w_out, g2, w_rt, b_rt, tri)
    return pl.pallas_call(
        _mixer_kernel,
        grid=(MIX_STEPS,),
        in_specs=[pl.BlockSpec(memory_space=pl.ANY)] + [full(a) for a in consts],
        out_specs=[pl.BlockSpec((MIX_ROWS, D_MODEL), lambda i: (i, 0)),
                   pl.BlockSpec(memory_space=pl.ANY),
                   pl.BlockSpec((N_EXPERTS, LANES), lambda i: (0, 0))],
        out_shape=[jax.ShapeDtypeStruct((N_TOK, D_MODEL), jnp.float32),
                   jax.ShapeDtypeStruct((XS_ROWS * ROW_SUB, LANES), jnp.uint32),
                   jax.ShapeDtypeStruct((N_EXPERTS, LANES), jnp.float32)],
        scratch_shapes=[pltpu.VMEM((CONV_HALO + MIX_ROWS, LRU_WIDTH), jnp.float32),
                        pltpu.VMEM((POOL_HALO + MIX_ROWS, POOL_WIDTH), jnp.float32),
                        pltpu.VMEM((MIX_ROWS, LRU_WIDTH), jnp.float32),
                        pltpu.VMEM((MIX_ROWS, LRU_WIDTH), jnp.float32),
                        pltpu.VMEM((MIX_ROWS, LRU_WIDTH), jnp.float32),
                        pltpu.VMEM((SUBLANES, LRU_WIDTH), jnp.float32),
                        pltpu.VMEM((N_EXPERTS, LANES), jnp.float32),
                        pltpu.VMEM((2, MIX_T, BATCH, D_MODEL), jnp.float32),
                        pltpu.VMEM((TOP_K * MIX_ROWS * ROW_SUB, LANES), jnp.uint32),
                        pltpu.VMEM((SUBLANES, MIX_ROWS), jnp.int32),
                        pltpu.SMEM((SUBLANES, MIX_ROWS), jnp.int32),
                        pltpu.SemaphoreType.DMA((2,)),
                        pltpu.SemaphoreType.DMA,
                        pltpu.SemaphoreType.DMA],
        compiler_params=pltpu.CompilerParams(dimension_semantics=("arbitrary",),
                                             vmem_limit_bytes=VMEM_LIMIT),
        name="mixer",
    )(x, *consts)


def _experts_kernel(blk_ref, bexp_ref, nval_ref, ntot_ref, x_ref, wg_ref, wu_ref, wd_ref, y_hbm,
                    wgu_b, wd_b, obuf, addr_v, addr_s, row_sem, idx_sem):
    i = pl.program_id(0)
    slot = i % 2
    prev = 1 - slot
    n_used = ntot_ref[0]
    idx_copy = pltpu.make_async_copy(addr_v, addr_s.at[slot], idx_sem)
    prev_idx_copy = pltpu.make_async_copy(addr_v, addr_s.at[prev], idx_sem)

    def round_copy(s):
        return pltpu.make_async_copy(obuf.at[s], y_hbm.at[pl.ds(0, MOE_BM * DATA_SUB)], row_sem.at[s])

    def scatter_rows(first, count):
        for r in range(first, first + count):
            d = pl.multiple_of(addr_s[prev, 0, r], DATA_SUB)
            pltpu.make_async_copy(obuf.at[prev, pl.ds(r * DATA_SUB, DATA_SUB)],
                                  y_hbm.at[pl.ds(d, DATA_SUB)], row_sem.at[prev]).start(priority=r % 2)

    def dump_rows(s):
        return (Y_DUMP + s * MOE_BM + lax.broadcasted_iota(jnp.int32, (MOE_BM, LANES), 0)) * DATA_SUB

    @pl.when(i == 0)
    def _():
        obuf[...] = jnp.zeros_like(obuf)
        addr_v[...] = jnp.transpose(dump_rows(1))[0:SUBLANES, :]
        prev_idx_copy.start()

    @pl.when((i < n_used) & ((i == 0) | (bexp_ref[i] != bexp_ref[jnp.maximum(i - 1, 0)])))
    def _():
        wgu_b[:, 0:EXPERT_FF] = wg_ref[0].astype(jnp.bfloat16)
        wgu_b[:, EXPERT_FF:] = wu_ref[0].astype(jnp.bfloat16)
        wd_b[...] = wd_ref[0].astype(jnp.bfloat16)

    @pl.when(i < n_used)
    def _():
        prev_idx_copy.wait()
        meta = lax.bitcast_convert_type(x_ref[pl.ds(DATA_SUB, MOE_BM, stride=ROW_SUB), :], jnp.int32)
        row = lax.broadcasted_iota(jnp.int32, (MOE_BM, LANES), 0)
        lane = lax.broadcasted_iota(jnp.int32, (MOE_BM, LANES), 1)
        ret = jnp.where(row < nval_ref[i], meta, dump_rows(slot))
        addr_v[...] = jnp.transpose(jnp.where(lane == 0, ret, 0))[0:SUBLANES, :]
        idx_copy.start()

        def compute(nrows):
            batch = MOE_BM // 4
            parts = [_unpack_bf16_pairs(x_ref[pl.ds(c, nrows, stride=ROW_SUB), :]) for c in range(DATA_SUB)]
            x = jnp.concatenate([p[0] for p in parts] + [p[1] for p in parts], axis=-1)
            scatter_rows(0, batch)
            gu = _bdot(x, wgu_b[...])
            scatter_rows(batch, batch)
            g = gu[:, :EXPERT_FF]
            hid = (g * _sigmoid(g)) * gu[:, EXPERT_FF:]
            scatter_rows(2 * batch, batch)
            gate = lax.bitcast_convert_type(meta[0:nrows, 1:2], jnp.float32)
            o = _bdot(hid, wd_b[...]) * gate
            scatter_rows(3 * batch, batch)
            packed = _pack_bf16_pairs(o[:, :HALF], o[:, HALF:])

            @pl.when(i >= 1)
            def _():
                round_copy(slot).wait()

            for c in range(DATA_SUB):
                obuf[slot, pl.ds(c, nrows, stride=DATA_SUB), :] = packed[:, c * LANES:(c + 1) * LANES]

        half_full = nval_ref[i] <= MOE_BM // 2

        @pl.when(half_full)
        def _():
            compute(MOE_BM // 2)

        @pl.when(jnp.logical_not(half_full))
        def _():
            compute(MOE_BM)

    @pl.when(i == n_used)
    def _():
        prev_idx_copy.wait()
        scatter_rows(0, MOE_BM)

    @pl.when(i == pl.num_programs(0) - 1)
    def _():
        round_copy(n_used % 2).wait()
        round_copy((n_used - 1) % 2).wait()


def _experts(blk_idx, blk_expert, blk_nvalid, n_used, xs, w_gate, w_up, w_down):
    wmap = lambda i, blk, bexp, nval, ntot: (bexp[i], 0, 0)
    return pl.pallas_call(
        _experts_kernel,
        grid_spec=pltpu.PrefetchScalarGridSpec(
            num_scalar_prefetch=4,
            grid=(N_BLOCKS + 1,),
            in_specs=[pl.BlockSpec((MOE_BM * ROW_SUB, LANES),
                                   lambda i, blk, bexp, nval, ntot: (blk[i], 0)),
                      pl.BlockSpec((1, D_MODEL, EXPERT_FF), wmap),
                      pl.BlockSpec((1, D_MODEL, EXPERT_FF), wmap),
                      pl.BlockSpec((1, EXPERT_FF, D_MODEL), wmap)],
            out_specs=pl.BlockSpec(memory_space=pl.ANY),
            scratch_shapes=[pltpu.VMEM((D_MODEL, 2 * EXPERT_FF), jnp.bfloat16),
                            pltpu.VMEM((EXPERT_FF, D_MODEL), jnp.bfloat16),
                            pltpu.VMEM((2, MOE_BM * DATA_SUB, LANES), jnp.uint32),
                            pltpu.VMEM((SUBLANES, MOE_BM), jnp.int32),
                            pltpu.SMEM((2, SUBLANES, MOE_BM), jnp.int32),
                            pltpu.SemaphoreType.DMA((2,)),
                            pltpu.SemaphoreType.DMA]),
        out_shape=jax.ShapeDtypeStruct((Y_ROWS * DATA_SUB, LANES), jnp.uint32),
        compiler_params=pltpu.CompilerParams(dimension_semantics=("arbitrary",),
                                             vmem_limit_bytes=VMEM_LIMIT),
        name="experts",
    )(blk_idx, blk_expert, blk_nvalid, n_used, xs, w_gate, w_up, w_down)


def _final_kernel(h1_ref, y_ref, gain_ref, out_hbm, obuf, sem):
    i = pl.program_id(0)
    slot = i % 2
    stride = TOP_K * DATA_SUB

    def out_copy(step, s, b):
        return pltpu.make_async_copy(obuf.at[s, :, b, :],
                                     out_hbm.at[b, pl.ds(step * FIN_T, FIN_T), :], sem.at[s])

    lo, hi = [], []
    for c in range(DATA_SUB):
        y0 = _unpack_bf16_pairs(y_ref[pl.ds(c, FIN_TM, stride=stride), :])
        y1 = _unpack_bf16_pairs(y_ref[pl.ds(DATA_SUB + c, FIN_TM, stride=stride), :])
        lo.append(y0[0] + y1[0])
        hi.append(y0[1] + y1[1])
    y = jnp.concatenate(lo + hi, axis=-1)
    res = _rms(h1_ref[...] + y, gain_ref[...])

    @pl.when(i >= 2)
    def _():
        for b in range(BATCH):
            out_copy(i - 2, slot, b).wait()

    obuf[slot] = res.reshape(FIN_T, BATCH, D_MODEL)
    for b in range(BATCH):
        out_copy(i, slot, b).start()

    @pl.when(i == pl.num_programs(0) - 1)
    def _():
        for b in range(BATCH):
            out_copy(i - 1, 1 - slot, b).wait()
        for b in range(BATCH):
            out_copy(i, slot, b).wait()


def _final(h1, y2, final_gain):
    rows = FIN_TM * TOP_K * DATA_SUB
    return pl.pallas_call(
        _final_kernel,
        grid=(N_TOK // FIN_TM,),
        in_specs=[pl.BlockSpec((FIN_TM, D_MODEL), lambda i: (i, 0)),
                  pl.BlockSpec((rows, LANES), lambda i: (i, 0)),
                  pl.BlockSpec((1, D_MODEL), lambda i: (0, 0))],
        out_specs=pl.BlockSpec(memory_space=pl.ANY),
        out_shape=jax.ShapeDtypeStruct((BATCH, SEQ, D_MODEL), jnp.float32),
        scratch_shapes=[pltpu.VMEM((2, FIN_T, BATCH, D_MODEL), jnp.float32),
                        pltpu.SemaphoreType.DMA((2,))],
        compiler_params=pltpu.CompilerParams(dimension_semantics=("arbitrary",),
                                             vmem_limit_bytes=VMEM_LIMIT),
        name="final",
    )(h1, y2, final_gain)


def _block_diag(blocks):
    n, r, c = blocks.shape
    on_diag = jnp.eye(n, dtype=bool)[:, None, :, None]
    return jnp.where(on_diag, blocks[:, :, None, :], 0).reshape(n * r, n * c)


def kernel(x, meta_tokens, norm1_gain, w_in, conv_w, conv_b, lru_wa, lru_ba, lru_wx, lru_bx,
           lru_lambda, lru_out_gain, pool_w, pool_scale, w_out, norm2_gain, w_group, b_group,
           w_router, b_router, w_gate, w_up, w_down, final_gain):
    assert x.shape == (BATCH, SEQ, D_MODEL) and norm1_gain.shape[0] == 1
    f32, bf16 = jnp.float32, jnp.bfloat16
    row = lambda v: v.reshape(1, -1).astype(f32)

    meta_t = jnp.repeat(meta_tokens.astype(f32), BATCH, axis=0)

    heads_per = LRU_HEADS // 2
    w_gate_blk = jnp.stack([
        jnp.concatenate([_block_diag(lru_wa[0, j * heads_per:(j + 1) * heads_per]),
                         _block_diag(lru_wx[0, j * heads_per:(j + 1) * heads_per])],
                        axis=1) for j in range(2)]).astype(bf16)
    b_gate_blk = jnp.stack([lru_ba[0], lru_bx[0]]).astype(f32)
    w_pool_blk = jnp.stack([_block_diag(pool_w[0, 2 * j:2 * j + 2])
                            for j in range(2)]).astype(bf16)
    lam8 = row(LRU_C * jax.nn.log_sigmoid(lru_lambda[0].astype(f32)))
    pad = LANES - N_EXPERTS - N_GROUPS
    w_rt = jnp.concatenate([w_router[0], w_group[0], jnp.zeros((D_MODEL, pad), f32)], axis=1).astype(bf16)
    b_rt = jnp.concatenate([b_router[0], b_group[0], jnp.zeros((pad,), f32)]).reshape(1, LANES).astype(f32)
    ridx = jnp.arange(MIX_ROWS)
    tri = (ridx[:, None] < ridx[None, :]).astype(bf16)

    h1, xs, cnt = _mixer(
        x, meta_t, row(norm1_gain[0]), w_in[0].astype(bf16), conv_w[0].astype(f32),
        row(conv_b[0]), w_gate_blk, b_gate_blk, lam8, row(lru_out_gain[0]), w_pool_blk,
        row(pool_scale[0]), w_out[0].astype(bf16), row(norm2_gain[0]), w_rt, b_rt, tri)

    counts = cnt[:, 0].astype(jnp.int32)
    nblk = (counts + MOE_BM - 1) // MOE_BM
    blk_end = jnp.cumsum(nblk)
    step = jnp.arange(N_BLOCKS + 1, dtype=jnp.int32)
    used = step < blk_end[-1]
    step_c = jnp.minimum(step, blk_end[-1] - 1)
    e_of = jnp.sum(blk_end[None, :] <= step_c[:, None], axis=-1).astype(jnp.int32)
    onehot_e = e_of[:, None] == jnp.arange(N_EXPERTS, dtype=jnp.int32)[None, :]
    j_of = step_c - jnp.sum(jnp.where(onehot_e, (blk_end - nblk)[None, :], 0), axis=-1)
    cnt_of = jnp.sum(jnp.where(onehot_e, counts[None, :], 0), axis=-1)
    blk_idx = (e_of * CAP_BLOCKS + j_of).astype(jnp.int32)
    blk_nvalid = jnp.where(used, jnp.clip(cnt_of - j_of * MOE_BM, 0, MOE_BM), 0).astype(jnp.int32)

    y2 = _experts(blk_idx, e_of, blk_nvalid, blk_end[-1:].astype(jnp.int32), xs,
                  w_gate[0], w_up[0], w_down[0])
    return _final(h1, y2, row(final_gain))
```

```python
import functools

import jax
import jax.numpy as jnp
from jax import lax
from jax.experimental import pallas as pl
from jax.experimental.pallas import tpu as pltpu

D_MODEL = 1024
BATCH = 8
SEQ = 4096
N_META = 16
LRU_WIDTH = 512
LRU_HEADS = 8
LRU_HEAD_DIM = 64
CONV_WIDTH = 4
LRU_C = 8.0
POOL_WIDTH = 512
POOL_WINDOWS = (2, 4, 8, 16)
POOL_GROUP = 128
N_GROUPS = 4
EXPERTS_PER_GROUP = 8
N_EXPERTS = 32
TOP_K = 2
EXPERT_FF = 512
RMS_EPS = 1e-6
SQRT_FLOOR = 1e-37

N_TOK = BATCH * SEQ
SUBLANES = 8
LANES = 128
MIX_ROWS = 512
META_ROWS = N_META * BATCH
CONV_HALO = (CONV_WIDTH - 1) * BATCH
POOL_HALO = (max(POOL_WINDOWS) - 1) * BATCH
MIX_STEPS = N_TOK // MIX_ROWS
MIX_T = MIX_ROWS // BATCH
MOE_BM = 512
N_ASSIGN = N_TOK * TOP_K
N_BLOCKS = N_ASSIGN // MOE_BM + N_EXPERTS
EXPERT_CAP = N_TOK
CAP_BLOCKS = EXPERT_CAP // MOE_BM
ROW_SUB = 8
DATA_SUB = 4
HALF = D_MODEL // 2
XS_DUMP = N_EXPERTS * EXPERT_CAP
XS_ROWS = XS_DUMP + TOP_K * MIX_ROWS
SCATTER_BATCH = TOP_K * MIX_ROWS // 8
Y_DUMP = N_ASSIGN
Y_ROWS = Y_DUMP + 2 * MOE_BM
FIN_TM = 1024
FIN_T = FIN_TM // BATCH
VMEM_LIMIT = 56 * 1024 * 1024


def _rms(x, gain):
    return x * lax.rsqrt(jnp.mean(x * x, axis=-1, keepdims=True) + RMS_EPS) * gain


def _sigmoid(x):
    return 0.5 * jnp.tanh(0.5 * x) + 0.5


def _bdot(a, b):
    return jnp.dot(a.astype(jnp.bfloat16), b, preferred_element_type=jnp.float32)


def _pack_bf16_pairs(lo, hi):
    lo_b = lax.bitcast_convert_type(lo.astype(jnp.bfloat16).astype(jnp.float32), jnp.uint32)
    hi_b = lax.bitcast_convert_type(hi.astype(jnp.bfloat16).astype(jnp.float32), jnp.uint32)
    return (lo_b >> 16) | (hi_b & jnp.uint32(0xFFFF0000))


def _unpack_bf16_pairs(u):
    lo = lax.bitcast_convert_type(u << 16, jnp.float32)
    hi = lax.bitcast_convert_type(u & jnp.uint32(0xFFFF0000), jnp.float32)
    return lo, hi


def _mix_rows(x, nrows, with_output, g1_ref, w_in_ref, conv_w_ref, conv_b_ref, w_gate_ref,
              b_gate_ref, lam8_ref, lru_gain_ref, w_pool_ref, pool_scale_ref, w_out_ref,
              ux_buf, up_buf, a_buf, b_buf, hs_buf, hstate, between=lambda: None):
    hn = _rms(x, g1_ref[...])
    proj = _bdot(hn, w_in_ref[...])
    between()
    ux = proj[:, :LRU_WIDTH]
    up = proj[:, 2 * LRU_WIDTH:]

    ux_buf[CONV_HALO:CONV_HALO + nrows, :] = ux
    xc = conv_b_ref[...] + conv_w_ref[3:4, :] * ux
    for k in range(CONV_WIDTH - 1):
        xc = xc + conv_w_ref[k:k + 1, :] * ux_buf[k * BATCH:k * BATCH + nrows, :]
    ux_buf[0:CONV_HALO, :] = ux_buf[nrows:nrows + CONV_HALO, :]
    between()

    half = LRU_WIDTH // 2
    z = [_bdot(xc[:, j * half:(j + 1) * half], w_gate_ref[j]) for j in range(2)]
    za = jnp.concatenate([z[0][:, :half], z[1][:, :half]], axis=-1) + b_gate_ref[0:1, :]
    zx = jnp.concatenate([z[0][:, half:], z[1][:, half:]], axis=-1) + b_gate_ref[1:2, :]
    r = _sigmoid(za)
    gi = _sigmoid(zx)
    log_a = lam8_ref[...] * r
    a = jnp.exp(log_a)
    a_buf[0:nrows, :] = a
    v = jnp.tanh(-log_a) * (1.0 + a * a)
    root = v * lax.rsqrt(jnp.maximum(v, SQRT_FLOOR))
    b_buf[0:nrows, :] = root * (gi * xc)
    between()

    def step(t, h):
        r0 = pl.multiple_of(t * SUBLANES, SUBLANES)
        h = a_buf[pl.ds(r0, SUBLANES), :] * h + b_buf[pl.ds(r0, SUBLANES), :]
        hs_buf[pl.ds(r0, SUBLANES), :] = h
        return h

    hstate[...] = lax.fori_loop(0, nrows // SUBLANES, step, hstate[...], unroll=True)

    up_buf[POOL_HALO:POOL_HALO + nrows, :] = up
    if not with_output:
        up_buf[0:POOL_HALO, :] = up_buf[nrows:nrows + POOL_HALO, :]
        return None

    ug = proj[:, LRU_WIDTH:2 * LRU_WIDTH]
    y_lru = _rms(hs_buf[0:nrows, :] * jax.nn.gelu(ug), lru_gain_ref[...])

    pm = []
    for g, w in enumerate(POOL_WINDOWS):
        lo, hi = g * POOL_GROUP, (g + 1) * POOL_GROUP
        s = up_buf[POOL_HALO - (w - 1) * BATCH:POOL_HALO + nrows, lo:hi]
        m = 1
        while m < w:
            s = s[m * BATCH:, :] + s[:-m * BATCH, :]
            m *= 2
        pm.append(s * (1.0 / w) - up[:, lo:hi])
    up_buf[0:POOL_HALO, :] = up_buf[nrows:nrows + POOL_HALO, :]
    yp = [_bdot(jnp.concatenate(pm[2 * j:2 * j + 2], axis=-1), w_pool_ref[j]) for j in range(2)]
    y_pool = _rms(jnp.concatenate(yp, axis=-1), pool_scale_ref[...])
    between()

    y = _bdot(jnp.concatenate([y_lru, y_pool], axis=-1), w_out_ref[...])
    between()
    return x + y


def _mixer_kernel(x_hbm, meta_ref, g1_ref, w_in_ref, conv_w_ref, conv_b_ref, w_gate_ref,
                  b_gate_ref, lam8_ref, lru_gain_ref, w_pool_ref, pool_scale_ref, w_out_ref,
                  g2_ref, w_rt_ref, b_rt_ref, tri_ref,
                  h1_ref, xs_hbm, cnt_ref,
                  ux_buf, up_buf, a_buf, b_buf, hs_buf, hstate, running,
                  xin, rowbuf, dest_v, dest_s, in_sem, row_sem, idx_sem):
    i = pl.program_id(0)
    slot = i % 2
    n_rows = TOP_K * MIX_ROWS
    mix = functools.partial(
        _mix_rows, g1_ref=g1_ref, w_in_ref=w_in_ref, conv_w_ref=conv_w_ref,
        conv_b_ref=conv_b_ref, w_gate_ref=w_gate_ref, b_gate_ref=b_gate_ref, lam8_ref=lam8_ref,
        lru_gain_ref=lru_gain_ref, w_pool_ref=w_pool_ref, pool_scale_ref=pool_scale_ref,
        w_out_ref=w_out_ref, ux_buf=ux_buf, up_buf=up_buf, a_buf=a_buf, b_buf=b_buf,
        hs_buf=hs_buf, hstate=hstate)
    idx_copy = pltpu.make_async_copy(dest_v, dest_s, idx_sem)

    def round_copy(s):
        return pltpu.make_async_copy(rowbuf.at[s], xs_hbm.at[pl.ds(0, n_rows * ROW_SUB)], row_sem.at[s])

    def x_copy(step, s, b):
        return pltpu.make_async_copy(x_hbm.at[b, pl.ds(step * MIX_T, MIX_T), :],
                                     xin.at[s, :, b, :], in_sem.at[s])

    def scatter_rows(first, count, s):
        for n in range(first, first + count):
            k, r = divmod(n, MIX_ROWS)
            d = pl.multiple_of(dest_s[k, r], ROW_SUB)
            pltpu.make_async_copy(rowbuf.at[s, pl.ds(n * ROW_SUB, ROW_SUB)],
                                  xs_hbm.at[pl.ds(d, ROW_SUB)], row_sem.at[s]).start(priority=n % 2)

    issued = [0]

    def scatter_batch():
        scatter_rows(issued[0], SCATTER_BATCH, 1 - slot)
        issued[0] += SCATTER_BATCH

    @pl.when(i == 0)
    def _():
        for b in range(BATCH):
            x_copy(0, 0, b).start()
        ux_buf[0:CONV_HALO, :] = jnp.zeros((CONV_HALO, LRU_WIDTH), jnp.float32)
        up_buf[0:POOL_HALO, :] = jnp.zeros((POOL_HALO, POOL_WIDTH), jnp.float32)
        hstate[...] = jnp.zeros_like(hstate)
        running[...] = jnp.zeros_like(running)
        mix(meta_ref[...], META_ROWS, False)
        rowbuf[...] = jnp.zeros_like(rowbuf)
        col = lax.broadcasted_iota(jnp.int32, (SUBLANES, MIX_ROWS), 1)
        sub = lax.broadcasted_iota(jnp.int32, (SUBLANES, MIX_ROWS), 0)
        dest_v[...] = (XS_DUMP + jnp.minimum(sub, TOP_K - 1) * MIX_ROWS + col) * ROW_SUB
        idx_copy.start()

    @pl.when(i + 1 < MIX_STEPS)
    def _():
        for b in range(BATCH):
            x_copy(i + 1, 1 - slot, b).start()

    for b in range(BATCH):
        x_copy(i, slot, b).wait()
    idx_copy.wait()

    @pl.when(i >= 0)
    def _():
        x = xin[slot].reshape(MIX_ROWS, D_MODEL)
        h1_ref[...] = mix(x, MIX_ROWS, True, between=scatter_batch)

    h1 = h1_ref[...]
    hn2 = _rms(h1, g2_ref[...])
    logits = _bdot(hn2, w_rt_ref[...]) + b_rt_ref[...]
    scatter_batch()
    lane = lax.broadcasted_iota(jnp.int32, (MIX_ROWS, LANES), 1)
    lane_f = lane.astype(jnp.float32)
    neg = jnp.float32(-jnp.inf)
    big = jnp.float32(4 * LANES)

    is_g = (lane >= N_EXPERTS) & (lane < N_EXPERTS + N_GROUPS)
    gl = jnp.where(is_g, logits, neg)
    gmax = jnp.max(gl, axis=-1, keepdims=True)
    gidx = jnp.min(jnp.where(gl == gmax, lane_f, big), axis=-1, keepdims=True)
    p_g = 1.0 / jnp.sum(jnp.exp(gl - gmax), axis=-1, keepdims=True)
    grp = gidx.astype(jnp.int32) - N_EXPERTS

    el = jnp.where((lane >> 3) == grp, logits, neg)
    m1 = jnp.max(el, axis=-1, keepdims=True)
    i1 = jnp.min(jnp.where(el == m1, lane_f, big), axis=-1, keepdims=True)
    el2 = jnp.where(lane_f == i1, neg, el)
    m2 = jnp.max(el2, axis=-1, keepdims=True)
    i2 = jnp.min(jnp.where(el2 == m2, lane_f, big), axis=-1, keepdims=True)
    e21 = jnp.exp(m2 - m1)
    den = 1.0 + e21
    gate = (p_g * (1.0 / den), p_g * (e21 / den))
    scatter_batch()

    sel = (lane_f == i1, lane_f == i2)
    onehot = jnp.where(sel[0] | sel[1], 1.0, 0.0)
    before = jnp.dot(tri_ref[...], onehot.astype(jnp.bfloat16),
                     preferred_element_type=jnp.float32) + running[0:1, :]
    rank = [jnp.sum(jnp.where(sel[k], before, 0.0), axis=-1, keepdims=True) for k in range(TOP_K)]
    running[0:1, :] = running[0:1, :] + jnp.sum(onehot, axis=0, keepdims=True)
    cnt_ref[...] = running[...]
    scatter_batch()
    assert issued[0] == n_rows

    eid = (i1.astype(jnp.int32), i2.astype(jnp.int32))
    dst = [(eid[k] * EXPERT_CAP + rank[k].astype(jnp.int32)) * ROW_SUB for k in range(TOP_K)]
    dst_tile = jnp.where(lane == 0, dst[0], jnp.where(lane == 1, dst[1], 0))
    dest_v[...] = jnp.transpose(dst_tile)[0:SUBLANES, :]
    idx_copy.start()

    data = _pack_bf16_pairs(hn2[:, :HALF], hn2[:, HALF:])
    tok = i * MIX_ROWS + lax.broadcasted_iota(jnp.int32, (MIX_ROWS, LANES), 0)

    @pl.when(i >= 1)
    def _():
        round_copy(slot).wait()

    for k in range(TOP_K):
        base = k * MIX_ROWS * ROW_SUB
        for c in range(DATA_SUB):
            rowbuf[slot, pl.ds(base + c, MIX_ROWS, stride=ROW_SUB), :] = data[:, c * LANES:(c + 1) * LANES]
        ret = (tok * TOP_K + k) * DATA_SUB
        gbits = lax.bitcast_convert_type(jnp.broadcast_to(gate[k], (MIX_ROWS, LANES)), jnp.int32)
        meta = jnp.where(lane == 0, ret, jnp.where(lane == 1, gbits, 0))
        rowbuf[slot, pl.ds(base + DATA_SUB, MIX_ROWS, stride=ROW_SUB), :] = lax.bitcast_convert_type(
            meta, jnp.uint32)

    @pl.when(i == MIX_STEPS - 1)
    def _():
        idx_copy.wait()
        scatter_rows(0, n_rows, slot)
        round_copy(1 - slot).wait()
        round_copy(slot).wait()


def _mixer(x, meta_t, g1, w_in, conv_w, conv_b, w_gate, b_gate, lam8, lru_gain, w_pool,
           pool_scale, w_out, g2, w_rt, b_rt, tri):
    full = lambda a: pl.BlockSpec(a.shape, lambda i: (0,) * a.ndim)
    consts = (meta_t, g1, w_in, conv_w, conv_b, w_gate, b_gate, lam8, lru_gain, w_pool,
              pool_scale, w_out, g2, w_rt, b_rt, tri)
    return pl.pallas_call(
        _mixer_kernel,
        grid=(MIX_STEPS,),
        in_specs=[pl.BlockSpec(memory_space=pl.ANY)] + [full(a) for a in consts],
        out_specs=[pl.BlockSpec((MIX_ROWS, D_MODEL), lambda i: (i, 0)),
                   pl.BlockSpec(memory_space=pl.ANY),
                   pl.BlockSpec((SUBLANES, LANES), lambda i: (0, 0))],
        out_shape=[jax.ShapeDtypeStruct((N_TOK, D_MODEL), jnp.float32),
                   jax.ShapeDtypeStruct((XS_ROWS * ROW_SUB, LANES), jnp.uint32),
                   jax.ShapeDtypeStruct((SUBLANES, LANES), jnp.float32)],
        scratch_shapes=[pltpu.VMEM((CONV_HALO + MIX_ROWS, LRU_WIDTH), jnp.float32),
                        pltpu.VMEM((POOL_HALO + MIX_ROWS, POOL_WIDTH), jnp.float32),
                        pltpu.VMEM((MIX_ROWS, LRU_WIDTH), jnp.float32),
                        pltpu.VMEM((MIX_ROWS, LRU_WIDTH), jnp.float32),
                        pltpu.VMEM((MIX_ROWS, LRU_WIDTH), jnp.float32),
                        pltpu.VMEM((SUBLANES, LRU_WIDTH), jnp.float32),
                        pltpu.VMEM((SUBLANES, LANES), jnp.float32),
                        pltpu.VMEM((2, MIX_T, BATCH, D_MODEL), jnp.float32),
                        pltpu.VMEM((2, TOP_K * MIX_ROWS * ROW_SUB, LANES), jnp.uint32),
                        pltpu.VMEM((SUBLANES, MIX_ROWS), jnp.int32),
                        pltpu.SMEM((SUBLANES, MIX_ROWS), jnp.int32),
                        pltpu.SemaphoreType.DMA((2,)),
                        pltpu.SemaphoreType.DMA((2,)),
                        pltpu.SemaphoreType.DMA],
        compiler_params=pltpu.CompilerParams(dimension_semantics=("arbitrary",),
                                             vmem_limit_bytes=VMEM_LIMIT),
        name="mixer",
    )(x, *consts)


def _experts_kernel(blk_ref, bexp_ref, nval_ref, ntot_ref, x_ref, wg_ref, wu_ref, wd_ref, y_hbm,
                    wgu_b, wd_b, obuf, addr_v, addr_s, row_sem, idx_sem):
    i = pl.program_id(0)
    slot = i % 2
    prev = 1 - slot
    n_used = ntot_ref[0]
    idx_copy = pltpu.make_async_copy(addr_v, addr_s.at[slot], idx_sem)
    prev_idx_copy = pltpu.make_async_copy(addr_v, addr_s.at[prev], idx_sem)

    def round_copy(s):
        return pltpu.make_async_copy(obuf.at[s], y_hbm.at[pl.ds(0, MOE_BM * DATA_SUB)], row_sem.at[s])

    def scatter_rows(first, count):
        for r in range(first, first + count):
            d = pl.multiple_of(addr_s[prev, 0, r], DATA_SUB)
            pltpu.make_async_copy(obuf.at[prev, pl.ds(r * DATA_SUB, DATA_SUB)],
                                  y_hbm.at[pl.ds(d, DATA_SUB)], row_sem.at[prev]).start(priority=r % 2)

    def dump_rows(s):
        return (Y_DUMP + s * MOE_BM + lax.broadcasted_iota(jnp.int32, (MOE_BM, LANES), 0)) * DATA_SUB

    @pl.when(i == 0)
    def _():
        obuf[...] = jnp.zeros_like(obuf)
        addr_v[...] = jnp.transpose(dump_rows(1))[0:SUBLANES, :]
        prev_idx_copy.start()

    @pl.when((i < n_used) & ((i == 0) | (bexp_ref[i] != bexp_ref[jnp.maximum(i - 1, 0)])))
    def _():
        wgu_b[:, 0:EXPERT_FF] = wg_ref[0].astype(jnp.bfloat16)
        wgu_b[:, EXPERT_FF:] = wu_ref[0].astype(jnp.bfloat16)
        wd_b[...] = wd_ref[0].astype(jnp.bfloat16)

    @pl.when(i < n_used)
    def _():
        prev_idx_copy.wait()
        meta = lax.bitcast_convert_type(x_ref[pl.ds(DATA_SUB, MOE_BM, stride=ROW_SUB), :], jnp.int32)
        row = lax.broadcasted_iota(jnp.int32, (MOE_BM, LANES), 0)
        lane = lax.broadcasted_iota(jnp.int32, (MOE_BM, LANES), 1)
        ret = jnp.where(row < nval_ref[i], meta, dump_rows(slot))
        addr_v[...] = jnp.transpose(jnp.where(lane == 0, ret, 0))[0:SUBLANES, :]
        idx_copy.start()

        def compute(nrows):
            batch = MOE_BM // 4
            parts = [_unpack_bf16_pairs(x_ref[pl.ds(c, nrows, stride=ROW_SUB), :]) for c in range(DATA_SUB)]
            x = jnp.concatenate([p[0] for p in parts] + [p[1] for p in parts], axis=-1)
            scatter_rows(0, batch)
            gu = _bdot(x, wgu_b[...])
            scatter_rows(batch, batch)
            g = gu[:, :EXPERT_FF]
            hid = (g * _sigmoid(g)) * gu[:, EXPERT_FF:]
            scatter_rows(2 * batch, batch)
            gate = lax.bitcast_convert_type(meta[0:nrows, 1:2], jnp.float32)
            o = _bdot(hid, wd_b[...]) * gate
            scatter_rows(3 * batch, batch)
            packed = _pack_bf16_pairs(o[:, :HALF], o[:, HALF:])

            @pl.when(i >= 1)
            def _():
                round_copy(slot).wait()

            for c in range(DATA_SUB):
                obuf[slot, pl.ds(c, nrows, stride=DATA_SUB), :] = packed[:, c * LANES:(c + 1) * LANES]

        half_full = nval_ref[i] <= MOE_BM // 2

        @pl.when(half_full)
        def _():
            compute(MOE_BM // 2)

        @pl.when(jnp.logical_not(half_full))
        def _():
            compute(MOE_BM)

    @pl.when(i == n_used)
    def _():
        prev_idx_copy.wait()
        scatter_rows(0, MOE_BM)

    @pl.when(i == pl.num_programs(0) - 1)
    def _():
        round_copy(n_used % 2).wait()
        round_copy((n_used - 1) % 2).wait()


def _experts(blk_idx, blk_expert, blk_nvalid, n_used, xs, w_gate, w_up, w_down):
    wmap = lambda i, blk, bexp, nval, ntot: (bexp[i], 0, 0)
    return pl.pallas_call(
        _experts_kernel,
        grid_spec=pltpu.PrefetchScalarGridSpec(
            num_scalar_prefetch=4,
            grid=(N_BLOCKS + 1,),
            in_specs=[pl.BlockSpec((MOE_BM * ROW_SUB, LANES),
                                   lambda i, blk, bexp, nval, ntot: (blk[i], 0)),
                      pl.BlockSpec((1, D_MODEL, EXPERT_FF), wmap),
                      pl.BlockSpec((1, D_MODEL, EXPERT_FF), wmap),
                      pl.BlockSpec((1, EXPERT_FF, D_MODEL), wmap)],
            out_specs=pl.BlockSpec(memory_space=pl.ANY),
            scratch_shapes=[pltpu.VMEM((D_MODEL, 2 * EXPERT_FF), jnp.bfloat16),
                            pltpu.VMEM((EXPERT_FF, D_MODEL), jnp.bfloat16),
                            pltpu.VMEM((2, MOE_BM * DATA_SUB, LANES), jnp.uint32),
                            pltpu.VMEM((SUBLANES, MOE_BM), jnp.int32),
                            pltpu.SMEM((2, SUBLANES, MOE_BM), jnp.int32),
                            pltpu.SemaphoreType.DMA((2,)),
                            pltpu.SemaphoreType.DMA]),
        out_shape=jax.ShapeDtypeStruct((Y_ROWS * DATA_SUB, LANES), jnp.uint32),
        compiler_params=pltpu.CompilerParams(dimension_semantics=("arbitrary",),
                                             vmem_limit_bytes=VMEM_LIMIT),
        name="experts",
    )(blk_idx, blk_expert, blk_nvalid, n_used, xs, w_gate, w_up, w_down)


def _final_kernel(h1_ref, y_ref, gain_ref, out_hbm, obuf, sem):
    i = pl.program_id(0)
    slot = i % 2
    stride = TOP_K * DATA_SUB

    def out_copy(step, s, b):
        return pltpu.make_async_copy(obuf.at[s, :, b, :],
                                     out_hbm.at[b, pl.ds(step * FIN_T, FIN_T), :], sem.at[s])

    lo, hi = [], []
    for c in range(DATA_SUB):
        y0 = _unpack_bf16_pairs(y_ref[pl.ds(c, FIN_TM, stride=stride), :])
        y1 = _unpack_bf16_pairs(y_ref[pl.ds(DATA_SUB + c, FIN_TM, stride=stride), :])
        lo.append(y0[0] + y1[0])
        hi.append(y0[1] + y1[1])
    y = jnp.concatenate(lo + hi, axis=-1)
    res = _rms(h1_ref[...] + y, gain_ref[...])

    @pl.when(i >= 2)
    def _():
        for b in range(BATCH):
            out_copy(i - 2, slot, b).wait()

    obuf[slot] = res.reshape(FIN_T, BATCH, D_MODEL)
    for b in range(BATCH):
        out_copy(i, slot, b).start()

    @pl.when(i == pl.num_programs(0) - 1)
    def _():
        for b in range(BATCH):
            out_copy(i - 1, 1 - slot, b).wait()
        for b in range(BATCH):
            out_copy(i, slot, b).wait()


def _final(h1, y2, final_gain):
    rows = FIN_TM * TOP_K * DATA_SUB
    return pl.pallas_call(
        _final_kernel,
        grid=(N_TOK // FIN_TM,),
        in_specs=[pl.BlockSpec((FIN_TM, D_MODEL), lambda i: (i, 0)),
                  pl.BlockSpec((rows, LANES), lambda i: (i, 0)),
                  pl.BlockSpec((1, D_MODEL), lambda i: (0, 0))],
        out_specs=pl.BlockSpec(memory_space=pl.ANY),
        out_shape=jax.ShapeDtypeStruct((BATCH, SEQ, D_MODEL), jnp.float32),
        scratch_shapes=[pltpu.VMEM((2, FIN_T, BATCH, D_MODEL), jnp.float32),
                        pltpu.SemaphoreType.DMA((2,))],
        compiler_params=pltpu.CompilerParams(dimension_semantics=("arbitrary",),
                                             vmem_limit_bytes=VMEM_LIMIT),
        name="final",
    )(h1, y2, final_gain)


def _block_diag(blocks):
    n, r, c = blocks.shape
    on_diag = jnp.eye(n, dtype=bool)[:, None, :, None]
    return jnp.where(on_diag, blocks[:, :, None, :], 0).reshape(n * r, n * c)


def kernel(x, meta_tokens, norm1_gain, w_in, conv_w, conv_b, lru_wa, lru_ba, lru_wx, lru_bx,
           lru_lambda, lru_out_gain, pool_w, pool_scale, w_out, norm2_gain, w_group, b_group,
           w_router, b_router, w_gate, w_up, w_down, final_gain):
    assert x.shape == (BATCH, SEQ, D_MODEL) and norm1_gain.shape[0] == 1
    f32, bf16 = jnp.float32, jnp.bfloat16
    row = lambda v: v.reshape(1, -1).astype(f32)

    meta_t = jnp.repeat(meta_tokens.astype(f32), BATCH, axis=0)

    heads_per = LRU_HEADS // 2
    w_gate_blk = jnp.stack([
        jnp.concatenate([_block_diag(lru_wa[0, j * heads_per:(j + 1) * heads_per]),
                         _block_diag(lru_wx[0, j * heads_per:(j + 1) * heads_per])],
                        axis=1) for j in range(2)]).astype(bf16)
    b_gate_blk = jnp.stack([lru_ba[0], lru_bx[0]]).astype(f32)
    w_pool_blk = jnp.stack([_block_diag(pool_w[0, 2 * j:2 * j + 2])
                            for j in range(2)]).astype(bf16)
    lam8 = row(LRU_C * jax.nn.log_sigmoid(lru_lambda[0].astype(f32)))
    pad = LANES - N_EXPERTS - N_GROUPS
    w_rt = jnp.concatenate([w_router[0], w_group[0], jnp.zeros((D_MODEL, pad), f32)], axis=1).astype(bf16)
    b_rt = jnp.concatenate([b_router[0], b_group[0], jnp.zeros((pad,), f32)]).reshape(1, LANES).astype(f32)
    ridx = jnp.arange(MIX_ROWS)
    tri = (ridx[None, :] < ridx[:, None]).astype(bf16)

    h1, xs, cnt = _mixer(
        x, meta_t, row(norm1_gain[0]), w_in[0].astype(bf16), conv_w[0].astype(f32),
        row(conv_b[0]), w_gate_blk, b_gate_blk, lam8, row(lru_out_gain[0]), w_pool_blk,
        row(pool_scale[0]), w_out[0].astype(bf16), row(norm2_gain[0]), w_rt, b_rt, tri)

    counts = cnt[0, :N_EXPERTS].astype(jnp.int32)
    nblk = (counts + MOE_BM - 1) // MOE_BM
    blk_end = jnp.cumsum(nblk)
    step = jnp.arange(N_BLOCKS + 1, dtype=jnp.int32)
    used = step < blk_end[-1]
    step_c = jnp.minimum(step, blk_end[-1] - 1)
    e_of = jnp.sum(blk_end[None, :] <= step_c[:, None], axis=-1).astype(jnp.int32)
    onehot_e = e_of[:, None] == jnp.arange(N_EXPERTS, dtype=jnp.int32)[None, :]
    j_of = step_c - jnp.sum(jnp.where(onehot_e, (blk_end - nblk)[None, :], 0), axis=-1)
    cnt_of = jnp.sum(jnp.where(onehot_e, counts[None, :], 0), axis=-1)
    blk_idx = (e_of * CAP_BLOCKS + j_of).astype(jnp.int32)
    blk_nvalid = jnp.where(used, jnp.clip(cnt_of - j_of * MOE_BM, 0, MOE_BM), 0).astype(jnp.int32)

    y2 = _experts(blk_idx, e_of, blk_nvalid, blk_end[-1:].astype(jnp.int32), xs,
                  w_gate[0], w_up[0], w_down[0])
    return _final(h1, y2, row(final_gain))
```

```python
import functools

import jax
import jax.numpy as jnp
from jax import lax
from jax.experimental import pallas as pl
from jax.experimental.pallas import tpu as pltpu

D_MODEL = 1024
BATCH = 8
SEQ = 4096
N_META = 16
LRU_WIDTH = 512
LRU_HEADS = 8
LRU_HEAD_DIM = 64
CONV_WIDTH = 4
LRU_C = 8.0
POOL_WIDTH = 512
POOL_WINDOWS = (2, 4, 8, 16)
POOL_GROUP = 128
N_GROUPS = 4
EXPERTS_PER_GROUP = 8
N_EXPERTS = 32
TOP_K = 2
EXPERT_FF = 512
RMS_EPS = 1e-6
SQRT_FLOOR = 1e-37

N_TOK = BATCH * SEQ
SUBLANES = 8
LANES = 128
MIX_ROWS = 512
META_ROWS = N_META * BATCH
CONV_HALO = (CONV_WIDTH - 1) * BATCH
POOL_HALO = (max(POOL_WINDOWS) - 1) * BATCH
MIX_STEPS = N_TOK // MIX_ROWS
MIX_T = MIX_ROWS // BATCH
MOE_BM = 512
N_ASSIGN = N_TOK * TOP_K
N_BLOCKS = N_ASSIGN // MOE_BM + N_EXPERTS
EXPERT_CAP = N_TOK
CAP_BLOCKS = EXPERT_CAP // MOE_BM
ROW_SUB = 8
DATA_SUB = 4
HALF = D_MODEL // 2
XS_DUMP = N_EXPERTS * EXPERT_CAP
XS_ROWS = XS_DUMP + TOP_K * MIX_ROWS
SCATTER_BATCH = TOP_K * MIX_ROWS // 8
Y_DUMP = N_ASSIGN
Y_ROWS = Y_DUMP + 2 * MOE_BM
FIN_TM = 1024
FIN_T = FIN_TM // BATCH
VMEM_LIMIT = 56 * 1024 * 1024


def _rms(x, gain):
    return x * lax.rsqrt(jnp.mean(x * x, axis=-1, keepdims=True) + RMS_EPS) * gain


def _sigmoid(x):
    return 0.5 * jnp.tanh(0.5 * x) + 0.5


def _bdot(a, b):
    return jnp.dot(a.astype(jnp.bfloat16), b, preferred_element_type=jnp.float32)


def _pack_bf16_pairs(lo, hi):
    lo_b = lax.bitcast_convert_type(lo.astype(jnp.bfloat16).astype(jnp.float32), jnp.uint32)
    hi_b = lax.bitcast_convert_type(hi.astype(jnp.bfloat16).astype(jnp.float32), jnp.uint32)
    return (lo_b >> 16) | (hi_b & jnp.uint32(0xFFFF0000))


def _unpack_bf16_pairs(u):
    lo = lax.bitcast_convert_type(u << 16, jnp.float32)
    hi = lax.bitcast_convert_type(u & jnp.uint32(0xFFFF0000), jnp.float32)
    return lo, hi


def _mix_rows(x, nrows, with_output, g1_ref, w_in_ref, conv_w_ref, conv_b_ref, w_gate_ref,
              b_gate_ref, lam8_ref, lru_gain_ref, w_pool_ref, pool_scale_ref, w_out_ref,
              ux_buf, up_buf, a_buf, b_buf, hs_buf, hstate, between=lambda: None):
    hn = _rms(x, g1_ref[...])
    proj = _bdot(hn, w_in_ref[...])
    between()
    ux = proj[:, :LRU_WIDTH]
    up = proj[:, 2 * LRU_WIDTH:]

    ux_buf[CONV_HALO:CONV_HALO + nrows, :] = ux
    xc = conv_b_ref[...] + conv_w_ref[3:4, :] * ux
    for k in range(CONV_WIDTH - 1):
        xc = xc + conv_w_ref[k:k + 1, :] * ux_buf[k * BATCH:k * BATCH + nrows, :]
    ux_buf[0:CONV_HALO, :] = ux_buf[nrows:nrows + CONV_HALO, :]
    between()

    half = LRU_WIDTH // 2
    z = [_bdot(xc[:, j * half:(j + 1) * half], w_gate_ref[j]) for j in range(2)]
    za = jnp.concatenate([z[0][:, :half], z[1][:, :half]], axis=-1) + b_gate_ref[0:1, :]
    zx = jnp.concatenate([z[0][:, half:], z[1][:, half:]], axis=-1) + b_gate_ref[1:2, :]
    r = _sigmoid(za)
    gi = _sigmoid(zx)
    log_a = lam8_ref[...] * r
    a = jnp.exp(log_a)
    a_buf[0:nrows, :] = a
    v = jnp.tanh(-log_a) * (1.0 + a * a)
    root = v * lax.rsqrt(jnp.maximum(v, SQRT_FLOOR))
    b_buf[0:nrows, :] = root * (gi * xc)
    between()

    def step(t, h):
        r0 = pl.multiple_of(t * SUBLANES, SUBLANES)
        h = a_buf[pl.ds(r0, SUBLANES), :] * h + b_buf[pl.ds(r0, SUBLANES), :]
        hs_buf[pl.ds(r0, SUBLANES), :] = h
        return h

    hstate[...] = lax.fori_loop(0, nrows // SUBLANES, step, hstate[...], unroll=True)

    up_buf[POOL_HALO:POOL_HALO + nrows, :] = up
    if not with_output:
        up_buf[0:POOL_HALO, :] = up_buf[nrows:nrows + POOL_HALO, :]
        return None

    ug = proj[:, LRU_WIDTH:2 * LRU_WIDTH]
    y_lru = _rms(hs_buf[0:nrows, :] * jax.nn.gelu(ug), lru_gain_ref[...])

    pm = []
    for g, w in enumerate(POOL_WINDOWS):
        lo, hi = g * POOL_GROUP, (g + 1) * POOL_GROUP
        s = up_buf[POOL_HALO - (w - 1) * BATCH:POOL_HALO + nrows, lo:hi]
        m = 1
        while m < w:
            s = s[m * BATCH:, :] + s[:-m * BATCH, :]
            m *= 2
        pm.append(s * (1.0 / w) - up[:, lo:hi])
    up_buf[0:POOL_HALO, :] = up_buf[nrows:nrows + POOL_HALO, :]
    yp = [_bdot(jnp.concatenate(pm[2 * j:2 * j + 2], axis=-1), w_pool_ref[j]) for j in range(2)]
    y_pool = _rms(jnp.concatenate(yp, axis=-1), pool_scale_ref[...])
    between()

    y = _bdot(jnp.concatenate([y_lru, y_pool], axis=-1), w_out_ref[...])
    between()
    return x + y


def _mixer_kernel(x_hbm, meta_ref, g1_ref, w_in_ref, conv_w_ref, conv_b_ref, w_gate_ref,
                  b_gate_ref, lam8_ref, lru_gain_ref, w_pool_ref, pool_scale_ref, w_out_ref,
                  g2_ref, w_rt_ref, b_rt_ref, tri_ref,
                  h1_ref, xs_hbm, cnt_ref,
                  ux_buf, up_buf, a_buf, b_buf, hs_buf, hstate, running,
                  xin, rowbuf, dest_v, dest_s, in_sem, row_sem, idx_sem):
    i = pl.program_id(0)
    slot = i % 2
    n_rows = TOP_K * MIX_ROWS
    mix = functools.partial(
        _mix_rows, g1_ref=g1_ref, w_in_ref=w_in_ref, conv_w_ref=conv_w_ref,
        conv_b_ref=conv_b_ref, w_gate_ref=w_gate_ref, b_gate_ref=b_gate_ref, lam8_ref=lam8_ref,
        lru_gain_ref=lru_gain_ref, w_pool_ref=w_pool_ref, pool_scale_ref=pool_scale_ref,
        w_out_ref=w_out_ref, ux_buf=ux_buf, up_buf=up_buf, a_buf=a_buf, b_buf=b_buf,
        hs_buf=hs_buf, hstate=hstate)
    idx_copy = pltpu.make_async_copy(dest_v, dest_s, idx_sem)
    round_copy = pltpu.make_async_copy(rowbuf, xs_hbm.at[pl.ds(0, n_rows * ROW_SUB)], row_sem)

    def x_copy(step, s, b):
        return pltpu.make_async_copy(x_hbm.at[b, pl.ds(step * MIX_T, MIX_T), :],
                                     xin.at[s, :, b, :], in_sem.at[s])

    def scatter_rows(first, count):
        for n in range(first, first + count):
            k, r = divmod(n, MIX_ROWS)
            d = pl.multiple_of(dest_s[k, r], ROW_SUB)
            pltpu.make_async_copy(rowbuf.at[pl.ds(n * ROW_SUB, ROW_SUB)],
                                  xs_hbm.at[pl.ds(d, ROW_SUB)], row_sem).start(priority=n % 2)

    issued = [0]

    def scatter_batch():
        scatter_rows(issued[0], SCATTER_BATCH)
        issued[0] += SCATTER_BATCH

    @pl.when(i == 0)
    def _():
        for b in range(BATCH):
            x_copy(0, 0, b).start()
        ux_buf[0:CONV_HALO, :] = jnp.zeros((CONV_HALO, LRU_WIDTH), jnp.float32)
        up_buf[0:POOL_HALO, :] = jnp.zeros((POOL_HALO, POOL_WIDTH), jnp.float32)
        hstate[...] = jnp.zeros_like(hstate)
        running[...] = jnp.zeros_like(running)
        mix(meta_ref[...], META_ROWS, False)
        rowbuf[...] = jnp.zeros_like(rowbuf)
        col = lax.broadcasted_iota(jnp.int32, (SUBLANES, MIX_ROWS), 1)
        sub = lax.broadcasted_iota(jnp.int32, (SUBLANES, MIX_ROWS), 0)
        dest_v[...] = (XS_DUMP + jnp.minimum(sub, TOP_K - 1) * MIX_ROWS + col) * ROW_SUB
        idx_copy.start()

    @pl.when(i + 1 < MIX_STEPS)
    def _():
        for b in range(BATCH):
            x_copy(i + 1, 1 - slot, b).start()

    for b in range(BATCH):
        x_copy(i, slot, b).wait()
    idx_copy.wait()

    @pl.when(i >= 0)
    def _():
        x = xin[slot].reshape(MIX_ROWS, D_MODEL)
        h1_ref[...] = mix(x, MIX_ROWS, True, between=scatter_batch)

    h1 = h1_ref[...]
    hn2 = _rms(h1, g2_ref[...])
    logits = _bdot(hn2, w_rt_ref[...]) + b_rt_ref[...]
    scatter_batch()
    lane = lax.broadcasted_iota(jnp.int32, (MIX_ROWS, LANES), 1)
    lane_f = lane.astype(jnp.float32)
    neg = jnp.float32(-jnp.inf)
    big = jnp.float32(4 * LANES)

    is_g = (lane >= N_EXPERTS) & (lane < N_EXPERTS + N_GROUPS)
    gl = jnp.where(is_g, logits, neg)
    gmax = jnp.max(gl, axis=-1, keepdims=True)
    gidx = jnp.min(jnp.where(gl == gmax, lane_f, big), axis=-1, keepdims=True)
    p_g = 1.0 / jnp.sum(jnp.exp(gl - gmax), axis=-1, keepdims=True)
    grp = gidx.astype(jnp.int32) - N_EXPERTS

    el = jnp.where((lane >> 3) == grp, logits, neg)
    m1 = jnp.max(el, axis=-1, keepdims=True)
    i1 = jnp.min(jnp.where(el == m1, lane_f, big), axis=-1, keepdims=True)
    el2 = jnp.where(lane_f == i1, neg, el)
    m2 = jnp.max(el2, axis=-1, keepdims=True)
    i2 = jnp.min(jnp.where(el2 == m2, lane_f, big), axis=-1, keepdims=True)
    e21 = jnp.exp(m2 - m1)
    den = 1.0 + e21
    gate = (p_g * (1.0 / den), p_g * (e21 / den))
    scatter_batch()

    sel = (lane_f == i1, lane_f == i2)
    onehot = jnp.where(sel[0] | sel[1], 1.0, 0.0)
    before = jnp.dot(tri_ref[...], onehot.astype(jnp.bfloat16),
                     preferred_element_type=jnp.float32) + running[0:1, :]
    rank = [jnp.sum(jnp.where(sel[k], before, 0.0), axis=-1, keepdims=True) for k in range(TOP_K)]
    running[0:1, :] = running[0:1, :] + jnp.sum(onehot, axis=0, keepdims=True)
    cnt_ref[...] = running[...]
    scatter_batch()
    assert issued[0] == n_rows

    eid = (i1.astype(jnp.int32), i2.astype(jnp.int32))
    dst = [(eid[k] * EXPERT_CAP + rank[k].astype(jnp.int32)) * ROW_SUB for k in range(TOP_K)]
    dst_tile = jnp.where(lane == 0, dst[0], jnp.where(lane == 1, dst[1], 0))
    dest_v[...] = jnp.transpose(dst_tile)[0:SUBLANES, :]
    idx_copy.start()

    data = _pack_bf16_pairs(hn2[:, :HALF], hn2[:, HALF:])
    tok = i * MIX_ROWS + lax.broadcasted_iota(jnp.int32, (MIX_ROWS, LANES), 0)

    round_copy.wait()
    for k in range(TOP_K):
        base = k * MIX_ROWS * ROW_SUB
        for c in range(DATA_SUB):
            rowbuf[pl.ds(base + c, MIX_ROWS, stride=ROW_SUB), :] = data[:, c * LANES:(c + 1) * LANES]
        ret = (tok * TOP_K + k) * DATA_SUB
        gbits = lax.bitcast_convert_type(jnp.broadcast_to(gate[k], (MIX_ROWS, LANES)), jnp.int32)
        meta = jnp.where(lane == 0, ret, jnp.where(lane == 1, gbits, 0))
        rowbuf[pl.ds(base + DATA_SUB, MIX_ROWS, stride=ROW_SUB), :] = lax.bitcast_convert_type(
            meta, jnp.uint32)

    @pl.when(i == MIX_STEPS - 1)
    def _():
        idx_copy.wait()
        scatter_rows(0, n_rows)
        round_copy.wait()


def _mixer(x, meta_t, g1, w_in, conv_w, conv_b, w_gate, b_gate, lam8, lru_gain, w_pool,
           pool_scale, w_out, g2, w_rt, b_rt, tri):
    full = lambda a: pl.BlockSpec(a.shape, lambda i: (0,) * a.ndim)
    consts = (meta_t, g1, w_in, conv_w, conv_b, w_gate, b_gate, lam8, lru_gain, w_pool,
              pool_scale, w_out, g2, w_rt, b_rt, tri)
    return pl.pallas_call(
        _mixer_kernel,
        grid=(MIX_STEPS,),
        in_specs=[pl.BlockSpec(memory_space=pl.ANY)] + [full(a) for a in consts],
        out_specs=[pl.BlockSpec((MIX_ROWS, D_MODEL), lambda i: (i, 0)),
                   pl.BlockSpec(memory_space=pl.ANY),
                   pl.BlockSpec((SUBLANES, LANES), lambda i: (0, 0))],
        out_shape=[jax.ShapeDtypeStruct((N_TOK, D_MODEL), jnp.float32),
                   jax.ShapeDtypeStruct((XS_ROWS * ROW_SUB, LANES), jnp.uint32),
                   jax.ShapeDtypeStruct((SUBLANES, LANES), jnp.float32)],
        scratch_shapes=[pltpu.VMEM((CONV_HALO + MIX_ROWS, LRU_WIDTH), jnp.float32),
                        pltpu.VMEM((POOL_HALO + MIX_ROWS, POOL_WIDTH), jnp.float32),
                        pltpu.VMEM((MIX_ROWS, LRU_WIDTH), jnp.float32),
                        pltpu.VMEM((MIX_ROWS, LRU_WIDTH), jnp.float32),
                        pltpu.VMEM((MIX_ROWS, LRU_WIDTH), jnp.float32),
                        pltpu.VMEM((SUBLANES, LRU_WIDTH), jnp.float32),
                        pltpu.VMEM((SUBLANES, LANES), jnp.float32),
                        pltpu.VMEM((2, MIX_T, BATCH, D_MODEL), jnp.float32),
                        pltpu.VMEM((TOP_K * MIX_ROWS * ROW_SUB, LANES), jnp.uint32),
                        pltpu.VMEM((SUBLANES, MIX_ROWS), jnp.int32),
                        pltpu.SMEM((SUBLANES, MIX_ROWS), jnp.int32),
                        pltpu.SemaphoreType.DMA((2,)),
                        pltpu.SemaphoreType.DMA,
                        pltpu.SemaphoreType.DMA],
        compiler_params=pltpu.CompilerParams(dimension_semantics=("arbitrary",),
                                             vmem_limit_bytes=VMEM_LIMIT),
        name="mixer",
    )(x, *consts)


def _experts_kernel(blk_ref, bexp_ref, nval_ref, ntot_ref, wslot_ref, wnext_ref,
                    x_ref, wg_hbm, wu_hbm, wd_hbm, y_hbm,
                    wgu_b, wd_b, wg_f, wu_f, wd_f, obuf, addr_v, addr_s, row_sem, idx_sem, w_sem):
    i = pl.program_id(0)
    slot = i % 2
    prev = 1 - slot
    n_used = ntot_ref[0]
    idx_copy = pltpu.make_async_copy(addr_v, addr_s.at[slot], idx_sem)
    prev_idx_copy = pltpu.make_async_copy(addr_v, addr_s.at[prev], idx_sem)

    def round_copy(s):
        return pltpu.make_async_copy(obuf.at[s], y_hbm.at[pl.ds(0, MOE_BM * DATA_SUB)], row_sem.at[s])

    def scatter_rows(first, count):
        for r in range(first, first + count):
            d = pl.multiple_of(addr_s[prev, 0, r], DATA_SUB)
            pltpu.make_async_copy(obuf.at[prev, pl.ds(r * DATA_SUB, DATA_SUB)],
                                  y_hbm.at[pl.ds(d, DATA_SUB)], row_sem.at[prev]).start(priority=r % 2)

    def dump_rows(s):
        return (Y_DUMP + s * MOE_BM + lax.broadcasted_iota(jnp.int32, (MOE_BM, LANES), 0)) * DATA_SUB

    @pl.when(i == 0)
    def _():
        obuf[...] = jnp.zeros_like(obuf)
        addr_v[...] = jnp.transpose(dump_rows(1))[0:SUBLANES, :]
        prev_idx_copy.start()

    def weight_copies(e, s):
        return [pltpu.make_async_copy(src.at[e], dst.at[s], w_sem.at[s])
                for src, dst in ((wg_hbm, wg_f), (wu_hbm, wu_f), (wd_hbm, wd_f))]

    @pl.when((i < n_used) & ((i == 0) | (bexp_ref[i] != bexp_ref[jnp.maximum(i - 1, 0)])))
    def _():
        s = wslot_ref[i]

        @pl.when(i == 0)
        def _():
            for c in weight_copies(bexp_ref[0], s):
                c.start()

        for c in weight_copies(bexp_ref[i], s):
            c.wait()

        @pl.when(wnext_ref[i] < N_EXPERTS)
        def _():
            for c in weight_copies(wnext_ref[i], 1 - s):
                c.start()

        wgu_b[:, 0:EXPERT_FF] = wg_f[s].astype(jnp.bfloat16)
        wgu_b[:, EXPERT_FF:] = wu_f[s].astype(jnp.bfloat16)
        wd_b[...] = wd_f[s].astype(jnp.bfloat16)

    @pl.when(i < n_used)
    def _():
        prev_idx_copy.wait()
        meta = lax.bitcast_convert_type(x_ref[pl.ds(DATA_SUB, MOE_BM, stride=ROW_SUB), :], jnp.int32)
        row = lax.broadcasted_iota(jnp.int32, (MOE_BM, LANES), 0)
        lane = lax.broadcasted_iota(jnp.int32, (MOE_BM, LANES), 1)
        ret = jnp.where(row < nval_ref[i], meta, dump_rows(slot))
        addr_v[...] = jnp.transpose(jnp.where(lane == 0, ret, 0))[0:SUBLANES, :]
        idx_copy.start()

        def compute(nrows):
            batch = MOE_BM // 4
            parts = [_unpack_bf16_pairs(x_ref[pl.ds(c, nrows, stride=ROW_SUB), :]) for c in range(DATA_SUB)]
            x = jnp.concatenate([p[0] for p in parts] + [p[1] for p in parts], axis=-1)
            scatter_rows(0, batch)
            gu = _bdot(x, wgu_b[...])
            scatter_rows(batch, batch)
            g = gu[:, :EXPERT_FF]
            hid = (g * _sigmoid(g)) * gu[:, EXPERT_FF:]
            scatter_rows(2 * batch, batch)
            gate = lax.bitcast_convert_type(meta[0:nrows, 1:2], jnp.float32)
            o = _bdot(hid, wd_b[...]) * gate
            scatter_rows(3 * batch, batch)
            packed = _pack_bf16_pairs(o[:, :HALF], o[:, HALF:])

            @pl.when(i >= 1)
            def _():
                round_copy(slot).wait()

            for c in range(DATA_SUB):
                obuf[slot, pl.ds(c, nrows, stride=DATA_SUB), :] = packed[:, c * LANES:(c + 1) * LANES]

        half_full = nval_ref[i] <= MOE_BM // 2

        @pl.when(half_full)
        def _():
            compute(MOE_BM // 2)

        @pl.when(jnp.logical_not(half_full))
        def _():
            compute(MOE_BM)

    @pl.when(i == n_used)
    def _():
        prev_idx_copy.wait()
        scatter_rows(0, MOE_BM)

    @pl.when(i == pl.num_programs(0) - 1)
    def _():
        round_copy(n_used % 2).wait()
        round_copy((n_used - 1) % 2).wait()


def _experts(blk_idx, blk_expert, blk_nvalid, n_used, blk_wslot, blk_wnext, xs, w_gate, w_up, w_down):
    return pl.pallas_call(
        _experts_kernel,
        grid_spec=pltpu.PrefetchScalarGridSpec(
            num_scalar_prefetch=6,
            grid=(N_BLOCKS + 1,),
            in_specs=[pl.BlockSpec((MOE_BM * ROW_SUB, LANES),
                                   lambda i, blk, bexp, nval, ntot, wslot, wnext: (blk[i], 0)),
                      pl.BlockSpec(memory_space=pl.ANY),
                      pl.BlockSpec(memory_space=pl.ANY),
                      pl.BlockSpec(memory_space=pl.ANY)],
            out_specs=pl.BlockSpec(memory_space=pl.ANY),
            scratch_shapes=[pltpu.VMEM((D_MODEL, 2 * EXPERT_FF), jnp.bfloat16),
                            pltpu.VMEM((EXPERT_FF, D_MODEL), jnp.bfloat16),
                            pltpu.VMEM((2, D_MODEL, EXPERT_FF), jnp.float32),
                            pltpu.VMEM((2, D_MODEL, EXPERT_FF), jnp.float32),
                            pltpu.VMEM((2, EXPERT_FF, D_MODEL), jnp.float32),
                            pltpu.VMEM((2, MOE_BM * DATA_SUB, LANES), jnp.uint32),
                            pltpu.VMEM((SUBLANES, MOE_BM), jnp.int32),
                            pltpu.SMEM((2, SUBLANES, MOE_BM), jnp.int32),
                            pltpu.SemaphoreType.DMA((2,)),
                            pltpu.SemaphoreType.DMA,
                            pltpu.SemaphoreType.DMA((2,))]),
        out_shape=jax.ShapeDtypeStruct((Y_ROWS * DATA_SUB, LANES), jnp.uint32),
        compiler_params=pltpu.CompilerParams(dimension_semantics=("arbitrary",),
                                             vmem_limit_bytes=VMEM_LIMIT),
        name="experts",
    )(blk_idx, blk_expert, blk_nvalid, n_used, blk_wslot, blk_wnext, xs, w_gate, w_up, w_down)


def _final_kernel(h1_ref, y_ref, gain_ref, out_hbm, obuf, sem):
    i = pl.program_id(0)
    slot = i % 2
    stride = TOP_K * DATA_SUB

    def out_copy(step, s, b):
        return pltpu.make_async_copy(obuf.at[s, :, b, :],
                                     out_hbm.at[b, pl.ds(step * FIN_T, FIN_T), :], sem.at[s])

    lo, hi = [], []
    for c in range(DATA_SUB):
        y0 = _unpack_bf16_pairs(y_ref[pl.ds(c, FIN_TM, stride=stride), :])
        y1 = _unpack_bf16_pairs(y_ref[pl.ds(DATA_SUB + c, FIN_TM, stride=stride), :])
        lo.append(y0[0] + y1[0])
        hi.append(y0[1] + y1[1])
    y = jnp.concatenate(lo + hi, axis=-1)
    res = _rms(h1_ref[...] + y, gain_ref[...])

    @pl.when(i >= 2)
    def _():
        for b in range(BATCH):
            out_copy(i - 2, slot, b).wait()

    obuf[slot] = res.reshape(FIN_T, BATCH, D_MODEL)
    for b in range(BATCH):
        out_copy(i, slot, b).start()

    @pl.when(i == pl.num_programs(0) - 1)
    def _():
        for b in range(BATCH):
            out_copy(i - 1, 1 - slot, b).wait()
        for b in range(BATCH):
            out_copy(i, slot, b).wait()


def _final(h1, y2, final_gain):
    rows = FIN_TM * TOP_K * DATA_SUB
    return pl.pallas_call(
        _final_kernel,
        grid=(N_TOK // FIN_TM,),
        in_specs=[pl.BlockSpec((FIN_TM, D_MODEL), lambda i: (i, 0)),
                  pl.BlockSpec((rows, LANES), lambda i: (i, 0)),
                  pl.BlockSpec((1, D_MODEL), lambda i: (0, 0))],
        out_specs=pl.BlockSpec(memory_space=pl.ANY),
        out_shape=jax.ShapeDtypeStruct((BATCH, SEQ, D_MODEL), jnp.float32),
        scratch_shapes=[pltpu.VMEM((2, FIN_T, BATCH, D_MODEL), jnp.float32),
                        pltpu.SemaphoreType.DMA((2,))],
        compiler_params=pltpu.CompilerParams(dimension_semantics=("arbitrary",),
                                             vmem_limit_bytes=VMEM_LIMIT),
        name="final",
    )(h1, y2, final_gain)


def _block_diag(blocks):
    n, r, c = blocks.shape
    on_diag = jnp.eye(n, dtype=bool)[:, None, :, None]
    return jnp.where(on_diag, blocks[:, :, None, :], 0).reshape(n * r, n * c)


def kernel(x, meta_tokens, norm1_gain, w_in, conv_w, conv_b, lru_wa, lru_ba, lru_wx, lru_bx,
           lru_lambda, lru_out_gain, pool_w, pool_scale, w_out, norm2_gain, w_group, b_group,
           w_router, b_router, w_gate, w_up, w_down, final_gain):
    assert x.shape == (BATCH, SEQ, D_MODEL) and norm1_gain.shape[0] == 1
    f32, bf16 = jnp.float32, jnp.bfloat16
    row = lambda v: v.reshape(1, -1).astype(f32)

    meta_t = jnp.repeat(meta_tokens.astype(f32), BATCH, axis=0)

    heads_per = LRU_HEADS // 2
    w_gate_blk = jnp.stack([
        jnp.concatenate([_block_diag(lru_wa[0, j * heads_per:(j + 1) * heads_per]),
                         _block_diag(lru_wx[0, j * heads_per:(j + 1) * heads_per])],
                        axis=1) for j in range(2)]).astype(bf16)
    b_gate_blk = jnp.stack([lru_ba[0], lru_bx[0]]).astype(f32)
    w_pool_blk = jnp.stack([_block_diag(pool_w[0, 2 * j:2 * j + 2])
                            for j in range(2)]).astype(bf16)
    lam8 = row(LRU_C * jax.nn.log_sigmoid(lru_lambda[0].astype(f32)))
    pad = LANES - N_EXPERTS - N_GROUPS
    w_rt = jnp.concatenate([w_router[0], w_group[0], jnp.zeros((D_MODEL, pad), f32)], axis=1).astype(bf16)
    b_rt = jnp.concatenate([b_router[0], b_group[0], jnp.zeros((pad,), f32)]).reshape(1, LANES).astype(f32)
    ridx = jnp.arange(MIX_ROWS)
    tri = (ridx[None, :] < ridx[:, None]).astype(bf16)

    h1, xs, cnt = _mixer(
        x, meta_t, row(norm1_gain[0]), w_in[0].astype(bf16), conv_w[0].astype(f32),
        row(conv_b[0]), w_gate_blk, b_gate_blk, lam8, row(lru_out_gain[0]), w_pool_blk,
        row(pool_scale[0]), w_out[0].astype(bf16), row(norm2_gain[0]), w_rt, b_rt, tri)

    counts = cnt[0, :N_EXPERTS].astype(jnp.int32)
    nblk = (counts + MOE_BM - 1) // MOE_BM
    blk_end = jnp.cumsum(nblk)
    step = jnp.arange(N_BLOCKS + 1, dtype=jnp.int32)
    used = step < blk_end[-1]
    step_c = jnp.minimum(step, blk_end[-1] - 1)
    e_of = jnp.sum(blk_end[None, :] <= step_c[:, None], axis=-1).astype(jnp.int32)
    onehot_e = e_of[:, None] == jnp.arange(N_EXPERTS, dtype=jnp.int32)[None, :]
    j_of = step_c - jnp.sum(jnp.where(onehot_e, (blk_end - nblk)[None, :], 0), axis=-1)
    cnt_of = jnp.sum(jnp.where(onehot_e, counts[None, :], 0), axis=-1)
    blk_idx = (e_of * CAP_BLOCKS + j_of).astype(jnp.int32)
    blk_nvalid = jnp.where(used, jnp.clip(cnt_of - j_of * MOE_BM, 0, MOE_BM), 0).astype(jnp.int32)

    eidx = jnp.arange(N_EXPERTS, dtype=jnp.int32)
    present = nblk > 0
    slot_e = (jnp.cumsum(present) - 1) % 2
    later = present[None, :] & (eidx[None, :] > eidx[:, None])
    next_e = jnp.min(jnp.where(later, eidx[None, :], N_EXPERTS), axis=-1)
    blk_wslot = jnp.sum(jnp.where(onehot_e, slot_e[None, :], 0), axis=-1).astype(jnp.int32)
    blk_wnext = jnp.sum(jnp.where(onehot_e, next_e[None, :], 0), axis=-1).astype(jnp.int32)

    y2 = _experts(blk_idx, e_of, blk_nvalid, blk_end[-1:].astype(jnp.int32), blk_wslot, blk_wnext, xs,
                  w_gate[0], w_up[0], w_down[0])
    return _final(h1, y2, row(final_gain))
```

```python
import functools

import jax
import jax.numpy as jnp
from jax import lax
from jax.experimental import pallas as pl
from jax.experimental.pallas import tpu as pltpu

D_MODEL = 1024
BATCH = 8
SEQ = 4096
N_META = 16
LRU_WIDTH = 512
LRU_HEADS = 8
LRU_HEAD_DIM = 64
CONV_WIDTH = 4
LRU_C = 8.0
POOL_WIDTH = 512
POOL_WINDOWS = (2, 4, 8, 16)
POOL_GROUP = 128
N_GROUPS = 4
EXPERTS_PER_GROUP = 8
N_EXPERTS = 32
TOP_K = 2
EXPERT_FF = 512
RMS_EPS = 1e-6
SQRT_FLOOR = 1e-37

N_TOK = BATCH * SEQ
SUBLANES = 8
LANES = 128
MIX_ROWS = 512
META_ROWS = N_META * BATCH
CONV_HALO = (CONV_WIDTH - 1) * BATCH
POOL_HALO = (max(POOL_WINDOWS) - 1) * BATCH
MIX_STEPS = N_TOK // MIX_ROWS
MIX_T = MIX_ROWS // BATCH
MOE_BM = 512
N_ASSIGN = N_TOK * TOP_K
N_BLOCKS = N_ASSIGN // MOE_BM + N_EXPERTS
EXPERT_CAP = N_TOK
CAP_BLOCKS = EXPERT_CAP // MOE_BM
ROW_SUB = 8
DATA_SUB = 4
HALF = D_MODEL // 2
XS_DUMP = N_EXPERTS * EXPERT_CAP
XS_ROWS = XS_DUMP + TOP_K * MIX_ROWS
SCATTER_BATCH = TOP_K * MIX_ROWS // 8
Y_DUMP = N_ASSIGN
Y_ROWS = Y_DUMP + 2 * MOE_BM
FIN_TM = 1024
FIN_T = FIN_TM // BATCH
VMEM_LIMIT = 56 * 1024 * 1024


def _rms(x, gain):
    return x * lax.rsqrt(jnp.mean(x * x, axis=-1, keepdims=True) + RMS_EPS) * gain


def _sigmoid(x):
    return 0.5 * jnp.tanh(0.5 * x) + 0.5


def _bdot(a, b):
    return jnp.dot(a.astype(jnp.bfloat16), b, preferred_element_type=jnp.float32)


def _pack_bf16_pairs(lo, hi):
    lo_b = lax.bitcast_convert_type(lo.astype(jnp.bfloat16).astype(jnp.float32), jnp.uint32)
    hi_b = lax.bitcast_convert_type(hi.astype(jnp.bfloat16).astype(jnp.float32), jnp.uint32)
    return (lo_b >> 16) | (hi_b & jnp.uint32(0xFFFF0000))


def _unpack_bf16_pairs(u):
    lo = lax.bitcast_convert_type(u << 16, jnp.float32)
    hi = lax.bitcast_convert_type(u & jnp.uint32(0xFFFF0000), jnp.float32)
    return lo, hi


def _mix_rows(x, nrows, with_output, g1_ref, w_in_ref, conv_w_ref, conv_b_ref, w_gate_ref,
              b_gate_ref, lam8_ref, lru_gain_ref, w_pool_ref, pool_scale_ref, w_out_ref,
              ux_buf, up_buf, a_buf, b_buf, hs_buf, hstate, between=lambda: None):
    hn = _rms(x, g1_ref[...])
    proj = _bdot(hn, w_in_ref[...])
    between()
    ux = proj[:, :LRU_WIDTH]
    up = proj[:, 2 * LRU_WIDTH:]

    ux_buf[CONV_HALO:CONV_HALO + nrows, :] = ux
    xc = conv_b_ref[...] + conv_w_ref[3:4, :] * ux
    for k in range(CONV_WIDTH - 1):
        xc = xc + conv_w_ref[k:k + 1, :] * ux_buf[k * BATCH:k * BATCH + nrows, :]
    ux_buf[0:CONV_HALO, :] = ux_buf[nrows:nrows + CONV_HALO, :]
    between()

    half = LRU_WIDTH // 2
    z = [_bdot(xc[:, j * half:(j + 1) * half], w_gate_ref[j]) for j in range(2)]
    za = jnp.concatenate([z[0][:, :half], z[1][:, :half]], axis=-1) + b_gate_ref[0:1, :]
    zx = jnp.concatenate([z[0][:, half:], z[1][:, half:]], axis=-1) + b_gate_ref[1:2, :]
    r = _sigmoid(za)
    gi = _sigmoid(zx)
    log_a = lam8_ref[...] * r
    a = jnp.exp(log_a)
    a_buf[0:nrows, :] = a
    v = jnp.tanh(-log_a) * (1.0 + a * a)
    root = v * lax.rsqrt(jnp.maximum(v, SQRT_FLOOR))
    b_buf[0:nrows, :] = root * (gi * xc)
    between()

    def step(t, h):
        r0 = pl.multiple_of(t * SUBLANES, SUBLANES)
        h = a_buf[pl.ds(r0, SUBLANES), :] * h + b_buf[pl.ds(r0, SUBLANES), :]
        hs_buf[pl.ds(r0, SUBLANES), :] = h
        return h

    hstate[...] = lax.fori_loop(0, nrows // SUBLANES, step, hstate[...], unroll=True)

    up_buf[POOL_HALO:POOL_HALO + nrows, :] = up
    if not with_output:
        up_buf[0:POOL_HALO, :] = up_buf[nrows:nrows + POOL_HALO, :]
        return None

    ug = proj[:, LRU_WIDTH:2 * LRU_WIDTH]
    y_lru = _rms(hs_buf[0:nrows, :] * jax.nn.gelu(ug), lru_gain_ref[...])

    pm = []
    for g, w in enumerate(POOL_WINDOWS):
        lo, hi = g * POOL_GROUP, (g + 1) * POOL_GROUP
        s = up_buf[POOL_HALO - (w - 1) * BATCH:POOL_HALO + nrows, lo:hi]
        m = 1
        while m < w:
            s = s[m * BATCH:, :] + s[:-m * BATCH, :]
            m *= 2
        pm.append(s * (1.0 / w) - up[:, lo:hi])
    up_buf[0:POOL_HALO, :] = up_buf[nrows:nrows + POOL_HALO, :]
    yp = [_bdot(jnp.concatenate(pm[2 * j:2 * j + 2], axis=-1), w_pool_ref[j]) for j in range(2)]
    y_pool = _rms(jnp.concatenate(yp, axis=-1), pool_scale_ref[...])
    between()

    y = _bdot(jnp.concatenate([y_lru, y_pool], axis=-1), w_out_ref[...])
    between()
    return x + y


def _mixer_kernel(x_hbm, meta_ref, g1_ref, w_in_ref, conv_w_ref, conv_b_ref, w_gate_ref,
                  b_gate_ref, lam8_ref, lru_gain_ref, w_pool_ref, pool_scale_ref, w_out_ref,
                  g2_ref, w_rt_ref, b_rt_ref, tri_ref,
                  h1_ref, xs_hbm, cnt_ref,
                  ux_buf, up_buf, a_buf, b_buf, hs_buf, hstate, running,
                  xin, rowbuf, dest_v, dest_s, in_sem, row_sem, idx_sem):
    i = pl.program_id(0)
    slot = i % 2
    n_rows = TOP_K * MIX_ROWS
    mix = functools.partial(
        _mix_rows, g1_ref=g1_ref, w_in_ref=w_in_ref, conv_w_ref=conv_w_ref,
        conv_b_ref=conv_b_ref, w_gate_ref=w_gate_ref, b_gate_ref=b_gate_ref, lam8_ref=lam8_ref,
        lru_gain_ref=lru_gain_ref, w_pool_ref=w_pool_ref, pool_scale_ref=pool_scale_ref,
        w_out_ref=w_out_ref, ux_buf=ux_buf, up_buf=up_buf, a_buf=a_buf, b_buf=b_buf,
        hs_buf=hs_buf, hstate=hstate)
    idx_copy = pltpu.make_async_copy(dest_v, dest_s, idx_sem)
    round_copy = pltpu.make_async_copy(rowbuf, xs_hbm.at[pl.ds(0, n_rows * ROW_SUB)], row_sem)

    def x_copy(step, s, b):
        return pltpu.make_async_copy(x_hbm.at[b, pl.ds(step * MIX_T, MIX_T), :],
                                     xin.at[s, :, b, :], in_sem.at[s])

    def scatter_rows(first, count):
        for n in range(first, first + count):
            k, r = divmod(n, MIX_ROWS)
            d = pl.multiple_of(dest_s[k, r], ROW_SUB)
            pltpu.make_async_copy(rowbuf.at[pl.ds(n * ROW_SUB, ROW_SUB)],
                                  xs_hbm.at[pl.ds(d, ROW_SUB)], row_sem).start(priority=n % 2)

    issued = [0]

    def scatter_batch():
        if issued[0] == 0:
            idx_copy.wait()
        scatter_rows(issued[0], SCATTER_BATCH)
        issued[0] += SCATTER_BATCH

    @pl.when(i == 0)
    def _():
        for b in range(BATCH):
            x_copy(0, 0, b).start()
        ux_buf[0:CONV_HALO, :] = jnp.zeros((CONV_HALO, LRU_WIDTH), jnp.float32)
        up_buf[0:POOL_HALO, :] = jnp.zeros((POOL_HALO, POOL_WIDTH), jnp.float32)
        hstate[...] = jnp.zeros_like(hstate)
        running[...] = jnp.zeros_like(running)
        mix(meta_ref[...], META_ROWS, False)
        rowbuf[...] = jnp.zeros_like(rowbuf)
        col = lax.broadcasted_iota(jnp.int32, (SUBLANES, MIX_ROWS), 1)
        sub = lax.broadcasted_iota(jnp.int32, (SUBLANES, MIX_ROWS), 0)
        dest_v[...] = (XS_DUMP + jnp.minimum(sub, TOP_K - 1) * MIX_ROWS + col) * ROW_SUB
        idx_copy.start()

    @pl.when(i + 1 < MIX_STEPS)
    def _():
        for b in range(BATCH):
            x_copy(i + 1, 1 - slot, b).start()

    for b in range(BATCH):
        x_copy(i, slot, b).wait()

    @pl.when(i >= 0)
    def _():
        x = xin[slot].reshape(MIX_ROWS, D_MODEL)
        h1_ref[...] = mix(x, MIX_ROWS, True, between=scatter_batch)

    h1 = h1_ref[...]
    hn2 = _rms(h1, g2_ref[...])
    logits = _bdot(hn2, w_rt_ref[...]) + b_rt_ref[...]
    scatter_batch()
    lane = lax.broadcasted_iota(jnp.int32, (MIX_ROWS, LANES), 1)
    lane_f = lane.astype(jnp.float32)
    neg = jnp.float32(-jnp.inf)
    big = jnp.float32(4 * LANES)

    is_g = (lane >= N_EXPERTS) & (lane < N_EXPERTS + N_GROUPS)
    gl = jnp.where(is_g, logits, neg)
    gmax = jnp.max(gl, axis=-1, keepdims=True)
    gidx = jnp.min(jnp.where(gl == gmax, lane_f, big), axis=-1, keepdims=True)
    p_g = 1.0 / jnp.sum(jnp.exp(gl - gmax), axis=-1, keepdims=True)
    grp = gidx.astype(jnp.int32) - N_EXPERTS

    el = jnp.where((lane >> 3) == grp, logits, neg)
    m1 = jnp.max(el, axis=-1, keepdims=True)
    i1 = jnp.min(jnp.where(el == m1, lane_f, big), axis=-1, keepdims=True)
    el2 = jnp.where(lane_f == i1, neg, el)
    m2 = jnp.max(el2, axis=-1, keepdims=True)
    i2 = jnp.min(jnp.where(el2 == m2, lane_f, big), axis=-1, keepdims=True)
    e21 = jnp.exp(m2 - m1)
    den = 1.0 + e21
    gate = (p_g * (1.0 / den), p_g * (e21 / den))
    scatter_batch()

    sel = (lane_f == i1, lane_f == i2)
    onehot = jnp.where(sel[0] | sel[1], 1.0, 0.0)
    before = jnp.dot(tri_ref[...], onehot.astype(jnp.bfloat16),
                     preferred_element_type=jnp.float32) + running[0:1, :]
    rank = [jnp.sum(jnp.where(sel[k], before, 0.0), axis=-1, keepdims=True) for k in range(TOP_K)]
    running[0:1, :] = running[0:1, :] + jnp.sum(onehot, axis=0, keepdims=True)
    cnt_ref[...] = running[...]
    scatter_batch()
    assert issued[0] == n_rows

    eid = (i1.astype(jnp.int32), i2.astype(jnp.int32))
    dst = [(eid[k] * EXPERT_CAP + rank[k].astype(jnp.int32)) * ROW_SUB for k in range(TOP_K)]
    dst_tile = jnp.where(lane == 0, dst[0], jnp.where(lane == 1, dst[1], 0))
    dest_v[...] = jnp.transpose(dst_tile)[0:SUBLANES, :]
    idx_copy.start()

    data = _pack_bf16_pairs(hn2[:, :HALF], hn2[:, HALF:])
    tok = i * MIX_ROWS + lax.broadcasted_iota(jnp.int32, (MIX_ROWS, LANES), 0)

    round_copy.wait()
    for k in range(TOP_K):
        base = k * MIX_ROWS * ROW_SUB
        for c in range(DATA_SUB):
            rowbuf[pl.ds(base + c, MIX_ROWS, stride=ROW_SUB), :] = data[:, c * LANES:(c + 1) * LANES]
        ret = (tok * TOP_K + k) * DATA_SUB
        gbits = lax.bitcast_convert_type(jnp.broadcast_to(gate[k], (MIX_ROWS, LANES)), jnp.int32)
        meta = jnp.where(lane == 0, ret, jnp.where(lane == 1, gbits, 0))
        rowbuf[pl.ds(base + DATA_SUB, MIX_ROWS, stride=ROW_SUB), :] = lax.bitcast_convert_type(
            meta, jnp.uint32)

    @pl.when(i == MIX_STEPS - 1)
    def _():
        idx_copy.wait()
        scatter_rows(0, n_rows)
        round_copy.wait()


def _mixer(x, meta_t, g1, w_in, conv_w, conv_b, w_gate, b_gate, lam8, lru_gain, w_pool,
           pool_scale, w_out, g2, w_rt, b_rt, tri):
    full = lambda a: pl.BlockSpec(a.shape, lambda i: (0,) * a.ndim)
    consts = (meta_t, g1, w_in, conv_w, conv_b, w_gate, b_gate, lam8, lru_gain, w_pool,
              pool_scale, w_out, g2, w_rt, b_rt, tri)
    return pl.pallas_call(
        _mixer_kernel,
        grid=(MIX_STEPS,),
        in_specs=[pl.BlockSpec(memory_space=pl.ANY)] + [full(a) for a in consts],
        out_specs=[pl.BlockSpec((MIX_ROWS, D_MODEL), lambda i: (i, 0)),
                   pl.BlockSpec(memory_space=pl.ANY),
                   pl.BlockSpec((SUBLANES, LANES), lambda i: (0, 0))],
        out_shape=[jax.ShapeDtypeStruct((N_TOK, D_MODEL), jnp.float32),
                   jax.ShapeDtypeStruct((XS_ROWS * ROW_SUB, LANES), jnp.uint32),
                   jax.ShapeDtypeStruct((SUBLANES, LANES), jnp.float32)],
        scratch_shapes=[pltpu.VMEM((CONV_HALO + MIX_ROWS, LRU_WIDTH), jnp.float32),
                        pltpu.VMEM((POOL_HALO + MIX_ROWS, POOL_WIDTH), jnp.float32),
                        pltpu.VMEM((MIX_ROWS, LRU_WIDTH), jnp.float32),
                        pltpu.VMEM((MIX_ROWS, LRU_WIDTH), jnp.float32),
                        pltpu.VMEM((MIX_ROWS, LRU_WIDTH), jnp.float32),
                        pltpu.VMEM((SUBLANES, LRU_WIDTH), jnp.float32),
                        pltpu.VMEM((SUBLANES, LANES), jnp.float32),
                        pltpu.VMEM((2, MIX_T, BATCH, D_MODEL), jnp.float32),
                        pltpu.VMEM((TOP_K * MIX_ROWS * ROW_SUB, LANES), jnp.uint32),
                        pltpu.VMEM((SUBLANES, MIX_ROWS), jnp.int32),
                        pltpu.SMEM((SUBLANES, MIX_ROWS), jnp.int32),
                        pltpu.SemaphoreType.DMA((2,)),
                        pltpu.SemaphoreType.DMA,
                        pltpu.SemaphoreType.DMA],
        compiler_params=pltpu.CompilerParams(dimension_semantics=("arbitrary",),
                                             vmem_limit_bytes=VMEM_LIMIT),
        name="mixer",
    )(x, *consts)


def _experts_kernel(blk_ref, bexp_ref, nval_ref, ntot_ref, wslot_ref, wnext_ref,
                    x_ref, wg_hbm, wu_hbm, wd_hbm, y_hbm,
                    wgu_b, wd_b, wg_f, wu_f, wd_f, obuf, addr_v, addr_s, row_sem, idx_sem, w_sem):
    i = pl.program_id(0)
    slot = i % 2
    prev = 1 - slot
    n_used = ntot_ref[0]
    idx_copy = pltpu.make_async_copy(addr_v, addr_s.at[slot], idx_sem)
    prev_idx_copy = pltpu.make_async_copy(addr_v, addr_s.at[prev], idx_sem)

    def round_copy(s):
        return pltpu.make_async_copy(obuf.at[s], y_hbm.at[pl.ds(0, MOE_BM * DATA_SUB)], row_sem.at[s])

    def scatter_rows(first, count):
        for r in range(first, first + count):
            d = pl.multiple_of(addr_s[prev, 0, r], DATA_SUB)
            pltpu.make_async_copy(obuf.at[prev, pl.ds(r * DATA_SUB, DATA_SUB)],
                                  y_hbm.at[pl.ds(d, DATA_SUB)], row_sem.at[prev]).start(priority=r % 2)

    def dump_rows(s):
        return (Y_DUMP + s * MOE_BM + lax.broadcasted_iota(jnp.int32, (MOE_BM, LANES), 0)) * DATA_SUB

    @pl.when(i == 0)
    def _():
        obuf[...] = jnp.zeros_like(obuf)
        addr_v[...] = jnp.transpose(dump_rows(1))[0:SUBLANES, :]
        prev_idx_copy.start()

    def weight_copies(e, s):
        return [pltpu.make_async_copy(src.at[e], dst.at[s], w_sem.at[s])
                for src, dst in ((wg_hbm, wg_f), (wu_hbm, wu_f), (wd_hbm, wd_f))]

    @pl.when((i < n_used) & ((i == 0) | (bexp_ref[i] != bexp_ref[jnp.maximum(i - 1, 0)])))
    def _():
        s = wslot_ref[i]

        @pl.when(i == 0)
        def _():
            for c in weight_copies(bexp_ref[0], s):
                c.start()

        for c in weight_copies(bexp_ref[i], s):
            c.wait()

        @pl.when(wnext_ref[i] < N_EXPERTS)
        def _():
            for c in weight_copies(wnext_ref[i], 1 - s):
                c.start()

        wgu_b[:, 0:EXPERT_FF] = wg_f[s].astype(jnp.bfloat16)
        wgu_b[:, EXPERT_FF:] = wu_f[s].astype(jnp.bfloat16)
        wd_b[...] = wd_f[s].astype(jnp.bfloat16)

    @pl.when(i < n_used)
    def _():
        prev_idx_copy.wait()
        meta = lax.bitcast_convert_type(x_ref[pl.ds(DATA_SUB, MOE_BM, stride=ROW_SUB), :], jnp.int32)
        row = lax.broadcasted_iota(jnp.int32, (MOE_BM, LANES), 0)
        lane = lax.broadcasted_iota(jnp.int32, (MOE_BM, LANES), 1)
        ret = jnp.where(row < nval_ref[i], meta, dump_rows(slot))
        addr_v[...] = jnp.transpose(jnp.where(lane == 0, ret, 0))[0:SUBLANES, :]
        idx_copy.start()

        def compute(nrows):
            batch = MOE_BM // 4
            parts = [_unpack_bf16_pairs(x_ref[pl.ds(c, nrows, stride=ROW_SUB), :]) for c in range(DATA_SUB)]
            x = jnp.concatenate([p[0] for p in parts] + [p[1] for p in parts], axis=-1)
            scatter_rows(0, batch)
            gu = _bdot(x, wgu_b[...])
            scatter_rows(batch, batch)
            g = gu[:, :EXPERT_FF]
            hid = (g * _sigmoid(g)) * gu[:, EXPERT_FF:]
            scatter_rows(2 * batch, batch)
            gate = lax.bitcast_convert_type(meta[0:nrows, 1:2], jnp.float32)
            o = _bdot(hid, wd_b[...]) * gate
            scatter_rows(3 * batch, batch)
            packed = _pack_bf16_pairs(o[:, :HALF], o[:, HALF:])

            @pl.when(i >= 1)
            def _():
                round_copy(slot).wait()

            for c in range(DATA_SUB):
                obuf[slot, pl.ds(c, nrows, stride=DATA_SUB), :] = packed[:, c * LANES:(c + 1) * LANES]

        half_full = nval_ref[i] <= MOE_BM // 2

        @pl.when(half_full)
        def _():
            compute(MOE_BM // 2)

        @pl.when(jnp.logical_not(half_full))
        def _():
            compute(MOE_BM)

    @pl.when(i == n_used)
    def _():
        prev_idx_copy.wait()
        scatter_rows(0, MOE_BM)

    @pl.when(i == pl.num_programs(0) - 1)
    def _():
        round_copy(n_used % 2).wait()
        round_copy((n_used - 1) % 2).wait()


def _experts(blk_idx, blk_expert, blk_nvalid, n_used, blk_wslot, blk_wnext, xs, w_gate, w_up, w_down):
    return pl.pallas_call(
        _experts_kernel,
        grid_spec=pltpu.PrefetchScalarGridSpec(
            num_scalar_prefetch=6,
            grid=(N_BLOCKS + 1,),
            in_specs=[pl.BlockSpec((MOE_BM * ROW_SUB, LANES),
                                   lambda i, blk, bexp, nval, ntot, wslot, wnext: (blk[i], 0)),
                      pl.BlockSpec(memory_space=pl.ANY),
                      pl.BlockSpec(memory_space=pl.ANY),
                      pl.BlockSpec(memory_space=pl.ANY)],
            out_specs=pl.BlockSpec(memory_space=pl.ANY),
            scratch_shapes=[pltpu.VMEM((D_MODEL, 2 * EXPERT_FF), jnp.bfloat16),
                            pltpu.VMEM((EXPERT_FF, D_MODEL), jnp.bfloat16),
                            pltpu.VMEM((2, D_MODEL, EXPERT_FF), jnp.float32),
                            pltpu.VMEM((2, D_MODEL, EXPERT_FF), jnp.float32),
                            pltpu.VMEM((2, EXPERT_FF, D_MODEL), jnp.float32),
                            pltpu.VMEM((2, MOE_BM * DATA_SUB, LANES), jnp.uint32),
                            pltpu.VMEM((SUBLANES, MOE_BM), jnp.int32),
                            pltpu.SMEM((2, SUBLANES, MOE_BM), jnp.int32),
                            pltpu.SemaphoreType.DMA((2,)),
                            pltpu.SemaphoreType.DMA,
                            pltpu.SemaphoreType.DMA((2,))]),
        out_shape=jax.ShapeDtypeStruct((Y_ROWS * DATA_SUB, LANES), jnp.uint32),
        compiler_params=pltpu.CompilerParams(dimension_semantics=("arbitrary",),
                                             vmem_limit_bytes=VMEM_LIMIT),
        name="experts",
    )(blk_idx, blk_expert, blk_nvalid, n_used, blk_wslot, blk_wnext, xs, w_gate, w_up, w_down)


def _final_kernel(h1_ref, y_ref, gain_ref, out_hbm, obuf, sem):
    i = pl.program_id(0)
    slot = i % 2
    stride = TOP_K * DATA_SUB

    def out_copy(step, s, b):
        return pltpu.make_async_copy(obuf.at[s, :, b, :],
                                     out_hbm.at[b, pl.ds(step * FIN_T, FIN_T), :], sem.at[s])

    lo, hi = [], []
    for c in range(DATA_SUB):
        y0 = _unpack_bf16_pairs(y_ref[pl.ds(c, FIN_TM, stride=stride), :])
        y1 = _unpack_bf16_pairs(y_ref[pl.ds(DATA_SUB + c, FIN_TM, stride=stride), :])
        lo.append(y0[0] + y1[0])
        hi.append(y0[1] + y1[1])
    y = jnp.concatenate(lo + hi, axis=-1)
    res = _rms(h1_ref[...] + y, gain_ref[...])

    @pl.when(i >= 2)
    def _():
        for b in range(BATCH):
            out_copy(i - 2, slot, b).wait()

    obuf[slot] = res.reshape(FIN_T, BATCH, D_MODEL)
    for b in range(BATCH):
        out_copy(i, slot, b).start()

    @pl.when(i == pl.num_programs(0) - 1)
    def _():
        for b in range(BATCH):
            out_copy(i - 1, 1 - slot, b).wait()
        for b in range(BATCH):
            out_copy(i, slot, b).wait()


def _final(h1, y2, final_gain):
    rows = FIN_TM * TOP_K * DATA_SUB
    return pl.pallas_call(
        _final_kernel,
        grid=(N_TOK // FIN_TM,),
        in_specs=[pl.BlockSpec((FIN_TM, D_MODEL), lambda i: (i, 0)),
                  pl.BlockSpec((rows, LANES), lambda i: (i, 0)),
                  pl.BlockSpec((1, D_MODEL), lambda i: (0, 0))],
        out_specs=pl.BlockSpec(memory_space=pl.ANY),
        out_shape=jax.ShapeDtypeStruct((BATCH, SEQ, D_MODEL), jnp.float32),
        scratch_shapes=[pltpu.VMEM((2, FIN_T, BATCH, D_MODEL), jnp.float32),
                        pltpu.SemaphoreType.DMA((2,))],
        compiler_params=pltpu.CompilerParams(dimension_semantics=("arbitrary",),
                                             vmem_limit_bytes=VMEM_LIMIT),
        name="final",
    )(h1, y2, final_gain)


def _block_diag(blocks):
    n, r, c = blocks.shape
    on_diag = jnp.eye(n, dtype=bool)[:, None, :, None]
    return jnp.where(on_diag, blocks[:, :, None, :], 0).reshape(n * r, n * c)


def kernel(x, meta_tokens, norm1_gain, w_in, conv_w, conv_b, lru_wa, lru_ba, lru_wx, lru_bx,
           lru_lambda, lru_out_gain, pool_w, pool_scale, w_out, norm2_gain, w_group, b_group,
           w_router, b_router, w_gate, w_up, w_down, final_gain):
    assert x.shape == (BATCH, SEQ, D_MODEL) and norm1_gain.shape[0] == 1
    f32, bf16 = jnp.float32, jnp.bfloat16
    row = lambda v: v.reshape(1, -1).astype(f32)

    meta_t = jnp.repeat(meta_tokens.astype(f32), BATCH, axis=0)

    heads_per = LRU_HEADS // 2
    w_gate_blk = jnp.stack([
        jnp.concatenate([_block_diag(lru_wa[0, j * heads_per:(j + 1) * heads_per]),
                         _block_diag(lru_wx[0, j * heads_per:(j + 1) * heads_per])],
                        axis=1) for j in range(2)]).astype(bf16)
    b_gate_blk = jnp.stack([lru_ba[0], lru_bx[0]]).astype(f32)
    w_pool_blk = jnp.stack([_block_diag(pool_w[0, 2 * j:2 * j + 2])
                            for j in range(2)]).astype(bf16)
    lam8 = row(LRU_C * jax.nn.log_sigmoid(lru_lambda[0].astype(f32)))
    pad = LANES - N_EXPERTS - N_GROUPS
    w_rt = jnp.concatenate([w_router[0], w_group[0], jnp.zeros((D_MODEL, pad), f32)], axis=1).astype(bf16)
    b_rt = jnp.concatenate([b_router[0], b_group[0], jnp.zeros((pad,), f32)]).reshape(1, LANES).astype(f32)
    ridx = jnp.arange(MIX_ROWS)
    tri = (ridx[None, :] < ridx[:, None]).astype(bf16)

    h1, xs, cnt = _mixer(
        x, meta_t, row(norm1_gain[0]), w_in[0].astype(bf16), conv_w[0].astype(f32),
        row(conv_b[0]), w_gate_blk, b_gate_blk, lam8, row(lru_out_gain[0]), w_pool_blk,
        row(pool_scale[0]), w_out[0].astype(bf16), row(norm2_gain[0]), w_rt, b_rt, tri)

    counts = cnt[0, :N_EXPERTS].astype(jnp.int32)
    nblk = (counts + MOE_BM - 1) // MOE_BM
    blk_end = jnp.cumsum(nblk)
    step = jnp.arange(N_BLOCKS + 1, dtype=jnp.int32)
    used = step < blk_end[-1]
    step_c = jnp.minimum(step, blk_end[-1] - 1)
    e_of = jnp.sum(blk_end[None, :] <= step_c[:, None], axis=-1).astype(jnp.int32)
    onehot_e = e_of[:, None] == jnp.arange(N_EXPERTS, dtype=jnp.int32)[None, :]
    j_of = step_c - jnp.sum(jnp.where(onehot_e, (blk_end - nblk)[None, :], 0), axis=-1)
    cnt_of = jnp.sum(jnp.where(onehot_e, counts[None, :], 0), axis=-1)
    blk_idx = (e_of * CAP_BLOCKS + j_of).astype(jnp.int32)
    blk_nvalid = jnp.where(used, jnp.clip(cnt_of - j_of * MOE_BM, 0, MOE_BM), 0).astype(jnp.int32)

    eidx = jnp.arange(N_EXPERTS, dtype=jnp.int32)
    present = nblk > 0
    slot_e = (jnp.cumsum(present) - 1) % 2
    later = present[None, :] & (eidx[None, :] > eidx[:, None])
    next_e = jnp.min(jnp.where(later, eidx[None, :], N_EXPERTS), axis=-1)
    blk_wslot = jnp.sum(jnp.where(onehot_e, slot_e[None, :], 0), axis=-1).astype(jnp.int32)
    blk_wnext = jnp.sum(jnp.where(onehot_e, next_e[None, :], 0), axis=-1).astype(jnp.int32)

    y2 = _experts(blk_idx, e_of, blk_nvalid, blk_end[-1:].astype(jnp.int32), blk_wslot, blk_wnext, xs,
                  w_gate[0], w_up[0], w_down[0])
    return _final(h1, y2, row(final_gain))
```
